```python
import jax, jax.numpy as jnp
from jax import lax
import numpy as np

D_MODEL = 1024
BATCH = 8
SEQ = 4096
DEPTH = 1

N_HEADS_ATTN = 8
N_KV_HEADS = 2
GQA_GROUP = N_HEADS_ATTN // N_KV_HEADS
HEAD_DIM = 64
D_ATTN = N_HEADS_ATTN * HEAD_DIM
D_KV = N_KV_HEADS * HEAD_DIM
N_CONV_GROUPS = 8
D_CONV = D_MODEL - D_ATTN
CONV_WIDTH = 3
CMP_BLOCK = 32
CMP_STRIDE = 16
CMP_HIDDEN = 2 * HEAD_DIM
SEL_BLOCK = 64
SEL_TOPK = 16
WINDOW = 512
Q_CHUNK = 64
D_FF = 2816
FFN_CONV_WIDTH = 3
EPS = 1e-6
NEG = -1e30
BIG = 1e30
PROJ_WIDTHS = [D_ATTN] + [D_KV] * 6 + [3 * N_HEADS_ATTN] + [D_CONV] * 3
W_IN_COLS = sum(PROJ_WIDTHS)

kernel_name = "hymba_nsa_shortconv_convffn"


def rmsnorm(x, g):
    xf = x.astype(jnp.float32)
    y = xf * lax.rsqrt(jnp.mean(xf * xf, axis=-1, keepdims=True) + EPS)
    return (y * g.astype(jnp.float32)).astype(x.dtype)


def causal_dwconv(u, w):
    T = u.shape[1]
    W = w.shape[0]
    up = jnp.pad(u, ((0, 0), (W - 1, 0), (0, 0)))
    y = w[0] * up[:, 0:T]
    for k in range(1, W):
        y = y + w[k] * up[:, k:k + T]
    return y


def alibi_slopes(n):
    return jnp.exp2(-8.0 * jnp.arange(1, n + 1, dtype=jnp.float32) / n)


def masked_softmax(s, mask):
    s = jnp.where(mask, s, NEG)
    p = jax.nn.softmax(s, axis=-1)
    return jnp.where(mask, p, 0.0)


def compress_blocks(k, pos, w1, w2):
    B, T, H, dk = k.shape
    n_cmp = (T - CMP_BLOCK) // CMP_STRIDE + 1
    idx = np.arange(n_cmp)[:, None] * CMP_STRIDE + np.arange(CMP_BLOCK)[None, :]
    blk = k[:, idx] + pos[None, None, :, None, :]
    blk = blk.transpose(0, 3, 1, 2, 4).reshape(B, H, n_cmp, CMP_BLOCK * dk)
    return jax.nn.gelu(blk @ w1) @ w2


def cmp_to_sel_map(T):
    n_cmp = (T - CMP_BLOCK) // CMP_STRIDE + 1
    n_sel = T // SEL_BLOCK
    cs = np.arange(n_cmp)[:, None] * CMP_STRIDE
    ss = np.arange(n_sel)[None, :] * SEL_BLOCK
    ov = np.maximum(0, np.minimum(cs + CMP_BLOCK, ss + SEL_BLOCK) - np.maximum(cs, ss))
    return jnp.asarray((ov / CMP_BLOCK).astype(np.float32))


def nsa_attention(q, k_cmp, v_cmp, k_slc, v_slc, k_win, v_win, gates):
    B, H, G, T, dk = q.shape
    n_cmp = k_cmp.shape[2]
    n_sel = T // SEL_BLOCK
    topk = min(SEL_TOPK, n_sel)
    scale = dk ** -0.5
    slopes5 = alibi_slopes(N_HEADS_ATTN).reshape(1, H, G, 1, 1)
    slopes6 = slopes5[..., None]
    cmp_end = jnp.arange(n_cmp) * CMP_STRIDE + (CMP_BLOCK - 1)
    m_sel = cmp_to_sel_map(T)
    k_blocks = k_slc.reshape(B, H, n_sel, SEL_BLOCK, dk)
    v_blocks = v_slc.reshape(B, H, n_sel, SEL_BLOCK, dk)
    k_win_p = jnp.pad(k_win, ((0, 0), (0, 0), (WINDOW, 0), (0, 0)))
    v_win_p = jnp.pad(v_win, ((0, 0), (0, 0), (WINDOW, 0), (0, 0)))
    b_idx = jnp.arange(B)[:, None, None, None]
    h_idx = jnp.arange(H)[None, :, None, None]
    sel_j = jnp.arange(n_sel)

    def chunk(q0):
        t = q0 + jnp.arange(Q_CHUNK)
        qc = lax.dynamic_slice_in_dim(q, q0, Q_CHUNK, axis=3) * scale
        gc = lax.dynamic_slice_in_dim(gates, q0, Q_CHUNK, axis=3)
        d_c = (t[:, None] - cmp_end[None, :]).astype(jnp.float32)
        s_c = jnp.einsum('bkgqd,bknd->bkgqn', qc, k_cmp).astype(jnp.float32) - slopes5 * d_c
        p_c = masked_softmax(s_c, d_c >= 0)
        o_c = jnp.einsum('bkgqn,bknd->bkgqd', p_c.astype(v_cmp.dtype), v_cmp)
        imp = jnp.einsum('bkgqn,nj->bkqj', p_c, m_sel)
        jt = t // SEL_BLOCK
        forced = (sel_j[None, :] == 0) | (sel_j[None, :] == jt[:, None]) | (sel_j[None, :] == jt[:, None] - 1)
        imp = jnp.where(forced, BIG, imp)
        imp = jnp.where(sel_j[None, :] > jt[:, None], NEG, imp)
        _, idx = lax.top_k(imp, topk)
        ks = k_blocks[b_idx, h_idx, idx]
        vs = v_blocks[b_idx, h_idx, idx]
        s_pos = idx[..., None] * SEL_BLOCK + jnp.arange(SEL_BLOCK)
        d_s = (t[None, None, :, None, None] - s_pos).astype(jnp.float32)[:, :, None]
        s_s = jnp.einsum('bkgqd,bkqnsd->bkgqns', qc, ks).astype(jnp.float32) - slopes6 * d_s
        n_tok = topk * SEL_BLOCK
        s_s = s_s.reshape(B, H, G, Q_CHUNK, n_tok)
        p_s = masked_softmax(s_s, (d_s >= 0).reshape(B, H, 1, Q_CHUNK, n_tok))
        o_s = jnp.einsum('bkgqm,bkqmd->bkgqd', p_s.astype(vs.dtype), vs.reshape(B, H, Q_CHUNK, n_tok, dk))
        kw = lax.dynamic_slice_in_dim(k_win_p, q0, Q_CHUNK + WINDOW, axis=2)
        vw = lax.dynamic_slice_in_dim(v_win_p, q0, Q_CHUNK + WINDOW, axis=2)
        w_pos = q0 - WINDOW + jnp.arange(Q_CHUNK + WINDOW)
        d_w = t[:, None] - w_pos[None, :]
        m_w = (d_w >= 0) & (d_w < WINDOW) & (w_pos[None, :] >= 0)
        s_w = jnp.einsum('bkgqd,bkmd->bkgqm', qc, kw).astype(jnp.float32) - slopes5 * d_w.astype(jnp.float32)
        p_w = masked_softmax(s_w, m_w)
        o_w = jnp.einsum('bkgqm,bkmd->bkgqd', p_w.astype(vw.dtype), vw)
        return gc[..., 0:1] * o_c + gc[..., 1:2] * o_s + gc[..., 2:3] * o_w

    starts = jnp.arange(T // Q_CHUNK) * Q_CHUNK
    outs = lax.map(chunk, starts)
    return outs.transpose(1, 0, 4, 2, 3, 5).reshape(B, T, H * G * dk)


def setup_inputs(seed: int = 0) -> dict:
    key = jax.random.key(seed)
    ks = jax.random.split(key, 24)
    nrm = lambda k, shape, s: jax.random.normal(k, shape, jnp.float32) * s
    gain = lambda k, shape: 1.0 + 0.02 * jax.random.normal(k, shape, jnp.float32)
    L = DEPTH
    return {
        "x": nrm(ks[0], (BATCH, SEQ, D_MODEL), 1.0),
        "norm_mix_g": gain(ks[1], (L, D_MODEL)),
        "w_in": nrm(ks[2], (L, D_MODEL, W_IN_COLS), D_MODEL ** -0.5),
        "pos_ck": nrm(ks[3], (L, CMP_BLOCK, HEAD_DIM), 0.1),
        "w_ck1": nrm(ks[4], (L, CMP_BLOCK * HEAD_DIM, CMP_HIDDEN), (CMP_BLOCK * HEAD_DIM) ** -0.5),
        "w_ck2": nrm(ks[5], (L, CMP_HIDDEN, HEAD_DIM), CMP_HIDDEN ** -0.5),
        "pos_cv": nrm(ks[6], (L, CMP_BLOCK, HEAD_DIM), 0.1),
        "w_cv1": nrm(ks[7], (L, CMP_BLOCK * HEAD_DIM, CMP_HIDDEN), (CMP_BLOCK * HEAD_DIM) ** -0.5),
        "w_cv2": nrm(ks[8], (L, CMP_HIDDEN, HEAD_DIM), CMP_HIDDEN ** -0.5),
        "conv_mix_w": nrm(ks[9], (L, CONV_WIDTH, D_CONV), CONV_WIDTH ** -0.5),
        "norm_out_attn_g": gain(ks[10], (L, D_ATTN)),
        "norm_out_conv_g": gain(ks[11], (L, D_CONV)),
        "w_out": nrm(ks[12], (L, D_ATTN + D_CONV, D_MODEL), (D_ATTN + D_CONV) ** -0.5),
        "norm_ffn_g": gain(ks[13], (L, D_MODEL)),
        "w_gate": nrm(ks[14], (L, D_MODEL, D_FF), D_MODEL ** -0.5),
        "w_up": nrm(ks[15], (L, D_MODEL, D_FF), D_MODEL ** -0.5),
        "ffn_conv_w": nrm(ks[16], (L, FFN_CONV_WIDTH, D_FF), FFN_CONV_WIDTH ** -0.5),
        "ffn_conv_b": nrm(ks[17], (L, D_FF), 0.01),
        "w_down": nrm(ks[18], (L, D_FF, D_MODEL), D_FF ** -0.5),
        "norm_final_g": gain(ks[19], (D_MODEL,)),
    }


def reference(x, norm_mix_g, w_in, pos_ck, w_ck1, w_ck2, pos_cv, w_cv1, w_cv2, conv_mix_w,
              norm_out_attn_g, norm_out_conv_g, w_out, norm_ffn_g, w_gate, w_up,
              ffn_conv_w, ffn_conv_b, w_down, norm_final_g):
    B, T, _ = x.shape
    split_at = [int(v) for v in np.cumsum(PROJ_WIDTHS)[:-1]]
    H, G, dk = N_KV_HEADS, GQA_GROUP, HEAD_DIM
    for l in range(DEPTH):
        h = rmsnorm(x, norm_mix_g[l])
        proj = h @ w_in[l]
        q, k_c, v_c, k_s, v_s, k_w, v_w, g_lin, b_g, c_g, u = jnp.split(proj, split_at, axis=-1)
        q = q.reshape(B, T, H, G, dk).transpose(0, 2, 3, 1, 4)
        kv_heads = lambda a: a.reshape(B, T, H, dk)
        k_cmp = compress_blocks(kv_heads(k_c), pos_ck[l], w_ck1[l], w_ck2[l])
        v_cmp = compress_blocks(kv_heads(v_c), pos_cv[l], w_cv1[l], w_cv2[l])
        to_bhtd = lambda a: kv_heads(a).transpose(0, 2, 1, 3)
        gates = jax.nn.sigmoid(g_lin).reshape(B, T, H, G, 3).transpose(0, 2, 3, 1, 4)
        o_attn = nsa_attention(q, k_cmp, v_cmp, to_bhtd(k_s), to_bhtd(v_s),
                               to_bhtd(k_w), to_bhtd(v_w), gates)
        o_conv = b_g * causal_dwconv(c_g * u, conv_mix_w[l])
        mixed = jnp.concatenate([rmsnorm(o_attn, norm_out_attn_g[l]),
                                 rmsnorm(o_conv, norm_out_conv_g[l])], axis=-1)
        x = x + mixed @ w_out[l]
        h2 = rmsnorm(x, norm_ffn_g[l])
        gate = causal_dwconv(h2 @ w_gate[l], ffn_conv_w[l]) + ffn_conv_b[l]
        x = x + (jax.nn.silu(gate) * (h2 @ w_up[l])) @ w_down[l]
    return rmsnorm(x, norm_final_g)
```

```python
import functools

import jax
import jax.numpy as jnp
import numpy as np
from jax import lax
from jax.experimental import pallas as pl
from jax.experimental.pallas import tpu as pltpu

F32 = jnp.float32
BF16 = jnp.bfloat16

D_MODEL = 1024
N_KV_HEADS = 2
GQA_GROUP = 4
N_HEADS_ATTN = N_KV_HEADS * GQA_GROUP
HEAD_DIM = 64
D_ATTN = N_HEADS_ATTN * HEAD_DIM
D_KV = N_KV_HEADS * HEAD_DIM
D_CONV = D_MODEL - D_ATTN
CMP_BLOCK = 32
CMP_STRIDE = 16
CMP_HIDDEN = 2 * HEAD_DIM
SEL_BLOCK = 64
SEL_TOPK = 16
WINDOW = 512
D_FF = 2816
EPS = 1e-6
NEG = -1e30
BIG = 1e30

AUG_K = 256
AUG_ALIBI = HEAD_DIM
AUG_ALIBI_ROWS = 16
AUG_SEL = AUG_ALIBI + AUG_ALIBI_ROWS
V_ROWS = 80

ROW_TILE = 512
Q_TILE = 256
FF_CHUNK = 256
PROJ_COLS = 2944
VMEM_LIMIT = 56 * 1024 * 1024


def _rms(x, g):
    return x * lax.rsqrt(jnp.mean(x * x, axis=-1, keepdims=True) + EPS) * g


def _proj_kernel(x_ref, g_ref, w_ref, cw_ref, gc_ref, q_ref, kv_ref, gate_ref, mc_ref, cbuf, *, tiles_per_seq):
    i = pl.program_id(0)
    tm = x_ref.shape[0]
    h = _rms(x_ref[...], g_ref[...])
    p = jnp.dot(h.astype(BF16), w_ref[...], preferred_element_type=F32)
    q_ref[...] = p[:, 0:512].astype(BF16)
    kv_ref[...] = p[:, 512:1280].astype(BF16)
    b = p[:, 1280:1792]
    cu = p[:, 1792:2304] * p[:, 2304:2816]
    gate_ref[...] = jax.nn.sigmoid(p[:, 2816:2944])

    @pl.when(i % tiles_per_seq == 0)
    def _():
        cbuf[0:8, :] = jnp.zeros((8, D_CONV), F32)

    @pl.when(i % tiles_per_seq != 0)
    def _():
        cbuf[0:8, :] = cbuf[tm:tm + 8, :]

    cbuf[8:tm + 8, :] = cu
    y = cw_ref[0:1, :] * cbuf[6:tm + 6, :] + cw_ref[1:2, :] * cbuf[7:tm + 7, :] + cw_ref[2:3, :] * cu
    mc_ref[...] = _rms(b * y, gc_ref[...]).astype(BF16)


def _proj_call(x2, g, w, cw, gc, seq):
    n = x2.shape[0]
    tm = ROW_TILE
    row = lambda i: (i, 0)
    fix = lambda i: (0, 0)
    return pl.pallas_call(
        functools.partial(_proj_kernel, tiles_per_seq=seq // tm),
        grid=(n // tm,),
        in_specs=[
            pl.BlockSpec((tm, D_MODEL), row),
            pl.BlockSpec((1, D_MODEL), fix),
            pl.BlockSpec((D_MODEL, PROJ_COLS), fix),
            pl.BlockSpec((3, D_CONV), fix),
            pl.BlockSpec((1, D_CONV), fix),
        ],
        out_specs=[
            pl.BlockSpec((tm, D_ATTN), row),
            pl.BlockSpec((tm, 6 * D_KV), row),
            pl.BlockSpec((tm, 128), row),
            pl.BlockSpec((tm, D_CONV), row),
        ],
        out_shape=[
            jax.ShapeDtypeStruct((n, D_ATTN), BF16),
            jax.ShapeDtypeStruct((n, 6 * D_KV), BF16),
            jax.ShapeDtypeStruct((n, 128), F32),
            jax.ShapeDtypeStruct((n, D_CONV), BF16),
        ],
        scratch_shapes=[pltpu.VMEM((tm + 8, D_CONV), F32)],
        compiler_params=pltpu.CompilerParams(dimension_semantics=("arbitrary",), vmem_limit_bytes=VMEM_LIMIT),
    )(x2, g, w, cw, gc)


def _compress_kernel(x_ref, w1_ref, pos_ref, w2_ref, o_ref, sbuf):
    nc = x_ref.shape[0]
    y = jnp.dot(x_ref[...], w1_ref[...], preferred_element_type=F32)
    pb = jnp.dot(pos_ref[...], w1_ref[...], preferred_element_type=F32)
    posb = pb[0:1, 0:CMP_HIDDEN] + pb[1:2, CMP_HIDDEN:2 * CMP_HIDDEN]
    sbuf[0:nc, :] = y[:, CMP_HIDDEN:2 * CMP_HIDDEN]
    sbuf[nc:nc + 8, :] = jnp.zeros((8, CMP_HIDDEN), F32)
    hid = y[:, 0:CMP_HIDDEN] + sbuf[1:nc + 1, :] + posb
    act = jax.nn.gelu(hid)
    out = jnp.dot(act.astype(BF16), w2_ref[...], preferred_element_type=F32)
    rowi = lax.broadcasted_iota(jnp.int32, out.shape, 0)
    o_ref[...] = jnp.where(rowi < nc - 1, out, 0.0)


def _compress_call(xc, w1ab, pos8, w2):
    b, _, nc, _ = xc.shape
    return pl.pallas_call(
        _compress_kernel,
        grid=(b, 4),
        in_specs=[
            pl.BlockSpec((None, None, nc, 16 * HEAD_DIM), lambda i, s: (i, s, 0, 0)),
            pl.BlockSpec((None, 16 * HEAD_DIM, 2 * CMP_HIDDEN), lambda i, s: (s // 2, 0, 0)),
            pl.BlockSpec((None, 8, 16 * HEAD_DIM), lambda i, s: (s // 2, 0, 0)),
            pl.BlockSpec((None, CMP_HIDDEN, HEAD_DIM), lambda i, s: (s // 2, 0, 0)),
        ],
        out_specs=pl.BlockSpec((None, None, nc, HEAD_DIM), lambda i, s: (i, s, 0, 0)),
        out_shape=jax.ShapeDtypeStruct((b, 4, nc, HEAD_DIM), F32),
        scratch_shapes=[pltpu.VMEM((nc + 8, CMP_HIDDEN), F32)],
        compiler_params=pltpu.CompilerParams(dimension_semantics=("arbitrary", "arbitrary")),
    )(xc, w1ab, pos8, w2)


def _attn_kernel(qt_ref, gt_ref, ks_ref, kw_ref, vs_ref, vw_ref, kc_ref, vc_ref, msel_ref, o_ref,
                 qaug, acc_ref, *, topk):
    h = pl.program_id(1)
    qi = pl.program_id(2)
    tq = qt_ref.shape[1]
    r = GQA_GROUP * tq
    nc = kc_ref.shape[0]
    nsel = msel_ref.shape[0]
    t0 = qi * tq

    lane16 = lax.broadcasted_iota(jnp.int32, (AUG_ALIBI_ROWS, r), 1)
    sub16 = lax.broadcasted_iota(jnp.int32, (AUG_ALIBI_ROWS, r), 0)
    gl = lane16 // tq
    off = (lane16 % tq).astype(F32)
    base = jnp.where(h == 0, 0.5, 0.03125).astype(F32)
    slope = jnp.where(gl == 0, base, jnp.where(gl == 1, base * 0.5, jnp.where(gl == 2, base * 0.25, base * 0.125)))
    blk0 = (t0 // SEL_BLOCK).astype(F32)
    arow = jnp.where(sub16 == 0, slope,
                     jnp.where(sub16 == 1, 64.0 * slope,
                               jnp.where(sub16 == 2, -64.0 * slope * blk0,
                                         jnp.where(sub16 == 3, -slope * off, 0.0))))
    for g in range(GQA_GROUP):
        qaug[0:HEAD_DIM, g * tq:(g + 1) * tq] = qt_ref[g * HEAD_DIM:(g + 1) * HEAD_DIM, :]
    qaug[AUG_ALIBI:AUG_SEL, :] = arow.astype(BF16)
    qaug[AUG_SEL:AUG_K, :] = jnp.zeros((AUG_K - AUG_SEL, r), BF16)

    lane_q = lax.broadcasted_iota(jnp.int32, (1, r), 1) % tq

    sc = jnp.dot(kc_ref[...], qaug[...], preferred_element_type=F32)
    cend = lax.broadcasted_iota(jnp.int32, (nc, r), 0) * CMP_STRIDE + (CMP_BLOCK - 1)
    mask_c = cend <= (t0 + lane_q)
    sc = jnp.where(mask_c, sc, NEG)
    m_c = jnp.max(sc, axis=0, keepdims=True)
    e_c = jnp.where(mask_c, jnp.exp(sc - m_c), 0.0)
    l_c = jnp.sum(e_c, axis=0, keepdims=True)
    p_c = e_c * jnp.where(l_c > 0.0, 1.0 / l_c, 0.0)
    o_cmp = jnp.dot(vc_ref[...], p_c.astype(BF16), preferred_element_type=F32)

    p_sum = p_c[:, 0:tq]
    for g in range(1, GQA_GROUP):
        p_sum = p_sum + p_c[:, g * tq:(g + 1) * tq]
    p1 = p_sum.astype(BF16)
    r1 = p_sum - p1.astype(F32)
    p2 = r1.astype(BF16)
    p3 = (r1 - p2.astype(F32)).astype(BF16)
    msel = msel_ref[...]
    imp = (jnp.dot(msel, p1, preferred_element_type=F32) + jnp.dot(msel, p2, preferred_element_type=F32)
           + jnp.dot(msel, p3, preferred_element_type=F32))
    jj = lax.broadcasted_iota(jnp.int32, (nsel, tq), 0)
    jt = (t0 + lax.broadcasted_iota(jnp.int32, (nsel, tq), 1)) // SEL_BLOCK
    imp = jnp.where((jj == 0) | (jj == jt) | (jj == jt - 1), BIG, imp)
    imp = jnp.where(jj > jt, NEG, imp)
    cnt = jnp.zeros((nsel, tq), jnp.int32)
    for jp in range(nsel):
        rowv = imp[jp:jp + 1, :]
        beats = (rowv > imp) | ((rowv == imp) & (jj > jp))
        cnt = cnt + jnp.where(beats, 1, 0)
    selbias = jnp.where(cnt < topk, 0.0, NEG).astype(BF16)
    for g in range(GQA_GROUP):
        qaug[AUG_SEL:AUG_SEL + nsel, g * tq:(g + 1) * tq] = selbias

    q_all = qaug[...]
    key_i = lax.broadcasted_iota(jnp.int32, (tq, r), 0)
    causal = key_i <= lane_q

    acc_ref[...] = jnp.zeros(acc_ref.shape, F32)

    def sel_tile(kb, m_old, diag):
        s = jnp.dot(ks_ref[kb], q_all, preferred_element_type=F32)
        if diag:
            s = jnp.where(causal, s, NEG)
        m_new = jnp.maximum(m_old, jnp.max(s, axis=0, keepdims=True))
        p = jnp.exp(s - m_new)
        alpha = jnp.exp(m_old - m_new)
        acc_ref[...] = acc_ref[...] * alpha + jnp.dot(vs_ref[kb], p.astype(BF16), preferred_element_type=F32)
        return m_new

    m_s = lax.fori_loop(0, qi, lambda kb, m: sel_tile(kb, m, False), jnp.full((1, r), NEG, F32))
    sel_tile(qi, m_s, True)
    acc_s = acc_ref[...]
    o_sel = acc_s[0:HEAD_DIM, :] * (1.0 / acc_s[HEAD_DIM:HEAD_DIM + 1, :])

    nwin = WINDOW // tq
    scores = []
    for w in range(nwin + 1):
        kb = qi - nwin + w
        kbc = jnp.maximum(kb, 0)
        s = jnp.dot(kw_ref[kbc], q_all, preferred_element_type=F32)
        if w == 0:
            s = jnp.where((key_i > lane_q) & (kb >= 0), s, NEG)
        elif w == nwin:
            s = jnp.where(causal, s, NEG)
        else:
            s = jnp.where(kb >= 0, s, NEG)
        scores.append((kbc, s))
    m_w = scores[0][1].max(axis=0, keepdims=True)
    for _, s in scores[1:]:
        m_w = jnp.maximum(m_w, s.max(axis=0, keepdims=True))
    acc_w = jnp.zeros((V_ROWS, r), F32)
    for kbc, s in scores:
        acc_w = acc_w + jnp.dot(vw_ref[kbc], jnp.exp(s - m_w).astype(BF16), preferred_element_type=F32)
    o_win = acc_w[0:HEAD_DIM, :] * (1.0 / acc_w[HEAD_DIM:HEAD_DIM + 1, :])

    gt = gt_ref[...]
    for g in range(GQA_GROUP):
        sl = slice(g * tq, (g + 1) * tq)
        o_ref[g * HEAD_DIM:(g + 1) * HEAD_DIM, :] = (gt[3 * g:3 * g + 1, :] * o_cmp[:, sl]
                                                     + gt[3 * g + 1:3 * g + 2, :] * o_sel[:, sl]
                                                     + gt[3 * g + 2:3 * g + 3, :] * o_win[:, sl])


def _attn_call(qt, gt, ks, kw, vs, vw, kc, vc, msel, topk):
    b, hh, _, t = qt.shape
    tq = Q_TILE
    nkb = t // tq
    nc = kc.shape[2]
    nsel = msel.shape[0]
    per_q = lambda i, j, k: (i, j, 0, k)
    per_bh4 = lambda i, j, k: (i, j, 0, 0)
    per_bh5 = lambda i, j, k: (i, j, 0, 0, 0)
    return pl.pallas_call(
        functools.partial(_attn_kernel, topk=topk),
        grid=(b, hh, nkb),
        in_specs=[
            pl.BlockSpec((None, None, GQA_GROUP * HEAD_DIM, tq), per_q),
            pl.BlockSpec((None, None, 16, tq), per_q),
            pl.BlockSpec((None, None, nkb, tq, AUG_K), per_bh5),
            pl.BlockSpec((None, None, nkb, tq, AUG_K), per_bh5),
            pl.BlockSpec((None, None, nkb, V_ROWS, tq), per_bh5),
            pl.BlockSpec((None, None, nkb, V_ROWS, tq), per_bh5),
            pl.BlockSpec((None, None, nc, AUG_K), per_bh4),
            pl.BlockSpec((None, None, HEAD_DIM, nc), per_bh4),
            pl.BlockSpec((nsel, nc), lambda i, j, k: (0, 0)),
        ],
        out_specs=pl.BlockSpec((None, None, GQA_GROUP * HEAD_DIM, tq), per_q),
        out_shape=jax.ShapeDtypeStruct((b, hh, GQA_GROUP * HEAD_DIM, t), F32),
        scratch_shapes=[pltpu.VMEM((AUG_K, GQA_GROUP * tq), BF16), pltpu.VMEM((V_ROWS, GQA_GROUP * tq), F32)],
        compiler_params=pltpu.CompilerParams(dimension_semantics=("arbitrary", "arbitrary", "arbitrary"),
                                             vmem_limit_bytes=VMEM_LIMIT),
    )(qt, gt, ks, kw, vs, vw, kc, vc, msel)


def _ffn_kernel(x_ref, oa_ref, mc_ref, ga_ref, wo_ref, gf_ref, wg_ref, wu_ref, cw_ref, cb_ref, wd_ref, gl_ref,
                o_ref, gbuf, ybuf, *, tiles_per_seq):
    i = pl.program_id(0)
    tm = x_ref.shape[0]
    ma = _rms(oa_ref[...], ga_ref[...]).astype(BF16)
    x1 = (x_ref[...] + jnp.dot(ma, wo_ref[0:D_ATTN, :], preferred_element_type=F32)
          + jnp.dot(mc_ref[...], wo_ref[D_ATTN:D_MODEL, :], preferred_element_type=F32))
    h2 = _rms(x1, gf_ref[...]).astype(BF16)

    @pl.when(i % tiles_per_seq == 0)
    def _():
        gbuf[0:8, :] = jnp.zeros((8, D_FF), F32)

    @pl.when(i % tiles_per_seq != 0)
    def _():
        gbuf[0:8, :] = gbuf[tm:tm + 8, :]

    for c in range(D_FF // FF_CHUNK):
        cs = slice(c * FF_CHUNK, (c + 1) * FF_CHUNK)
        gpre = jnp.dot(h2, wg_ref[c], preferred_element_type=F32)
        up = jnp.dot(h2, wu_ref[c], preferred_element_type=F32)
        gbuf[8:tm + 8, cs] = gpre
        gate = (cw_ref[0:1, cs] * gbuf[6:tm + 6, cs] + cw_ref[1:2, cs] * gbuf[7:tm + 7, cs]
                + cw_ref[2:3, cs] * gpre + cb_ref[:, cs])
        ybuf[:, cs] = (jax.nn.silu(gate) * up).astype(BF16)
    acc = x1 + jnp.dot(ybuf[...], wd_ref[...], preferred_element_type=F32)
    o_ref[...] = _rms(acc, gl_ref[...])


def _ffn_call(x2, oa, mc, ga, wo, gf, wg, wu, cw, cb, wd, gl, seq):
    n = x2.shape[0]
    tm = ROW_TILE
    nch = D_FF // FF_CHUNK
    row = lambda i: (i, 0)
    fix = lambda i: (0, 0)
    fix3 = lambda i: (0, 0, 0)
    once = dict(pipeline_mode=pl.Buffered(1))
    return pl.pallas_call(
        functools.partial(_ffn_kernel, tiles_per_seq=seq // tm),
        grid=(n // tm,),
        in_specs=[
            pl.BlockSpec((tm, D_MODEL), row),
            pl.BlockSpec((tm, D_ATTN), row),
            pl.BlockSpec((tm, D_CONV), row),
            pl.BlockSpec((1, D_ATTN), fix),
            pl.BlockSpec((D_MODEL, D_MODEL), fix, **once),
            pl.BlockSpec((1, D_MODEL), fix),
            pl.BlockSpec((nch, D_MODEL, FF_CHUNK), fix3, **once),
            pl.BlockSpec((nch, D_MODEL, FF_CHUNK), fix3, **once),
            pl.BlockSpec((3, D_FF), fix),
            pl.BlockSpec((1, D_FF), fix),
            pl.BlockSpec((D_FF, D_MODEL), fix, **once),
            pl.BlockSpec((1, D_MODEL), fix),
        ],
        out_specs=pl.BlockSpec((tm, D_MODEL), row),
        out_shape=jax.ShapeDtypeStruct((n, D_MODEL), F32),
        scratch_shapes=[pltpu.VMEM((tm + 8, D_FF), F32), pltpu.VMEM((tm, D_FF), BF16)],
        compiler_params=pltpu.CompilerParams(dimension_semantics=("arbitrary",), vmem_limit_bytes=VMEM_LIMIT),
    )(x2, oa, mc, ga, wo, gf, wg, wu, cw, cb, wd, gl)


def _alibi_cols(pos):
    cols = np.zeros((len(pos), AUG_ALIBI_ROWS), np.float32)
    cols[:, 0] = pos % SEL_BLOCK
    cols[:, 1] = pos // SEL_BLOCK
    cols[:, 2] = 1.0
    cols[:, 3] = 1.0
    return cols


def _key_consts(t, nsel, with_sel):
    pos = np.arange(t)
    c = np.zeros((t, AUG_K - HEAD_DIM), np.float32)
    c[:, 0:AUG_ALIBI_ROWS] = _alibi_cols(pos)
    if with_sel:
        c[pos, AUG_ALIBI_ROWS + pos // SEL_BLOCK] = 1.0
    return jnp.asarray(c, BF16)


def _cmp_consts(nc):
    c = np.zeros((nc, AUG_K - HEAD_DIM), np.float32)
    c[:, 0:AUG_ALIBI_ROWS] = _alibi_cols(np.arange(nc) * CMP_STRIDE + (CMP_BLOCK - 1))
    return jnp.asarray(c, BF16)


def _sel_map_t(t, nc):
    n_cmp = (t - CMP_BLOCK) // CMP_STRIDE + 1
    n_sel = t // SEL_BLOCK
    cs = np.arange(n_cmp)[:, None] * CMP_STRIDE
    ss = np.arange(n_sel)[None, :] * SEL_BLOCK
    ov = np.maximum(0, np.minimum(cs + CMP_BLOCK, ss + SEL_BLOCK) - np.maximum(cs, ss)) / CMP_BLOCK
    m = np.zeros((n_sel, nc), np.float32)
    m[:, :n_cmp] = ov.T
    return jnp.asarray(m, BF16)


def kernel(x, norm_mix_g, w_in, pos_ck, w_ck1, w_ck2, pos_cv, w_cv1, w_cv2, conv_mix_w, norm_out_attn_g,
           norm_out_conv_g, w_out, norm_ffn_g, w_gate, w_up, ffn_conv_w, ffn_conv_b, w_down, norm_final_g):
    b, t, _ = x.shape
    hh, dk = N_KV_HEADS, HEAD_DIM
    assert t % ROW_TILE == 0 and t % Q_TILE == 0 and WINDOW % Q_TILE == 0 and t // SEL_BLOCK <= AUG_K - AUG_SEL
    nc = t // CMP_STRIDE
    nsel = t // SEL_BLOCK
    nkb = t // Q_TILE
    depth = w_in.shape[0]
    assert depth == 1
    xx = x.reshape(b * t, D_MODEL)
    for l in range(depth):
        wi = w_in[l]
        w_p = jnp.concatenate([wi[:, 0:512] * (dk ** -0.5), wi[:, 512:1280], wi[:, 1304:2840], wi[:, 1280:1304],
                               jnp.zeros((D_MODEL, PROJ_COLS - 2840), F32)], axis=1).astype(BF16)
        q, kv, gates, mixed_conv = _proj_call(xx, norm_mix_g[l][None], w_p, conv_mix_w[l],
                                              norm_out_conv_g[l][None], t)

        xc = kv[:, 0:2 * D_KV].reshape(b, nc, CMP_STRIDE, 4, dk).transpose(0, 3, 1, 2, 4).reshape(b, 4, nc, 16 * dk)
        half = CMP_STRIDE * dk
        w1ab = jnp.stack([jnp.concatenate([w[:half], w[half:]], axis=1) for w in (w_ck1[l], w_cv1[l])]).astype(BF16)
        pos8 = jnp.stack([jnp.concatenate([p.reshape(2, half), jnp.zeros((6, half), F32)], axis=0)
                          for p in (pos_ck[l], pos_cv[l])]).astype(BF16)
        w2 = jnp.stack([w_ck2[l], w_cv2[l]]).astype(BF16)
        cmp = _compress_call(xc, w1ab, pos8, w2)
        kc = jnp.concatenate([cmp[:, 0:2].astype(BF16),
                              jnp.broadcast_to(_cmp_consts(nc), (b, hh, nc, AUG_K - dk))], axis=-1)
        vc = cmp[:, 2:4].astype(BF16).transpose(0, 1, 3, 2)

        heads = lambda a: a.reshape(b, t, hh, dk).transpose(0, 2, 1, 3)
        def keys(a, with_sel):
            c = jnp.broadcast_to(_key_consts(t, nsel, with_sel), (b, hh, t, AUG_K - dk))
            return jnp.concatenate([heads(a), c], axis=-1).reshape(b, hh, nkb, Q_TILE, AUG_K)
        def values(a):
            vt = heads(a).transpose(0, 1, 3, 2)
            ones = jnp.ones((b, hh, 1, t), BF16)
            pad = jnp.zeros((b, hh, V_ROWS - dk - 1, t), BF16)
            vt = jnp.concatenate([vt, ones, pad], axis=2)
            return vt.reshape(b, hh, V_ROWS, nkb, Q_TILE).transpose(0, 1, 3, 2, 4)
        kvb = kv.reshape(b, t, 6, D_KV)
        ks, vs, kw, vw = kvb[:, :, 2], kvb[:, :, 3], kvb[:, :, 4], kvb[:, :, 5]
        qt = q.reshape(b, t, hh, GQA_GROUP * dk).transpose(0, 2, 3, 1)
        gt = gates[:, 0:24].reshape(b, t, hh, 12).transpose(0, 2, 3, 1)
        gt = jnp.concatenate([gt, jnp.zeros((b, hh, 4, t), F32)], axis=2)
        ot = _attn_call(qt, gt, keys(ks, True), keys(kw, False), values(vs), values(vw), kc, vc,
                        _sel_map_t(t, nc), min(SEL_TOPK, nsel))
        o_attn = ot.transpose(0, 3, 1, 2).reshape(b * t, D_ATTN)

        nch = D_FF // FF_CHUNK
        chunk_cols = lambda w: w.reshape(D_MODEL, nch, FF_CHUNK).transpose(1, 0, 2).astype(BF16)
        last = l == depth - 1
        gl = norm_final_g[None] if last else jnp.ones((1, D_MODEL), F32)
        xx = _ffn_call(xx, o_attn, mixed_conv, norm_out_attn_g[l][None], w_out[l].astype(BF16),
                       norm_ffn_g[l][None], chunk_cols(w_gate[l]), chunk_cols(w_up[l]), ffn_conv_w[l],
                       ffn_conv_b[l][None], w_down[l].astype(BF16), gl, t)
    return xx.reshape(b, t, D_MODEL)
```

```python
import functools

import jax
import jax.numpy as jnp
import numpy as np
from jax import lax
from jax.experimental import pallas as pl
from jax.experimental.pallas import tpu as pltpu

F32 = jnp.float32
BF16 = jnp.bfloat16

D_MODEL = 1024
N_KV_HEADS = 2
GQA_GROUP = 4
N_HEADS_ATTN = N_KV_HEADS * GQA_GROUP
HEAD_DIM = 64
D_ATTN = N_HEADS_ATTN * HEAD_DIM
D_KV = N_KV_HEADS * HEAD_DIM
D_CONV = D_MODEL - D_ATTN
CMP_BLOCK = 32
CMP_STRIDE = 16
CMP_HIDDEN = 2 * HEAD_DIM
SEL_BLOCK = 64
SEL_TOPK = 16
WINDOW = 512
D_FF = 2816
EPS = 1e-6
NEG = -1e30
BIG = 1e30

AUG_K = 256
AUG_ALIBI = HEAD_DIM
AUG_ALIBI_ROWS = 16
AUG_SEL = AUG_ALIBI + AUG_ALIBI_ROWS
V_ROWS = 80

ROW_TILE = 512
Q_TILE = 256
FF_CHUNK = 256
PROJ_COLS = 2944
VMEM_LIMIT = 56 * 1024 * 1024

LOG2E = 1.4426950408889634


def _bf16_terms(x, n):
    out = []
    for _ in range(n):
        t = float(np.asarray(x, np.float32).astype(jnp.bfloat16).astype(np.float32))
        out.append(t)
        x = x - t
    return tuple(out)


LOG2E_3 = _bf16_terms(LOG2E, 3)


def _rms(x, g):
    return x * lax.rsqrt(jnp.mean(x * x, axis=-1, keepdims=True) + EPS) * g


def _proj_kernel(x_ref, g_ref, w_ref, cw_ref, gc_ref, q_ref, kv_ref, gate_ref, mc_ref, cbuf, *, tiles_per_seq):
    i = pl.program_id(0)
    tm = x_ref.shape[0]
    h = _rms(x_ref[...], g_ref[...])
    p = jnp.dot(h.astype(BF16), w_ref[...], preferred_element_type=F32)
    q_ref[...] = (p[:, 0:512] * (HEAD_DIM ** -0.5 * LOG2E)).astype(BF16)
    kv_ref[...] = p[:, 512:1280].astype(BF16)
    b = p[:, 1280:1792]
    cu = p[:, 1792:2304] * p[:, 2304:2816]
    gate_ref[...] = jax.nn.sigmoid(p[:, 2816:2944])

    @pl.when(i % tiles_per_seq == 0)
    def _():
        cbuf[0:8, :] = jnp.zeros((8, D_CONV), F32)

    @pl.when(i % tiles_per_seq != 0)
    def _():
        cbuf[0:8, :] = cbuf[tm:tm + 8, :]

    cbuf[8:tm + 8, :] = cu
    y = cw_ref[0:1, :] * cbuf[6:tm + 6, :] + cw_ref[1:2, :] * cbuf[7:tm + 7, :] + cw_ref[2:3, :] * cu
    mc_ref[...] = _rms(b * y, gc_ref[...]).astype(BF16)


def _proj_call(x2, g, w, cw, gc, seq):
    n = x2.shape[0]
    tm = ROW_TILE
    row = lambda i: (i, 0)
    fix = lambda i: (0, 0)
    return pl.pallas_call(
        functools.partial(_proj_kernel, tiles_per_seq=seq // tm),
        grid=(n // tm,),
        in_specs=[
            pl.BlockSpec((tm, D_MODEL), row),
            pl.BlockSpec((1, D_MODEL), fix),
            pl.BlockSpec((D_MODEL, PROJ_COLS), fix),
            pl.BlockSpec((3, D_CONV), fix),
            pl.BlockSpec((1, D_CONV), fix),
        ],
        out_specs=[
            pl.BlockSpec((tm, D_ATTN), row),
            pl.BlockSpec((tm, 6 * D_KV), row),
            pl.BlockSpec((tm, 128), row),
            pl.BlockSpec((tm, D_CONV), row),
        ],
        out_shape=[
            jax.ShapeDtypeStruct((n, D_ATTN), BF16),
            jax.ShapeDtypeStruct((n, 6 * D_KV), BF16),
            jax.ShapeDtypeStruct((n, 128), F32),
            jax.ShapeDtypeStruct((n, D_CONV), BF16),
        ],
        scratch_shapes=[pltpu.VMEM((tm + 8, D_CONV), F32)],
        compiler_params=pltpu.CompilerParams(dimension_semantics=("arbitrary",), vmem_limit_bytes=VMEM_LIMIT),
    )(x2, g, w, cw, gc)


def _compress_kernel(x_ref, w1_ref, pos_ref, w2_ref, o_ref, sbuf):
    nc = x_ref.shape[0]
    y = jnp.dot(x_ref[...], w1_ref[...], preferred_element_type=F32)
    pb = jnp.dot(pos_ref[...], w1_ref[...], preferred_element_type=F32)
    posb = pb[0:1, 0:CMP_HIDDEN] + pb[1:2, CMP_HIDDEN:2 * CMP_HIDDEN]
    sbuf[0:nc, :] = y[:, CMP_HIDDEN:2 * CMP_HIDDEN]
    sbuf[nc:nc + 8, :] = jnp.zeros((8, CMP_HIDDEN), F32)
    hid = y[:, 0:CMP_HIDDEN] + sbuf[1:nc + 1, :] + posb
    act = jax.nn.gelu(hid)
    out = jnp.dot(act.astype(BF16), w2_ref[...], preferred_element_type=F32)
    rowi = lax.broadcasted_iota(jnp.int32, out.shape, 0)
    o_ref[...] = jnp.where(rowi < nc - 1, out, 0.0)


def _compress_call(xc, w1ab, pos8, w2):
    b, _, nc, _ = xc.shape
    return pl.pallas_call(
        _compress_kernel,
        grid=(b, 4),
        in_specs=[
            pl.BlockSpec((None, None, nc, 16 * HEAD_DIM), lambda i, s: (i, s, 0, 0)),
            pl.BlockSpec((None, 16 * HEAD_DIM, 2 * CMP_HIDDEN), lambda i, s: (s // 2, 0, 0)),
            pl.BlockSpec((None, 8, 16 * HEAD_DIM), lambda i, s: (s // 2, 0, 0)),
            pl.BlockSpec((None, CMP_HIDDEN, HEAD_DIM), lambda i, s: (s // 2, 0, 0)),
        ],
        out_specs=pl.BlockSpec((None, None, nc, HEAD_DIM), lambda i, s: (i, s, 0, 0)),
        out_shape=jax.ShapeDtypeStruct((b, 4, nc, HEAD_DIM), F32),
        scratch_shapes=[pltpu.VMEM((nc + 8, CMP_HIDDEN), F32)],
        compiler_params=pltpu.CompilerParams(dimension_semantics=("arbitrary", "arbitrary")),
    )(xc, w1ab, pos8, w2)


def _attn_kernel(qt_ref, gt_ref, ks_ref, kw_ref, vs_ref, vw_ref, kc_ref, vc_ref, msel_ref, o_ref,
                 qaug, acc_ref, sbuf, pbuf, *, topk):
    h = pl.program_id(1)
    qi = pl.program_id(2)
    tq = qt_ref.shape[1]
    r = GQA_GROUP * tq
    nc = kc_ref.shape[0]
    nsel = msel_ref.shape[0]
    t0 = qi * tq

    lane16 = lax.broadcasted_iota(jnp.int32, (AUG_ALIBI_ROWS, r), 1)
    sub16 = lax.broadcasted_iota(jnp.int32, (AUG_ALIBI_ROWS, r), 0)
    gl = lane16 // tq
    off = (lane16 % tq).astype(F32)
    base = jnp.where(h == 0, 0.5, 0.03125).astype(F32)
    slope = jnp.where(gl == 0, base, jnp.where(gl == 1, base * 0.5, jnp.where(gl == 2, base * 0.25, base * 0.125)))
    blk0 = (t0 // SEL_BLOCK).astype(F32)
    c3 = jnp.where(sub16 % 3 == 0, LOG2E_3[0], jnp.where(sub16 % 3 == 1, LOG2E_3[1], LOG2E_3[2]))
    arow = jnp.where(sub16 < 3, slope * c3,
                     jnp.where(sub16 < 6, 64.0 * slope * c3,
                               jnp.where(sub16 == 6, -slope * LOG2E * (64.0 * blk0 + off), 0.0)))
    for g in range(GQA_GROUP):
        qaug[0:HEAD_DIM, g * tq:(g + 1) * tq] = qt_ref[g * HEAD_DIM:(g + 1) * HEAD_DIM, :]
    qaug[AUG_ALIBI:AUG_SEL, :] = arow.astype(BF16)
    qaug[AUG_SEL:AUG_K, :] = jnp.zeros((AUG_K - AUG_SEL, r), BF16)

    lane_q = lax.broadcasted_iota(jnp.int32, (1, r), 1) % tq
    key_i = lax.broadcasted_iota(jnp.int32, (tq, r), 0)
    causal = key_i <= lane_q

    q_nosel = qaug[...]
    nwin = WINDOW // tq
    scores = []
    for w in range(nwin + 1):
        kb = qi - nwin + w
        kbc = jnp.maximum(kb, 0)
        s = jnp.dot(kw_ref[kbc], q_nosel, preferred_element_type=F32)
        if w == 0:
            s = jnp.where((key_i > lane_q) & (kb >= 0), s, NEG)
        elif w == nwin:
            s = jnp.where(causal, s, NEG)
        else:
            s = jnp.where(kb >= 0, s, NEG)
        scores.append((kbc, s))
    m_w = scores[0][1].max(axis=0, keepdims=True)
    for _, s in scores[1:]:
        m_w = jnp.maximum(m_w, s.max(axis=0, keepdims=True))
    acc_w = jnp.zeros((V_ROWS, r), F32)
    for kbc, s in scores:
        acc_w = acc_w + jnp.dot(vw_ref[kbc], jnp.exp2(s - m_w).astype(BF16), preferred_element_type=F32)
    o_win = acc_w[0:HEAD_DIM, :] * (1.0 / acc_w[HEAD_DIM:HEAD_DIM + 1, :])

    sc = jnp.dot(kc_ref[...], q_nosel, preferred_element_type=F32)
    cend = lax.broadcasted_iota(jnp.int32, (nc, r), 0) * CMP_STRIDE + (CMP_BLOCK - 1)
    mask_c = cend <= (t0 + lane_q)
    sc = jnp.where(mask_c, sc, NEG)
    m_c = jnp.max(sc, axis=0, keepdims=True)
    e_c = jnp.where(mask_c, jnp.exp2(sc - m_c), 0.0)
    l_c = jnp.sum(e_c, axis=0, keepdims=True)
    p_c = e_c * jnp.where(l_c > 0.0, 1.0 / l_c, 0.0)
    o_cmp = jnp.dot(vc_ref[...], p_c.astype(BF16), preferred_element_type=F32)

    p_sum = p_c[:, 0:tq]
    for g in range(1, GQA_GROUP):
        p_sum = p_sum + p_c[:, g * tq:(g + 1) * tq]
    p1 = p_sum.astype(BF16)
    r1 = p_sum - p1.astype(F32)
    p2 = r1.astype(BF16)
    p3 = (r1 - p2.astype(F32)).astype(BF16)
    msel = msel_ref[...]
    imp = (jnp.dot(msel, p1, preferred_element_type=F32) + jnp.dot(msel, p2, preferred_element_type=F32)
           + jnp.dot(msel, p3, preferred_element_type=F32))
    jj = lax.broadcasted_iota(jnp.int32, (nsel, tq), 0)
    jt = (t0 + lax.broadcasted_iota(jnp.int32, (nsel, tq), 1)) // SEL_BLOCK
    imp = jnp.where((jj == 0) | (jj == jt) | (jj == jt - 1), BIG, imp)
    imp = jnp.where(jj > jt, NEG, imp)
    sub8 = lax.broadcasted_iota(jnp.int32, (8, tq), 0)
    groups = [imp[8 * gi:8 * gi + 8, :] for gi in range(nsel // 8)]
    cnts = [jnp.zeros((8, tq), jnp.int32) for _ in groups]
    for jp in range(nsel):
        rowv = jnp.broadcast_to(imp[jp:jp + 1, :], (8, tq))
        for gi, grp in enumerate(groups):
            if 8 * gi > jp:
                beats = rowv >= grp
            elif 8 * gi + 7 <= jp:
                beats = rowv > grp
            else:
                beats = (rowv > grp) | ((rowv == grp) & (sub8 + 8 * gi > jp))
            cnts[gi] = cnts[gi] + jnp.where(beats, 1, 0)
    selbias = jnp.concatenate([jnp.where(c < topk, 0.0, NEG) for c in cnts], axis=0).astype(BF16)
    for g in range(GQA_GROUP):
        qaug[AUG_SEL:AUG_SEL + nsel, g * tq:(g + 1) * tq] = selbias

    q_all = qaug[...]

    def qk(kb):
        return jnp.dot(ks_ref[kb], q_all, preferred_element_type=F32)

    def softmax_tile(s, m_old):
        m_new = jnp.maximum(m_old, jnp.max(s, axis=0, keepdims=True))
        return m_new, jnp.exp2(s - m_new).astype(BF16), jnp.exp2(m_old - m_new)

    def pv(kb, p, alpha):
        acc_ref[...] = acc_ref[...] * alpha + jnp.dot(vs_ref[kb], p, preferred_element_type=F32)

    acc_ref[...] = jnp.zeros(acc_ref.shape, F32)
    pbuf[1] = jnp.zeros((tq, r), BF16)
    sbuf[0] = qk(0)

    def step(j, cur, carry):
        m_old, alpha_prev = carry
        s = sbuf[cur]
        sbuf[1 - cur] = qk(j + 1)
        pv(jnp.maximum(j - 1, 0), pbuf[1 - cur], alpha_prev)
        m_new, p, alpha = softmax_tile(s, m_old)
        pbuf[cur] = p
        return m_new, alpha

    def finish(cur, carry):
        m_old, alpha_prev = carry
        _, p_last, alpha_last = softmax_tile(jnp.where(causal, sbuf[cur], NEG), m_old)
        pv(jnp.maximum(qi - 1, 0), pbuf[1 - cur], alpha_prev)
        pv(qi, p_last, alpha_last)

    def pair(i, carry):
        return step(2 * i + 1, 1, step(2 * i, 0, carry))

    carry0 = (jnp.full((1, r), NEG, F32), jnp.ones((1, r), F32))

    @pl.when(qi % 2 == 0)
    def _():
        finish(0, lax.fori_loop(0, qi // 2, pair, carry0))

    @pl.when(qi % 2 == 1)
    def _():
        finish(1, step(qi - 1, 0, lax.fori_loop(0, qi // 2, pair, carry0)))

    acc_s = acc_ref[...]
    o_sel = acc_s[0:HEAD_DIM, :] * (1.0 / acc_s[HEAD_DIM:HEAD_DIM + 1, :])

    gt = gt_ref[...]
    for g in range(GQA_GROUP):
        sl = slice(g * tq, (g + 1) * tq)
        o_ref[g * HEAD_DIM:(g + 1) * HEAD_DIM, :] = (gt[3 * g:3 * g + 1, :] * o_cmp[:, sl]
                                                     + gt[3 * g + 1:3 * g + 2, :] * o_sel[:, sl]
                                                     + gt[3 * g + 2:3 * g + 3, :] * o_win[:, sl])


def _attn_call(qt, gt, ks, kw, vs, vw, kc, vc, msel, topk):
    b, hh, _, t = qt.shape
    tq = Q_TILE
    nkb = t // tq
    nc = kc.shape[2]
    nsel = msel.shape[0]
    per_q = lambda i, j, k: (i, j, 0, k)
    per_bh4 = lambda i, j, k: (i, j, 0, 0)
    per_bh5 = lambda i, j, k: (i, j, 0, 0, 0)
    return pl.pallas_call(
        functools.partial(_attn_kernel, topk=topk),
        grid=(b, hh, nkb),
        in_specs=[
            pl.BlockSpec((None, None, GQA_GROUP * HEAD_DIM, tq), per_q),
            pl.BlockSpec((None, None, 16, tq), per_q),
            pl.BlockSpec((None, None, nkb, tq, AUG_K), per_bh5),
            pl.BlockSpec((None, None, nkb, tq, AUG_K), per_bh5),
            pl.BlockSpec((None, None, nkb, V_ROWS, tq), per_bh5),
            pl.BlockSpec((None, None, nkb, V_ROWS, tq), per_bh5),
            pl.BlockSpec((None, None, nc, AUG_K), per_bh4),
            pl.BlockSpec((None, None, HEAD_DIM, nc), per_bh4),
            pl.BlockSpec((nsel, nc), lambda i, j, k: (0, 0)),
        ],
        out_specs=pl.BlockSpec((None, None, GQA_GROUP * HEAD_DIM, tq), per_q),
        out_shape=jax.ShapeDtypeStruct((b, hh, GQA_GROUP * HEAD_DIM, t), F32),
        scratch_shapes=[pltpu.VMEM((AUG_K, GQA_GROUP * tq), BF16), pltpu.VMEM((V_ROWS, GQA_GROUP * tq), F32),
                        pltpu.VMEM((2, tq, GQA_GROUP * tq), F32), pltpu.VMEM((2, tq, GQA_GROUP * tq), BF16)],
        compiler_params=pltpu.CompilerParams(dimension_semantics=("arbitrary", "arbitrary", "arbitrary"),
                                             vmem_limit_bytes=VMEM_LIMIT),
    )(qt, gt, ks, kw, vs, vw, kc, vc, msel)


def _ffn_kernel(x_ref, oa_ref, mc_ref, ga_ref, wo_ref, gf_ref, wg_ref, wu_ref, cw_ref, cb_ref, wd_ref, gl_ref,
                o_ref, gbuf, ybuf, *, tiles_per_seq):
    i = pl.program_id(0)
    tm = x_ref.shape[0]
    ma = _rms(oa_ref[...], ga_ref[...]).astype(BF16)
    x1 = (x_ref[...] + jnp.dot(ma, wo_ref[0:D_ATTN, :], preferred_element_type=F32)
          + jnp.dot(mc_ref[...], wo_ref[D_ATTN:D_MODEL, :], preferred_element_type=F32))
    h2 = _rms(x1, gf_ref[...]).astype(BF16)

    @pl.when(i % tiles_per_seq == 0)
    def _():
        gbuf[0:8, :] = jnp.zeros((8, D_FF), F32)

    @pl.when(i % tiles_per_seq != 0)
    def _():
        gbuf[0:8, :] = gbuf[tm:tm + 8, :]

    for c in range(D_FF // FF_CHUNK):
        cs = slice(c * FF_CHUNK, (c + 1) * FF_CHUNK)
        gpre = jnp.dot(h2, wg_ref[c], preferred_element_type=F32)
        up = jnp.dot(h2, wu_ref[c], preferred_element_type=F32)
        gbuf[8:tm + 8, cs] = gpre
        gate = (cw_ref[0:1, cs] * gbuf[6:tm + 6, cs] + cw_ref[1:2, cs] * gbuf[7:tm + 7, cs]
                + cw_ref[2:3, cs] * gpre + cb_ref[:, cs])
        ybuf[:, cs] = (jax.nn.silu(gate) * up).astype(BF16)
    acc = x1 + jnp.dot(ybuf[...], wd_ref[...], preferred_element_type=F32)
    o_ref[...] = _rms(acc, gl_ref[...])


def _ffn_call(x2, oa, mc, ga, wo, gf, wg, wu, cw, cb, wd, gl, seq):
    n = x2.shape[0]
    tm = ROW_TILE
    nch = D_FF // FF_CHUNK
    row = lambda i: (i, 0)
    fix = lambda i: (0, 0)
    fix3 = lambda i: (0, 0, 0)
    once = dict(pipeline_mode=pl.Buffered(1))
    return pl.pallas_call(
        functools.partial(_ffn_kernel, tiles_per_seq=seq // tm),
        grid=(n // tm,),
        in_specs=[
            pl.BlockSpec((tm, D_MODEL), row),
            pl.BlockSpec((tm, D_ATTN), row),
            pl.BlockSpec((tm, D_CONV), row),
            pl.BlockSpec((1, D_ATTN), fix),
            pl.BlockSpec((D_MODEL, D_MODEL), fix, **once),
            pl.BlockSpec((1, D_MODEL), fix),
            pl.BlockSpec((nch, D_MODEL, FF_CHUNK), fix3, **once),
            pl.BlockSpec((nch, D_MODEL, FF_CHUNK), fix3, **once),
            pl.BlockSpec((3, D_FF), fix),
            pl.BlockSpec((1, D_FF), fix),
            pl.BlockSpec((D_FF, D_MODEL), fix, **once),
            pl.BlockSpec((1, D_MODEL), fix),
        ],
        out_specs=pl.BlockSpec((tm, D_MODEL), row),
        out_shape=jax.ShapeDtypeStruct((n, D_MODEL), F32),
        scratch_shapes=[pltpu.VMEM((tm + 8, D_FF), F32), pltpu.VMEM((tm, D_FF), BF16)],
        compiler_params=pltpu.CompilerParams(dimension_semantics=("arbitrary",), vmem_limit_bytes=VMEM_LIMIT),
    )(x2, oa, mc, ga, wo, gf, wg, wu, cw, cb, wd, gl)


def _alibi_cols(pos):
    cols = np.zeros((len(pos), AUG_ALIBI_ROWS), np.float32)
    cols[:, 0:3] = (pos % SEL_BLOCK)[:, None]
    cols[:, 3:6] = (pos // SEL_BLOCK)[:, None]
    cols[:, 6] = 1.0
    return cols


def _key_consts(t, nsel, with_sel):
    pos = np.arange(t)
    c = np.zeros((t, AUG_K - HEAD_DIM), np.float32)
    c[:, 0:AUG_ALIBI_ROWS] = _alibi_cols(pos)
    if with_sel:
        c[pos, AUG_ALIBI_ROWS + pos // SEL_BLOCK] = 1.0
    return jnp.asarray(c, BF16)


def _cmp_consts(nc):
    c = np.zeros((nc, AUG_K - HEAD_DIM), np.float32)
    c[:, 0:AUG_ALIBI_ROWS] = _alibi_cols(np.arange(nc) * CMP_STRIDE + (CMP_BLOCK - 1))
    return jnp.asarray(c, BF16)


def _sel_map_t(t, nc):
    n_cmp = (t - CMP_BLOCK) // CMP_STRIDE + 1
    n_sel = t // SEL_BLOCK
    cs = np.arange(n_cmp)[:, None] * CMP_STRIDE
    ss = np.arange(n_sel)[None, :] * SEL_BLOCK
    ov = np.maximum(0, np.minimum(cs + CMP_BLOCK, ss + SEL_BLOCK) - np.maximum(cs, ss)) / CMP_BLOCK
    m = np.zeros((n_sel, nc), np.float32)
    m[:, :n_cmp] = ov.T
    return jnp.asarray(m, BF16)


def kernel(x, norm_mix_g, w_in, pos_ck, w_ck1, w_ck2, pos_cv, w_cv1, w_cv2, conv_mix_w, norm_out_attn_g,
           norm_out_conv_g, w_out, norm_ffn_g, w_gate, w_up, ffn_conv_w, ffn_conv_b, w_down, norm_final_g):
    b, t, _ = x.shape
    hh, dk = N_KV_HEADS, HEAD_DIM
    assert t % ROW_TILE == 0 and t % Q_TILE == 0 and WINDOW % Q_TILE == 0 and t // SEL_BLOCK <= AUG_K - AUG_SEL
    nc = t // CMP_STRIDE
    nsel = t // SEL_BLOCK
    nkb = t // Q_TILE
    depth = w_in.shape[0]
    assert depth == 1
    xx = x.reshape(b * t, D_MODEL)
    for l in range(depth):
        wi = w_in[l]
        w_p = jnp.concatenate([wi[:, 0:1280], wi[:, 1304:2840], wi[:, 1280:1304],
                               jnp.zeros((D_MODEL, PROJ_COLS - 2840), F32)], axis=1).astype(BF16)
        q, kv, gates, mixed_conv = _proj_call(xx, norm_mix_g[l][None], w_p, conv_mix_w[l],
                                              norm_out_conv_g[l][None], t)

        xc = kv[:, 0:2 * D_KV].reshape(b, nc, CMP_STRIDE, 4, dk).transpose(0, 3, 1, 2, 4).reshape(b, 4, nc, 16 * dk)
        half = CMP_STRIDE * dk
        w1ab = jnp.stack([jnp.concatenate([w[:half], w[half:]], axis=1) for w in (w_ck1[l], w_cv1[l])]).astype(BF16)
        pos8 = jnp.stack([jnp.concatenate([p.reshape(2, half), jnp.zeros((6, half), F32)], axis=0)
                          for p in (pos_ck[l], pos_cv[l])]).astype(BF16)
        w2 = jnp.stack([w_ck2[l], w_cv2[l]]).astype(BF16)
        cmp = _compress_call(xc, w1ab, pos8, w2)
        kc = jnp.concatenate([cmp[:, 0:2].astype(BF16),
                              jnp.broadcast_to(_cmp_consts(nc), (b, hh, nc, AUG_K - dk))], axis=-1)
        vc = cmp[:, 2:4].astype(BF16).transpose(0, 1, 3, 2)

        heads = lambda a: a.reshape(b, t, hh, dk).transpose(0, 2, 1, 3)
        def keys(a, with_sel):
            c = jnp.broadcast_to(_key_consts(t, nsel, with_sel), (b, hh, t, AUG_K - dk))
            return jnp.concatenate([heads(a), c], axis=-1).reshape(b, hh, nkb, Q_TILE, AUG_K)
        def values(a):
            vt = heads(a).transpose(0, 1, 3, 2)
            ones = jnp.ones((b, hh, 1, t), BF16)
            pad = jnp.zeros((b, hh, V_ROWS - dk - 1, t), BF16)
            vt = jnp.concatenate([vt, ones, pad], axis=2)
            return vt.reshape(b, hh, V_ROWS, nkb, Q_TILE).transpose(0, 1, 3, 2, 4)
        kvb = kv.reshape(b, t, 6, D_KV)
        ks, vs, kw, vw = kvb[:, :, 2], kvb[:, :, 3], kvb[:, :, 4], kvb[:, :, 5]
        qt = q.reshape(b, t, hh, GQA_GROUP * dk).transpose(0, 2, 3, 1)
        gt = gates[:, 0:24].reshape(b, t, hh, 12).transpose(0, 2, 3, 1)
        gt = jnp.concatenate([gt, jnp.zeros((b, hh, 4, t), F32)], axis=2)
        ot = _attn_call(qt, gt, keys(ks, True), keys(kw, False), values(vs), values(vw), kc, vc,
                        _sel_map_t(t, nc), min(SEL_TOPK, nsel))
        o_attn = ot.transpose(0, 3, 1, 2).reshape(b * t, D_ATTN)

        nch = D_FF // FF_CHUNK
        chunk_cols = lambda w: w.reshape(D_MODEL, nch, FF_CHUNK).transpose(1, 0, 2).astype(BF16)
        last = l == depth - 1
        gl = norm_final_g[None] if last else jnp.ones((1, D_MODEL), F32)
        xx = _ffn_call(xx, o_attn, mixed_conv, norm_out_attn_g[l][None], w_out[l].astype(BF16),
                       norm_ffn_g[l][None], chunk_cols(w_gate[l]), chunk_cols(w_up[l]), ffn_conv_w[l],
                       ffn_conv_b[l][None], w_down[l].astype(BF16), gl, t)
    return xx.reshape(b, t, D_MODEL)
```

```python
import functools

import jax
import jax.numpy as jnp
import numpy as np
from jax import lax
from jax.experimental import pallas as pl
from jax.experimental.pallas import tpu as pltpu

F32 = jnp.float32
BF16 = jnp.bfloat16

D_MODEL = 1024
N_KV_HEADS = 2
GQA_GROUP = 4
N_HEADS_ATTN = N_KV_HEADS * GQA_GROUP
HEAD_DIM = 64
D_ATTN = N_HEADS_ATTN * HEAD_DIM
D_KV = N_KV_HEADS * HEAD_DIM
D_CONV = D_MODEL - D_ATTN
CMP_BLOCK = 32
CMP_STRIDE = 16
CMP_HIDDEN = 2 * HEAD_DIM
SEL_BLOCK = 64
SEL_TOPK = 16
WINDOW = 512
D_FF = 2816
EPS = 1e-6
NEG = -1e30
BIG = 1e30

AUG_K = 256
AUG_ALIBI = HEAD_DIM
AUG_ALIBI_ROWS = 16
AUG_SEL = AUG_ALIBI + AUG_ALIBI_ROWS
V_ROWS = 80

ROW_TILE = 512
Q_TILE = 256
FF_CHUNK = 256
PROJ_COLS = 2944
VMEM_LIMIT = 56 * 1024 * 1024

LOG2E = 1.4426950408889634


def _bf16_terms(x, n):
    out = []
    for _ in range(n):
        t = float(np.asarray(x, np.float32).astype(jnp.bfloat16).astype(np.float32))
        out.append(t)
        x = x - t
    return tuple(out)


LOG2E_3 = _bf16_terms(LOG2E, 3)


def _rms(x, g):
    return x * lax.rsqrt(jnp.mean(x * x, axis=-1, keepdims=True) + EPS) * g


COL_Q = 0
COL_KV = D_ATTN
COL_CMP = COL_KV + 4 * 2 * HEAD_DIM
COL_B = COL_CMP + 2 * D_KV
COL_C = COL_B + D_CONV
COL_U = COL_C + D_CONV
COL_GATE = COL_U + D_CONV
GATE_ROWS = 16


def _proj_kernel(x_ref, g_ref, w_ref, cw_ref, gc_ref, csel_ref, cwin_ref,
                 qt_ref, gt_ref, ksa_ref, kwa_ref, vst_ref, vwt_ref, kvc_ref, mc_ref, cbuf, *, tiles_per_seq):
    i = pl.program_id(0)
    tm = x_ref.shape[0]
    tq = vst_ref.shape[-1]
    h = _rms(x_ref[...], g_ref[...])
    p = jnp.dot(h.astype(BF16), w_ref[...], preferred_element_type=F32)

    for hd in range(N_KV_HEADS):
        qs = p[:, COL_Q + hd * 256:COL_Q + (hd + 1) * 256] * (HEAD_DIM ** -0.5 * LOG2E)
        qt_ref[hd] = qs.T.astype(BF16)
    gt = jax.nn.sigmoid(p[:, COL_GATE:COL_GATE + 128]).T
    for hd in range(N_KV_HEADS):
        gt_ref[hd] = gt[hd * GATE_ROWS:(hd + 1) * GATE_ROWS, :]

    lane = lax.broadcasted_iota(jnp.int32, (tm, 128), 1)
    ones_rows = jnp.where(lax.broadcasted_iota(jnp.int32, (V_ROWS - HEAD_DIM, tq), 0) == 0, 1.0, 0.0).astype(BF16)
    for branch, (ka_ref, vt_ref, c_ref) in enumerate(((ksa_ref, vst_ref, csel_ref), (kwa_ref, vwt_ref, cwin_ref))):
        for hd in range(N_KV_HEADS):
            c0 = COL_KV + (2 * branch + hd) * 128
            slab = p[:, c0:c0 + 128]
            ka_ref[hd, :, 0:128] = jnp.where(lane < HEAD_DIM, slab.astype(BF16), c_ref[:, 0:128])
            ka_ref[hd, :, 128:AUG_K] = c_ref[:, 128:AUG_K]
            vt = slab.T[HEAD_DIM:128, :].astype(BF16)
            for kt in range(tm // tq):
                vt_ref[hd, kt, 0:HEAD_DIM, :] = vt[:, kt * tq:(kt + 1) * tq]
                vt_ref[hd, kt, HEAD_DIM:V_ROWS, :] = ones_rows

    kvc_ref[...] = p[:, COL_CMP:COL_CMP + 2 * D_KV].astype(BF16)
    b = p[:, COL_B:COL_B + D_CONV]
    cu = p[:, COL_C:COL_C + D_CONV] * p[:, COL_U:COL_U + D_CONV]

    @pl.when(i % tiles_per_seq == 0)
    def _():
        cbuf[0:8, :] = jnp.zeros((8, D_CONV), F32)

    @pl.when(i % tiles_per_seq != 0)
    def _():
        cbuf[0:8, :] = cbuf[tm:tm + 8, :]

    cbuf[8:tm + 8, :] = cu
    y = cw_ref[0:1, :] * cbuf[6:tm + 6, :] + cw_ref[1:2, :] * cbuf[7:tm + 7, :] + cw_ref[2:3, :] * cu
    mc_ref[...] = _rms(b * y, gc_ref[...]).astype(BF16)


def _proj_call(x3, g, w, cw, gc, csel, cwin):
    bsz, seq, _ = x3.shape
    tm = ROW_TILE
    tq = Q_TILE
    tps = seq // tm
    hh = N_KV_HEADS
    row = lambda i: (i // tps, i % tps, 0)
    fix = lambda i: (0, 0)
    seq_tile = lambda i: (i % tps, 0)
    tok_minor = lambda i: (i // tps, 0, 0, i % tps)
    tok_major = lambda i: (i // tps, 0, i % tps, 0)
    tok_tiles = lambda i: (i // tps, 0, i % tps, 0, 0)
    return pl.pallas_call(
        functools.partial(_proj_kernel, tiles_per_seq=tps),
        grid=(bsz * tps,),
        in_specs=[
            pl.BlockSpec((None, tm, D_MODEL), row),
            pl.BlockSpec((1, D_MODEL), fix),
            pl.BlockSpec((D_MODEL, PROJ_COLS), fix),
            pl.BlockSpec((3, D_CONV), fix),
            pl.BlockSpec((1, D_CONV), fix),
            pl.BlockSpec((tm, AUG_K), seq_tile),
            pl.BlockSpec((tm, AUG_K), seq_tile),
        ],
        out_specs=[
            pl.BlockSpec((None, hh, GQA_GROUP * HEAD_DIM, tm), tok_minor),
            pl.BlockSpec((None, hh, GATE_ROWS, tm), tok_minor),
            pl.BlockSpec((None, hh, tm, AUG_K), tok_major),
            pl.BlockSpec((None, hh, tm, AUG_K), tok_major),
            pl.BlockSpec((None, hh, tm // tq, V_ROWS, tq), tok_tiles),
            pl.BlockSpec((None, hh, tm // tq, V_ROWS, tq), tok_tiles),
            pl.BlockSpec((None, tm, 2 * D_KV), row),
            pl.BlockSpec((None, tm, D_CONV), row),
        ],
        out_shape=[
            jax.ShapeDtypeStruct((bsz, hh, GQA_GROUP * HEAD_DIM, seq), BF16),
            jax.ShapeDtypeStruct((bsz, hh, GATE_ROWS, seq), F32),
            jax.ShapeDtypeStruct((bsz, hh, seq, AUG_K), BF16),
            jax.ShapeDtypeStruct((bsz, hh, seq, AUG_K), BF16),
            jax.ShapeDtypeStruct((bsz, hh, seq // tq, V_ROWS, tq), BF16),
            jax.ShapeDtypeStruct((bsz, hh, seq // tq, V_ROWS, tq), BF16),
            jax.ShapeDtypeStruct((bsz, seq, 2 * D_KV), BF16),
            jax.ShapeDtypeStruct((bsz, seq, D_CONV), BF16),
        ],
        scratch_shapes=[pltpu.VMEM((tm + 8, D_CONV), F32)],
        compiler_params=pltpu.CompilerParams(dimension_semantics=("arbitrary",), vmem_limit_bytes=VMEM_LIMIT),
    )(x3, g, w, cw, gc, csel, cwin)


def _compress_kernel(x_ref, w1_ref, pos_ref, w2_ref, o_ref, sbuf):
    nc = x_ref.shape[0]
    y = jnp.dot(x_ref[...], w1_ref[...], preferred_element_type=F32)
    pb = jnp.dot(pos_ref[...], w1_ref[...], preferred_element_type=F32)
    posb = pb[0:1, 0:CMP_HIDDEN] + pb[1:2, CMP_HIDDEN:2 * CMP_HIDDEN]
    sbuf[0:nc, :] = y[:, CMP_HIDDEN:2 * CMP_HIDDEN]
    sbuf[nc:nc + 8, :] = jnp.zeros((8, CMP_HIDDEN), F32)
    hid = y[:, 0:CMP_HIDDEN] + sbuf[1:nc + 1, :] + posb
    act = jax.nn.gelu(hid)
    out = jnp.dot(act.astype(BF16), w2_ref[...], preferred_element_type=F32)
    rowi = lax.broadcasted_iota(jnp.int32, out.shape, 0)
    o_ref[...] = jnp.where(rowi < nc - 1, out, 0.0)


def _compress_call(xc, w1ab, pos8, w2):
    b, _, nc, _ = xc.shape
    return pl.pallas_call(
        _compress_kernel,
        grid=(b, 4),
        in_specs=[
            pl.BlockSpec((None, None, nc, 16 * HEAD_DIM), lambda i, s: (i, s, 0, 0)),
            pl.BlockSpec((None, 16 * HEAD_DIM, 2 * CMP_HIDDEN), lambda i, s: (s // 2, 0, 0)),
            pl.BlockSpec((None, 8, 16 * HEAD_DIM), lambda i, s: (s // 2, 0, 0)),
            pl.BlockSpec((None, CMP_HIDDEN, HEAD_DIM), lambda i, s: (s // 2, 0, 0)),
        ],
        out_specs=pl.BlockSpec((None, None, nc, HEAD_DIM), lambda i, s: (i, s, 0, 0)),
        out_shape=jax.ShapeDtypeStruct((b, 4, nc, HEAD_DIM), F32),
        scratch_shapes=[pltpu.VMEM((nc + 8, CMP_HIDDEN), F32)],
        compiler_params=pltpu.CompilerParams(dimension_semantics=("arbitrary", "arbitrary")),
    )(xc, w1ab, pos8, w2)


def _attn_kernel(qt_ref, gt_ref, ks_ref, kw_ref, vs_ref, vw_ref, kc_ref, vc_ref, msel_ref, o_ref,
                 qaug, acc_ref, sbuf, pbuf, *, topk):
    h = pl.program_id(1)
    qi = pl.program_id(2)
    tq = qt_ref.shape[1]
    r = GQA_GROUP * tq
    nc = kc_ref.shape[0]
    nsel = msel_ref.shape[0]
    t0 = qi * tq

    lane16 = lax.broadcasted_iota(jnp.int32, (AUG_ALIBI_ROWS, r), 1)
    sub16 = lax.broadcasted_iota(jnp.int32, (AUG_ALIBI_ROWS, r), 0)
    gl = lane16 // tq
    off = (lane16 % tq).astype(F32)
    base = jnp.where(h == 0, 0.5, 0.03125).astype(F32)
    slope = jnp.where(gl == 0, base, jnp.where(gl == 1, base * 0.5, jnp.where(gl == 2, base * 0.25, base * 0.125)))
    blk0 = (t0 // SEL_BLOCK).astype(F32)
    c3 = jnp.where(sub16 % 3 == 0, LOG2E_3[0], jnp.where(sub16 % 3 == 1, LOG2E_3[1], LOG2E_3[2]))
    arow = jnp.where(sub16 < 3, slope * c3,
                     jnp.where(sub16 < 6, 64.0 * slope * c3,
                               jnp.where(sub16 == 6, -slope * LOG2E * (64.0 * blk0 + off), 0.0)))
    for g in range(GQA_GROUP):
        qaug[0:HEAD_DIM, g * tq:(g + 1) * tq] = qt_ref[g * HEAD_DIM:(g + 1) * HEAD_DIM, :]
    qaug[AUG_ALIBI:AUG_SEL, :] = arow.astype(BF16)
    qaug[AUG_SEL:AUG_K, :] = jnp.zeros((AUG_K - AUG_SEL, r), BF16)

    lane_q = lax.broadcasted_iota(jnp.int32, (1, r), 1) % tq
    key_i = lax.broadcasted_iota(jnp.int32, (tq, r), 0)
    causal = key_i <= lane_q

    q_nosel = qaug[...]
    nwin = WINDOW // tq
    scores = []
    for w in range(nwin + 1):
        kb = qi - nwin + w
        kbc = jnp.maximum(kb, 0)
        s = jnp.dot(kw_ref[kbc], q_nosel, preferred_element_type=F32)
        if w == 0:
            s = jnp.where((key_i > lane_q) & (kb >= 0), s, NEG)
        elif w == nwin:
            s = jnp.where(causal, s, NEG)
        else:
            s = jnp.where(kb >= 0, s, NEG)
        scores.append((kbc, s))
    m_w = scores[0][1].max(axis=0, keepdims=True)
    for _, s in scores[1:]:
        m_w = jnp.maximum(m_w, s.max(axis=0, keepdims=True))
    acc_w = jnp.zeros((V_ROWS, r), F32)
    for kbc, s in scores:
        acc_w = acc_w + jnp.dot(vw_ref[kbc], jnp.exp2(s - m_w).astype(BF16), preferred_element_type=F32)
    o_win = acc_w[0:HEAD_DIM, :] * (1.0 / acc_w[HEAD_DIM:HEAD_DIM + 1, :])

    sc = jnp.dot(kc_ref[...], q_nosel, preferred_element_type=F32)
    cend = lax.broadcasted_iota(jnp.int32, (nc, r), 0) * CMP_STRIDE + (CMP_BLOCK - 1)
    mask_c = cend <= (t0 + lane_q)
    sc = jnp.where(mask_c, sc, NEG)
    m_c = jnp.max(sc, axis=0, keepdims=True)
    e_c = jnp.where(mask_c, jnp.exp2(sc - m_c), 0.0)
    l_c = jnp.sum(e_c, axis=0, keepdims=True)
    p_c = e_c * jnp.where(l_c > 0.0, 1.0 / l_c, 0.0)
    o_cmp = jnp.dot(vc_ref[...], p_c.astype(BF16), preferred_element_type=F32)

    p_sum = p_c[:, 0:tq]
    for g in range(1, GQA_GROUP):
        p_sum = p_sum + p_c[:, g * tq:(g + 1) * tq]
    p1 = p_sum.astype(BF16)
    r1 = p_sum - p1.astype(F32)
    p2 = r1.astype(BF16)
    p3 = (r1 - p2.astype(F32)).astype(BF16)
    msel = msel_ref[...]
    imp = (jnp.dot(msel, p1, preferred_element_type=F32) + jnp.dot(msel, p2, preferred_element_type=F32)
           + jnp.dot(msel, p3, preferred_element_type=F32))
    jj = lax.broadcasted_iota(jnp.int32, (nsel, tq), 0)
    jt = (t0 + lax.broadcasted_iota(jnp.int32, (nsel, tq), 1)) // SEL_BLOCK
    imp = jnp.where((jj == 0) | (jj == jt) | (jj == jt - 1), BIG, imp)
    imp = jnp.where(jj > jt, NEG, imp)
    sub8 = lax.broadcasted_iota(jnp.int32, (8, tq), 0)
    groups = [imp[8 * gi:8 * gi + 8, :] for gi in range(nsel // 8)]
    cnts = [jnp.zeros((8, tq), jnp.int32) for _ in groups]
    for jp in range(nsel):
        rowv = jnp.broadcast_to(imp[jp:jp + 1, :], (8, tq))
        for gi, grp in enumerate(groups):
            if 8 * gi > jp:
                beats = rowv >= grp
            elif 8 * gi + 7 <= jp:
                beats = rowv > grp
            else:
                beats = (rowv > grp) | ((rowv == grp) & (sub8 + 8 * gi > jp))
            cnts[gi] = cnts[gi] + jnp.where(beats, 1, 0)
    selbias = jnp.concatenate([jnp.where(c < topk, 0.0, NEG) for c in cnts], axis=0).astype(BF16)
    for g in range(GQA_GROUP):
        qaug[AUG_SEL:AUG_SEL + nsel, g * tq:(g + 1) * tq] = selbias

    q_all = qaug[...]

    def qk(kb):
        return jnp.dot(ks_ref[kb], q_all, preferred_element_type=F32)

    def softmax_tile(s, m_old):
        m_new = jnp.maximum(m_old, jnp.max(s, axis=0, keepdims=True))
        return m_new, jnp.exp2(s - m_new).astype(BF16), jnp.exp2(m_old - m_new)

    def pv(kb, p, alpha):
        acc_ref[...] = acc_ref[...] * alpha + jnp.dot(vs_ref[kb], p, preferred_element_type=F32)

    acc_ref[...] = jnp.zeros(acc_ref.shape, F32)
    pbuf[1] = jnp.zeros((tq, r), BF16)
    sbuf[0] = qk(0)

    def step(j, cur, carry):
        m_old, alpha_prev = carry
        s = sbuf[cur]
        sbuf[1 - cur] = qk(j + 1)
        pv(jnp.maximum(j - 1, 0), pbuf[1 - cur], alpha_prev)
        m_new, p, alpha = softmax_tile(s, m_old)
        pbuf[cur] = p
        return m_new, alpha

    def finish(cur, carry):
        m_old, alpha_prev = carry
        _, p_last, alpha_last = softmax_tile(jnp.where(causal, sbuf[cur], NEG), m_old)
        pv(jnp.maximum(qi - 1, 0), pbuf[1 - cur], alpha_prev)
        pv(qi, p_last, alpha_last)

    def pair(i, carry):
        return step(2 * i + 1, 1, step(2 * i, 0, carry))

    carry0 = (jnp.full((1, r), NEG, F32), jnp.ones((1, r), F32))

    @pl.when(qi % 2 == 0)
    def _():
        finish(0, lax.fori_loop(0, qi // 2, pair, carry0))

    @pl.when(qi % 2 == 1)
    def _():
        finish(1, step(qi - 1, 0, lax.fori_loop(0, qi // 2, pair, carry0)))

    acc_s = acc_ref[...]
    o_sel = acc_s[0:HEAD_DIM, :] * (1.0 / acc_s[HEAD_DIM:HEAD_DIM + 1, :])

    gt = gt_ref[...]
    for g in range(GQA_GROUP):
        sl = slice(g * tq, (g + 1) * tq)
        og = (gt[3 * g:3 * g + 1, :] * o_cmp[:, sl] + gt[3 * g + 1:3 * g + 2, :] * o_sel[:, sl]
              + gt[3 * g + 2:3 * g + 3, :] * o_win[:, sl])
        o_ref[:, g * HEAD_DIM:(g + 1) * HEAD_DIM] = og.T


def _attn_call(qt, gt, ks, kw, vs, vw, kc, vc, msel, topk):
    b, hh, _, t = qt.shape
    tq = Q_TILE
    nkb = t // tq
    nc = kc.shape[2]
    nsel = msel.shape[0]
    per_q = lambda i, j, k: (i, j, 0, k)
    per_bh4 = lambda i, j, k: (i, j, 0, 0)
    per_bh5 = lambda i, j, k: (i, j, 0, 0, 0)
    return pl.pallas_call(
        functools.partial(_attn_kernel, topk=topk),
        grid=(b, hh, nkb),
        in_specs=[
            pl.BlockSpec((None, None, GQA_GROUP * HEAD_DIM, tq), per_q),
            pl.BlockSpec((None, None, GATE_ROWS, tq), per_q),
            pl.BlockSpec((None, None, nkb, tq, AUG_K), per_bh5),
            pl.BlockSpec((None, None, nkb, tq, AUG_K), per_bh5),
            pl.BlockSpec((None, None, nkb, V_ROWS, tq), per_bh5),
            pl.BlockSpec((None, None, nkb, V_ROWS, tq), per_bh5),
            pl.BlockSpec((None, None, nc, AUG_K), per_bh4),
            pl.BlockSpec((None, None, HEAD_DIM, nc), per_bh4),
            pl.BlockSpec((nsel, nc), lambda i, j, k: (0, 0)),
        ],
        out_specs=pl.BlockSpec((None, tq, GQA_GROUP * HEAD_DIM), lambda i, j, k: (i, k, j)),
        out_shape=jax.ShapeDtypeStruct((b, t, D_ATTN), F32),
        scratch_shapes=[pltpu.VMEM((AUG_K, GQA_GROUP * tq), BF16), pltpu.VMEM((V_ROWS, GQA_GROUP * tq), F32),
                        pltpu.VMEM((2, tq, GQA_GROUP * tq), F32), pltpu.VMEM((2, tq, GQA_GROUP * tq), BF16)],
        compiler_params=pltpu.CompilerParams(dimension_semantics=("arbitrary", "arbitrary", "arbitrary"),
                                             vmem_limit_bytes=VMEM_LIMIT),
    )(qt, gt, ks, kw, vs, vw, kc, vc, msel)


def _ffn_kernel(x_ref, oa_ref, mc_ref, ga_ref, wo_ref, gf_ref, wg_ref, wu_ref, cw_ref, cb_ref, wd_ref, gl_ref,
                o_ref, gbuf, ybuf, *, tiles_per_seq):
    i = pl.program_id(0)
    tm = x_ref.shape[0]
    ma = _rms(oa_ref[...], ga_ref[...]).astype(BF16)
    x1 = (x_ref[...] + jnp.dot(ma, wo_ref[0:D_ATTN, :], preferred_element_type=F32)
          + jnp.dot(mc_ref[...], wo_ref[D_ATTN:D_MODEL, :], preferred_element_type=F32))
    h2 = _rms(x1, gf_ref[...]).astype(BF16)

    @pl.when(i % tiles_per_seq == 0)
    def _():
        gbuf[0:8, :] = jnp.zeros((8, D_FF), F32)

    @pl.when(i % tiles_per_seq != 0)
    def _():
        gbuf[0:8, :] = gbuf[tm:tm + 8, :]

    for c in range(D_FF // FF_CHUNK):
        cs = slice(c * FF_CHUNK, (c + 1) * FF_CHUNK)
        gpre = jnp.dot(h2, wg_ref[:, cs], preferred_element_type=F32)
        up = jnp.dot(h2, wu_ref[:, cs], preferred_element_type=F32)
        gbuf[8:tm + 8, cs] = gpre
        gate = (cw_ref[0:1, cs] * gbuf[6:tm + 6, cs] + cw_ref[1:2, cs] * gbuf[7:tm + 7, cs]
                + cw_ref[2:3, cs] * gpre + cb_ref[:, cs])
        ybuf[:, cs] = (jax.nn.silu(gate) * up).astype(BF16)
    acc = x1 + jnp.dot(ybuf[...], wd_ref[...], preferred_element_type=F32)
    o_ref[...] = _rms(acc, gl_ref[...])


def _ffn_call(x2, oa, mc, ga, wo, gf, wg, wu, cw, cb, wd, gl, seq):
    n = x2.shape[0]
    tm = ROW_TILE
    row = lambda i: (i, 0)
    fix = lambda i: (0, 0)
    once = dict(pipeline_mode=pl.Buffered(1))
    return pl.pallas_call(
        functools.partial(_ffn_kernel, tiles_per_seq=seq // tm),
        grid=(n // tm,),
        in_specs=[
            pl.BlockSpec((tm, D_MODEL), row),
            pl.BlockSpec((tm, D_ATTN), row),
            pl.BlockSpec((tm, D_CONV), row),
            pl.BlockSpec((1, D_ATTN), fix),
            pl.BlockSpec((D_MODEL, D_MODEL), fix, **once),
            pl.BlockSpec((1, D_MODEL), fix),
            pl.BlockSpec((D_MODEL, D_FF), fix, **once),
            pl.BlockSpec((D_MODEL, D_FF), fix, **once),
            pl.BlockSpec((3, D_FF), fix),
            pl.BlockSpec((1, D_FF), fix),
            pl.BlockSpec((D_FF, D_MODEL), fix, **once),
            pl.BlockSpec((1, D_MODEL), fix),
        ],
        out_specs=pl.BlockSpec((tm, D_MODEL), row),
        out_shape=jax.ShapeDtypeStruct((n, D_MODEL), F32),
        scratch_shapes=[pltpu.VMEM((tm + 8, D_FF), F32), pltpu.VMEM((tm, D_FF), BF16)],
        compiler_params=pltpu.CompilerParams(dimension_semantics=("arbitrary",), vmem_limit_bytes=VMEM_LIMIT),
    )(x2, oa, mc, ga, wo, gf, wg, wu, cw, cb, wd, gl)


def _alibi_cols(pos):
    cols = np.zeros((len(pos), AUG_ALIBI_ROWS), np.float32)
    cols[:, 0:3] = (pos % SEL_BLOCK)[:, None]
    cols[:, 3:6] = (pos // SEL_BLOCK)[:, None]
    cols[:, 6] = 1.0
    return cols


def _key_consts(t, with_sel):
    pos = np.arange(t)
    c = np.zeros((t, AUG_K), np.float32)
    c[:, AUG_ALIBI:AUG_SEL] = _alibi_cols(pos)
    if with_sel:
        c[pos, AUG_SEL + pos // SEL_BLOCK] = 1.0
    return jnp.asarray(c, BF16)


def _cmp_consts(nc):
    c = np.zeros((nc, AUG_K - HEAD_DIM), np.float32)
    c[:, 0:AUG_ALIBI_ROWS] = _alibi_cols(np.arange(nc) * CMP_STRIDE + (CMP_BLOCK - 1))
    return jnp.asarray(c, BF16)


def _sel_map_t(t, nc):
    n_cmp = (t - CMP_BLOCK) // CMP_STRIDE + 1
    n_sel = t // SEL_BLOCK
    cs = np.arange(n_cmp)[:, None] * CMP_STRIDE
    ss = np.arange(n_sel)[None, :] * SEL_BLOCK
    ov = np.maximum(0, np.minimum(cs + CMP_BLOCK, ss + SEL_BLOCK) - np.maximum(cs, ss)) / CMP_BLOCK
    m = np.zeros((n_sel, nc), np.float32)
    m[:, :n_cmp] = ov.T
    return jnp.asarray(m, BF16)


def kernel(x, norm_mix_g, w_in, pos_ck, w_ck1, w_ck2, pos_cv, w_cv1, w_cv2, conv_mix_w, norm_out_attn_g,
           norm_out_conv_g, w_out, norm_ffn_g, w_gate, w_up, ffn_conv_w, ffn_conv_b, w_down, norm_final_g):
    b, t, _ = x.shape
    hh, dk = N_KV_HEADS, HEAD_DIM
    assert t % ROW_TILE == 0 and t % Q_TILE == 0 and WINDOW % Q_TILE == 0 and t // SEL_BLOCK <= AUG_K - AUG_SEL
    nc = t // CMP_STRIDE
    nsel = t // SEL_BLOCK
    nkb = t // Q_TILE
    depth = w_in.shape[0]
    assert depth == 1
    xx = x
    for l in range(depth):
        wi = w_in[l]
        col = np.cumsum([0, D_ATTN] + [D_KV] * 6 + [3 * N_HEADS_ATTN] + [D_CONV] * 3)
        kv_slabs = [wi[:, col[3 + 2 * br] + hd * dk:col[3 + 2 * br] + (hd + 1) * dk] if part == 0 else
                    wi[:, col[4 + 2 * br] + hd * dk:col[4 + 2 * br] + (hd + 1) * dk]
                    for br in range(2) for hd in range(hh) for part in range(2)]
        gate_cols = [jnp.concatenate([wi[:, col[7] + 12 * hd:col[7] + 12 * (hd + 1)], jnp.zeros((D_MODEL, 4), F32)], axis=1)
                     for hd in range(hh)]
        w_p = jnp.concatenate([wi[:, 0:D_ATTN]] + kv_slabs + [wi[:, col[1]:col[3]], wi[:, col[8]:col[11]]] + gate_cols
                              + [jnp.zeros((D_MODEL, 128 - 2 * GATE_ROWS), F32)], axis=1).astype(BF16)
        assert w_p.shape[1] == PROJ_COLS
        qt, gt, ksa, kwa, vst, vwt, kvc, mixed_conv = _proj_call(
            xx, norm_mix_g[l][None], w_p, conv_mix_w[l], norm_out_conv_g[l][None],
            _key_consts(t, True), _key_consts(t, False))

        xc = kvc.reshape(b, nc, CMP_STRIDE, 4, dk).transpose(0, 3, 1, 2, 4).reshape(b, 4, nc, 16 * dk)
        half = CMP_STRIDE * dk
        w1ab = jnp.stack([jnp.concatenate([w[:half], w[half:]], axis=1) for w in (w_ck1[l], w_cv1[l])]).astype(BF16)
        pos8 = jnp.stack([jnp.concatenate([p.reshape(2, half), jnp.zeros((6, half), F32)], axis=0)
                          for p in (pos_ck[l], pos_cv[l])]).astype(BF16)
        w2 = jnp.stack([w_ck2[l], w_cv2[l]]).astype(BF16)
        cmp = _compress_call(xc, w1ab, pos8, w2)
        kc = jnp.concatenate([cmp[:, 0:2].astype(BF16),
                              jnp.broadcast_to(_cmp_consts(nc), (b, hh, nc, AUG_K - dk))], axis=-1)
        vc = cmp[:, 2:4].astype(BF16).transpose(0, 1, 3, 2)

        o_attn = _attn_call(qt, gt, ksa.reshape(b, hh, nkb, Q_TILE, AUG_K), kwa.reshape(b, hh, nkb, Q_TILE, AUG_K),
                            vst, vwt, kc, vc, _sel_map_t(t, nc), min(SEL_TOPK, nsel))

        xx = _ffn_call(xx.reshape(b * t, D_MODEL), o_attn.reshape(b * t, D_ATTN), mixed_conv.reshape(b * t, D_CONV),
                       norm_out_attn_g[l][None], w_out[l].astype(BF16), norm_ffn_g[l][None], w_gate[l].astype(BF16),
                       w_up[l].astype(BF16), ffn_conv_w[l], ffn_conv_b[l][None], w_down[l].astype(BF16),
                       norm_final_g[None], t)
    return xx.reshape(b, t, D_MODEL)
```

```python
import functools

import jax
import jax.numpy as jnp
import numpy as np
from jax import lax
from jax.experimental import pallas as pl
from jax.experimental.pallas import tpu as pltpu

F32 = jnp.float32
BF16 = jnp.bfloat16

D_MODEL = 1024
N_KV_HEADS = 2
GQA_GROUP = 4
N_HEADS_ATTN = N_KV_HEADS * GQA_GROUP
HEAD_DIM = 64
D_ATTN = N_HEADS_ATTN * HEAD_DIM
D_KV = N_KV_HEADS * HEAD_DIM
D_CONV = D_MODEL - D_ATTN
CMP_BLOCK = 32
CMP_STRIDE = 16
CMP_HIDDEN = 2 * HEAD_DIM
SEL_BLOCK = 64
SEL_TOPK = 16
WINDOW = 512
D_FF = 2816
EPS = 1e-6
NEG = -1e30
BIG = 1e30

AUG_K = 256
AUG_ALIBI = HEAD_DIM
AUG_ALIBI_ROWS = 16
AUG_SEL = AUG_ALIBI + AUG_ALIBI_ROWS
V_ROWS = 80

ROW_TILE = 512
Q_TILE = 256
FF_CHUNK = 256
PROJ_COLS = 2944
VMEM_LIMIT = 56 * 1024 * 1024

LOG2E = 1.4426950408889634


def _bf16_terms(x, n):
    out = []
    for _ in range(n):
        t = float(np.asarray(x, np.float32).astype(jnp.bfloat16).astype(np.float32))
        out.append(t)
        x = x - t
    return tuple(out)


LOG2E_3 = _bf16_terms(LOG2E, 3)


def _rms(x, g):
    return x * lax.rsqrt(jnp.mean(x * x, axis=-1, keepdims=True) + EPS) * g


COL_Q = 0
COL_KV = D_ATTN
COL_CMP = COL_KV + 4 * 2 * HEAD_DIM
COL_B = COL_CMP + 2 * D_KV
COL_C = COL_B + D_CONV
COL_U = COL_C + D_CONV
COL_GATE = COL_U + D_CONV
GATE_ROWS = 16


def _proj_kernel(x_ref, g_ref, w_ref, cw_ref, gc_ref, csel_ref, cwin_ref,
                 qt_ref, gt_ref, ksa_ref, kwa_ref, vst_ref, vwt_ref, kvc_ref, mc_ref, cbuf, *, tiles_per_seq):
    i = pl.program_id(0)
    tm = x_ref.shape[0]
    tq = vst_ref.shape[-1]
    h = _rms(x_ref[...], g_ref[...])
    p = jnp.dot(h.astype(BF16), w_ref[...], preferred_element_type=F32)

    for hd in range(N_KV_HEADS):
        qs = p[:, COL_Q + hd * 256:COL_Q + (hd + 1) * 256] * (HEAD_DIM ** -0.5 * LOG2E)
        qt_ref[hd] = qs.T.astype(BF16)
    gt = jax.nn.sigmoid(p[:, COL_GATE:COL_GATE + 128]).T
    for hd in range(N_KV_HEADS):
        gt_ref[hd] = gt[hd * GATE_ROWS:(hd + 1) * GATE_ROWS, :]

    lane = lax.broadcasted_iota(jnp.int32, (tm, 128), 1)
    ones_rows = jnp.where(lax.broadcasted_iota(jnp.int32, (V_ROWS - HEAD_DIM, tq), 0) == 0, 1.0, 0.0).astype(BF16)
    for branch, (ka_ref, vt_ref, c_ref) in enumerate(((ksa_ref, vst_ref, csel_ref), (kwa_ref, vwt_ref, cwin_ref))):
        for hd in range(N_KV_HEADS):
            c0 = COL_KV + (2 * branch + hd) * 128
            slab = p[:, c0:c0 + 128]
            ka_ref[hd, :, 0:128] = jnp.where(lane < HEAD_DIM, slab.astype(BF16), c_ref[:, 0:128])
            ka_ref[hd, :, 128:AUG_K] = c_ref[:, 128:AUG_K]
            vt = slab.T[HEAD_DIM:128, :].astype(BF16)
            for kt in range(tm // tq):
                vt_ref[hd, kt, 0:HEAD_DIM, :] = vt[:, kt * tq:(kt + 1) * tq]
                vt_ref[hd, kt, HEAD_DIM:V_ROWS, :] = ones_rows

    kvc_ref[...] = p[:, COL_CMP:COL_CMP + 2 * D_KV].astype(BF16)
    b = p[:, COL_B:COL_B + D_CONV]
    cu = p[:, COL_C:COL_C + D_CONV] * p[:, COL_U:COL_U + D_CONV]

    @pl.when(i % tiles_per_seq == 0)
    def _():
        cbuf[0:8, :] = jnp.zeros((8, D_CONV), F32)

    @pl.when(i % tiles_per_seq != 0)
    def _():
        cbuf[0:8, :] = cbuf[tm:tm + 8, :]

    cbuf[8:tm + 8, :] = cu
    y = cw_ref[0:1, :] * cbuf[6:tm + 6, :] + cw_ref[1:2, :] * cbuf[7:tm + 7, :] + cw_ref[2:3, :] * cu
    mc_ref[...] = _rms(b * y, gc_ref[...]).astype(BF16)


def _proj_call(x3, g, w, cw, gc, csel, cwin):
    bsz, seq, _ = x3.shape
    tm = ROW_TILE
    tq = Q_TILE
    tps = seq // tm
    hh = N_KV_HEADS
    row = lambda i: (i // tps, i % tps, 0)
    fix = lambda i: (0, 0)
    seq_tile = lambda i: (i % tps, 0)
    tok_minor = lambda i: (i // tps, 0, 0, i % tps)
    tok_major = lambda i: (i // tps, 0, i % tps, 0)
    tok_tiles = lambda i: (i // tps, 0, i % tps, 0, 0)
    return pl.pallas_call(
        functools.partial(_proj_kernel, tiles_per_seq=tps),
        grid=(bsz * tps,),
        in_specs=[
            pl.BlockSpec((None, tm, D_MODEL), row),
            pl.BlockSpec((1, D_MODEL), fix),
            pl.BlockSpec((D_MODEL, PROJ_COLS), fix),
            pl.BlockSpec((3, D_CONV), fix),
            pl.BlockSpec((1, D_CONV), fix),
            pl.BlockSpec((tm, AUG_K), seq_tile),
            pl.BlockSpec((tm, AUG_K), seq_tile),
        ],
        out_specs=[
            pl.BlockSpec((None, hh, GQA_GROUP * HEAD_DIM, tm), tok_minor),
            pl.BlockSpec((None, hh, GATE_ROWS, tm), tok_minor),
            pl.BlockSpec((None, hh, tm, AUG_K), tok_major),
            pl.BlockSpec((None, hh, tm, AUG_K), tok_major),
            pl.BlockSpec((None, hh, tm // tq, V_ROWS, tq), tok_tiles),
            pl.BlockSpec((None, hh, tm // tq, V_ROWS, tq), tok_tiles),
            pl.BlockSpec((None, tm, 2 * D_KV), row),
            pl.BlockSpec((None, tm, D_CONV), row),
        ],
        out_shape=[
            jax.ShapeDtypeStruct((bsz, hh, GQA_GROUP * HEAD_DIM, seq), BF16),
            jax.ShapeDtypeStruct((bsz, hh, GATE_ROWS, seq), F32),
            jax.ShapeDtypeStruct((bsz, hh, seq, AUG_K), BF16),
            jax.ShapeDtypeStruct((bsz, hh, seq, AUG_K), BF16),
            jax.ShapeDtypeStruct((bsz, hh, seq // tq, V_ROWS, tq), BF16),
            jax.ShapeDtypeStruct((bsz, hh, seq // tq, V_ROWS, tq), BF16),
            jax.ShapeDtypeStruct((bsz, seq, 2 * D_KV), BF16),
            jax.ShapeDtypeStruct((bsz, seq, D_CONV), BF16),
        ],
        scratch_shapes=[pltpu.VMEM((tm + 8, D_CONV), F32)],
        compiler_params=pltpu.CompilerParams(dimension_semantics=("arbitrary",), vmem_limit_bytes=VMEM_LIMIT),
    )(x3, g, w, cw, gc, csel, cwin)


def _compress_kernel(x_ref, w1_ref, pos_ref, w2_ref, o_ref, sbuf):
    nc = x_ref.shape[0]
    y = jnp.dot(x_ref[...], w1_ref[...], preferred_element_type=F32)
    pb = jnp.dot(pos_ref[...], w1_ref[...], preferred_element_type=F32)
    posb = pb[0:1, 0:CMP_HIDDEN] + pb[1:2, CMP_HIDDEN:2 * CMP_HIDDEN]
    sbuf[0:nc, :] = y[:, CMP_HIDDEN:2 * CMP_HIDDEN]
    sbuf[nc:nc + 8, :] = jnp.zeros((8, CMP_HIDDEN), F32)
    hid = y[:, 0:CMP_HIDDEN] + sbuf[1:nc + 1, :] + posb
    act = jax.nn.gelu(hid)
    out = jnp.dot(act.astype(BF16), w2_ref[...], preferred_element_type=F32)
    rowi = lax.broadcasted_iota(jnp.int32, out.shape, 0)
    o_ref[...] = jnp.where(rowi < nc - 1, out, 0.0)


def _compress_call(xc, w1ab, pos8, w2):
    b, _, nc, _ = xc.shape
    return pl.pallas_call(
        _compress_kernel,
        grid=(b, 4),
        in_specs=[
            pl.BlockSpec((None, None, nc, 16 * HEAD_DIM), lambda i, s: (i, s, 0, 0)),
            pl.BlockSpec((None, 16 * HEAD_DIM, 2 * CMP_HIDDEN), lambda i, s: (s // 2, 0, 0)),
            pl.BlockSpec((None, 8, 16 * HEAD_DIM), lambda i, s: (s // 2, 0, 0)),
            pl.BlockSpec((None, CMP_HIDDEN, HEAD_DIM), lambda i, s: (s // 2, 0, 0)),
        ],
        out_specs=pl.BlockSpec((None, None, nc, HEAD_DIM), lambda i, s: (i, s, 0, 0)),
        out_shape=jax.ShapeDtypeStruct((b, 4, nc, HEAD_DIM), F32),
        scratch_shapes=[pltpu.VMEM((nc + 8, CMP_HIDDEN), F32)],
        compiler_params=pltpu.CompilerParams(dimension_semantics=("arbitrary", "arbitrary")),
    )(xc, w1ab, pos8, w2)


def _attn_kernel(qt_ref, gt_ref, ks_ref, kw_ref, vs_ref, vw_ref, kc_ref, vc_ref, msel_ref, o_ref,
                 qaug, acc_ref, sbuf, pbuf, klist, *, topk):
    h = pl.program_id(1)
    qi = pl.program_id(2)
    tq = qt_ref.shape[1]
    r = GQA_GROUP * tq
    nc = kc_ref.shape[0]
    nsel = msel_ref.shape[0]
    nkb = ks_ref.shape[0]
    t0 = qi * tq

    lane16 = lax.broadcasted_iota(jnp.int32, (AUG_ALIBI_ROWS, r), 1)
    sub16 = lax.broadcasted_iota(jnp.int32, (AUG_ALIBI_ROWS, r), 0)
    gl = lane16 // tq
    off = (lane16 % tq).astype(F32)
    base = jnp.where(h == 0, 0.5, 0.03125).astype(F32)
    slope = jnp.where(gl == 0, base, jnp.where(gl == 1, base * 0.5, jnp.where(gl == 2, base * 0.25, base * 0.125)))
    blk0 = (t0 // SEL_BLOCK).astype(F32)
    c3 = jnp.where(sub16 % 3 == 0, LOG2E_3[0], jnp.where(sub16 % 3 == 1, LOG2E_3[1], LOG2E_3[2]))
    arow = jnp.where(sub16 < 3, slope * c3,
                     jnp.where(sub16 < 6, 64.0 * slope * c3,
                               jnp.where(sub16 == 6, -slope * LOG2E * (64.0 * blk0 + off), 0.0)))
    for g in range(GQA_GROUP):
        qaug[0:HEAD_DIM, g * tq:(g + 1) * tq] = qt_ref[g * HEAD_DIM:(g + 1) * HEAD_DIM, :]
    qaug[AUG_ALIBI:AUG_SEL, :] = arow.astype(BF16)
    qaug[AUG_SEL:AUG_K, :] = jnp.zeros((AUG_K - AUG_SEL, r), BF16)

    lane_q = lax.broadcasted_iota(jnp.int32, (1, r), 1) % tq
    key_i = lax.broadcasted_iota(jnp.int32, (tq, r), 0)
    causal = key_i <= lane_q

    q_nosel = qaug[...]
    nwin = WINDOW // tq
    scores = []
    for w in range(nwin + 1):
        kb = qi - nwin + w
        kbc = jnp.maximum(kb, 0)
        s = jnp.dot(kw_ref[kbc], q_nosel, preferred_element_type=F32)
        if w == 0:
            s = jnp.where((key_i > lane_q) & (kb >= 0), s, NEG)
        elif w == nwin:
            s = jnp.where(causal, s, NEG)
        else:
            s = jnp.where(kb >= 0, s, NEG)
        scores.append((kbc, s))
    m_w = scores[0][1].max(axis=0, keepdims=True)
    for _, s in scores[1:]:
        m_w = jnp.maximum(m_w, s.max(axis=0, keepdims=True))
    acc_w = jnp.zeros((V_ROWS, r), F32)
    for kbc, s in scores:
        acc_w = acc_w + jnp.dot(vw_ref[kbc], jnp.exp2(s - m_w).astype(BF16), preferred_element_type=F32)
    o_win = acc_w[0:HEAD_DIM, :] * (1.0 / acc_w[HEAD_DIM:HEAD_DIM + 1, :])

    sc = jnp.dot(kc_ref[...], q_nosel, preferred_element_type=F32)
    cend = lax.broadcasted_iota(jnp.int32, (nc, r), 0) * CMP_STRIDE + (CMP_BLOCK - 1)
    mask_c = cend <= (t0 + lane_q)
    sc = jnp.where(mask_c, sc, NEG)
    m_c = jnp.max(sc, axis=0, keepdims=True)
    e_c = jnp.where(mask_c, jnp.exp2(sc - m_c), 0.0)
    l_c = jnp.sum(e_c, axis=0, keepdims=True)
    p_c = e_c * jnp.where(l_c > 0.0, 1.0 / l_c, 0.0)
    o_cmp = jnp.dot(vc_ref[...], p_c.astype(BF16), preferred_element_type=F32)

    p_sum = p_c[:, 0:tq]
    for g in range(1, GQA_GROUP):
        p_sum = p_sum + p_c[:, g * tq:(g + 1) * tq]
    p1 = p_sum.astype(BF16)
    r1 = p_sum - p1.astype(F32)
    p2 = r1.astype(BF16)
    p3 = (r1 - p2.astype(F32)).astype(BF16)
    msel = msel_ref[...]
    imp = (jnp.dot(msel, p1, preferred_element_type=F32) + jnp.dot(msel, p2, preferred_element_type=F32)
           + jnp.dot(msel, p3, preferred_element_type=F32))
    jj = lax.broadcasted_iota(jnp.int32, (nsel, tq), 0)
    jt = (t0 + lax.broadcasted_iota(jnp.int32, (nsel, tq), 1)) // SEL_BLOCK
    imp = jnp.where((jj == 0) | (jj == jt) | (jj == jt - 1), BIG, imp)
    imp = jnp.where(jj > jt, NEG, imp)
    sub8 = lax.broadcasted_iota(jnp.int32, (8, tq), 0)
    groups = [imp[8 * gi:8 * gi + 8, :] for gi in range(nsel // 8)]
    cnts = [jnp.zeros((8, tq), jnp.int32) for _ in groups]
    for jp in range(nsel):
        rowv = jnp.broadcast_to(imp[jp:jp + 1, :], (8, tq))
        for gi, grp in enumerate(groups):
            if 8 * gi > jp:
                beats = rowv >= grp
            elif 8 * gi + 7 <= jp:
                beats = rowv > grp
            else:
                beats = (rowv > grp) | ((rowv == grp) & (sub8 + 8 * gi > jp))
            cnts[gi] = cnts[gi] + jnp.where(beats, 1, 0)
    sels = [c < topk for c in cnts]
    selbias = jnp.concatenate([jnp.where(sl, 0.0, NEG) for sl in sels], axis=0).astype(BF16)
    for g in range(GQA_GROUP):
        qaug[AUG_SEL:AUG_SEL + nsel, g * tq:(g + 1) * tq] = selbias

    bpt = tq // SEL_BLOCK
    n_use = jnp.int32(0)
    for gi, sl in enumerate(sels):
        hit = jnp.max(jnp.where(sl, 1.0, 0.0), axis=1, keepdims=True)
        for part in range(8 // bpt):
            kb = gi * (8 // bpt) + part
            if kb < nkb - 1:
                klist[n_use] = kb
                used = (jnp.max(hit[part * bpt:(part + 1) * bpt, :]) > 0.0) & (kb < qi)
                n_use = n_use + used.astype(jnp.int32)
    klist[n_use] = qi

    q_all = qaug[...]

    def qk(kb):
        return jnp.dot(ks_ref[kb], q_all, preferred_element_type=F32)

    def softmax_tile(s, m_old):
        m_new = jnp.maximum(m_old, jnp.max(s, axis=0, keepdims=True))
        return m_new, jnp.exp2(s - m_new).astype(BF16), jnp.exp2(m_old - m_new)

    def pv(kb, p, alpha):
        acc_ref[...] = acc_ref[...] * alpha + jnp.dot(vs_ref[kb], p, preferred_element_type=F32)

    acc_ref[...] = jnp.zeros(acc_ref.shape, F32)
    pbuf[1] = jnp.zeros((tq, r), BF16)
    sbuf[0] = qk(klist[0])

    def step(j, cur, carry):
        m_old, alpha_prev = carry
        s = sbuf[cur]
        sbuf[1 - cur] = qk(klist[j + 1])
        pv(klist[jnp.maximum(j - 1, 0)], pbuf[1 - cur], alpha_prev)
        m_new, p, alpha = softmax_tile(s, m_old)
        pbuf[cur] = p
        return m_new, alpha

    def finish(cur, carry):
        m_old, alpha_prev = carry
        _, p_last, alpha_last = softmax_tile(jnp.where(causal, sbuf[cur], NEG), m_old)
        pv(klist[jnp.maximum(n_use - 1, 0)], pbuf[1 - cur], alpha_prev)
        pv(qi, p_last, alpha_last)

    def pair(i, carry):
        return step(2 * i + 1, 1, step(2 * i, 0, carry))

    carry0 = (jnp.full((1, r), NEG, F32), jnp.ones((1, r), F32))

    @pl.when(n_use % 2 == 0)
    def _():
        finish(0, lax.fori_loop(0, n_use // 2, pair, carry0))

    @pl.when(n_use % 2 == 1)
    def _():
        finish(1, step(n_use - 1, 0, lax.fori_loop(0, n_use // 2, pair, carry0)))

    acc_s = acc_ref[...]
    o_sel = acc_s[0:HEAD_DIM, :] * (1.0 / acc_s[HEAD_DIM:HEAD_DIM + 1, :])

    gt = gt_ref[...]
    for g in range(GQA_GROUP):
        sl = slice(g * tq, (g + 1) * tq)
        og = (gt[3 * g:3 * g + 1, :] * o_cmp[:, sl] + gt[3 * g + 1:3 * g + 2, :] * o_sel[:, sl]
              + gt[3 * g + 2:3 * g + 3, :] * o_win[:, sl])
        o_ref[:, g * HEAD_DIM:(g + 1) * HEAD_DIM] = og.T


def _attn_call(qt, gt, ks, kw, vs, vw, kc, vc, msel, topk):
    b, hh, _, t = qt.shape
    tq = Q_TILE
    nkb = t // tq
    nc = kc.shape[2]
    nsel = msel.shape[0]
    per_q = lambda i, j, k: (i, j, 0, k)
    per_bh4 = lambda i, j, k: (i, j, 0, 0)
    per_bh5 = lambda i, j, k: (i, j, 0, 0, 0)
    return pl.pallas_call(
        functools.partial(_attn_kernel, topk=topk),
        grid=(b, hh, nkb),
        in_specs=[
            pl.BlockSpec((None, None, GQA_GROUP * HEAD_DIM, tq), per_q),
            pl.BlockSpec((None, None, GATE_ROWS, tq), per_q),
            pl.BlockSpec((None, None, nkb, tq, AUG_K), per_bh5),
            pl.BlockSpec((None, None, nkb, tq, AUG_K), per_bh5),
            pl.BlockSpec((None, None, nkb, V_ROWS, tq), per_bh5),
            pl.BlockSpec((None, None, nkb, V_ROWS, tq), per_bh5),
            pl.BlockSpec((None, None, nc, AUG_K), per_bh4),
            pl.BlockSpec((None, None, HEAD_DIM, nc), per_bh4),
            pl.BlockSpec((nsel, nc), lambda i, j, k: (0, 0)),
        ],
        out_specs=pl.BlockSpec((None, tq, GQA_GROUP * HEAD_DIM), lambda i, j, k: (i, k, j)),
        out_shape=jax.ShapeDtypeStruct((b, t, D_ATTN), F32),
        scratch_shapes=[pltpu.VMEM((AUG_K, GQA_GROUP * tq), BF16), pltpu.VMEM((V_ROWS, GQA_GROUP * tq), F32),
                        pltpu.VMEM((2, tq, GQA_GROUP * tq), F32), pltpu.VMEM((2, tq, GQA_GROUP * tq), BF16),
                        pltpu.SMEM((nkb + 1,), jnp.int32)],
        compiler_params=pltpu.CompilerParams(dimension_semantics=("arbitrary", "arbitrary", "arbitrary"),
                                             vmem_limit_bytes=VMEM_LIMIT),
    )(qt, gt, ks, kw, vs, vw, kc, vc, msel)


def _ffn_kernel(x_ref, oa_ref, mc_ref, ga_ref, wo_ref, gf_ref, wg_ref, wu_ref, cw_ref, cb_ref, wd_ref, gl_ref,
                o_ref, gbuf, ybuf, *, tiles_per_seq):
    i = pl.program_id(0)
    tm = x_ref.shape[0]
    ma = _rms(oa_ref[...], ga_ref[...]).astype(BF16)
    x1 = (x_ref[...] + jnp.dot(ma, wo_ref[0:D_ATTN, :], preferred_element_type=F32)
          + jnp.dot(mc_ref[...], wo_ref[D_ATTN:D_MODEL, :], preferred_element_type=F32))
    h2 = _rms(x1, gf_ref[...]).astype(BF16)

    @pl.when(i % tiles_per_seq == 0)
    def _():
        gbuf[0:8, :] = jnp.zeros((8, D_FF), F32)

    @pl.when(i % tiles_per_seq != 0)
    def _():
        gbuf[0:8, :] = gbuf[tm:tm + 8, :]

    for c in range(D_FF // FF_CHUNK):
        cs = slice(c * FF_CHUNK, (c + 1) * FF_CHUNK)
        gpre = jnp.dot(h2, wg_ref[:, cs], preferred_element_type=F32)
        up = jnp.dot(h2, wu_ref[:, cs], preferred_element_type=F32)
        gbuf[8:tm + 8, cs] = gpre
        gate = (cw_ref[0:1, cs] * gbuf[6:tm + 6, cs] + cw_ref[1:2, cs] * gbuf[7:tm + 7, cs]
                + cw_ref[2:3, cs] * gpre + cb_ref[:, cs])
        ybuf[:, cs] = (jax.nn.silu(gate) * up).astype(BF16)
    acc = x1 + jnp.dot(ybuf[...], wd_ref[...], preferred_element_type=F32)
    o_ref[...] = _rms(acc, gl_ref[...])


def _ffn_call(x2, oa, mc, ga, wo, gf, wg, wu, cw, cb, wd, gl, seq):
    n = x2.shape[0]
    tm = ROW_TILE
    row = lambda i: (i, 0)
    fix = lambda i: (0, 0)
    once = dict(pipeline_mode=pl.Buffered(1))
    return pl.pallas_call(
        functools.partial(_ffn_kernel, tiles_per_seq=seq // tm),
        grid=(n // tm,),
        in_specs=[
            pl.BlockSpec((tm, D_MODEL), row),
            pl.BlockSpec((tm, D_ATTN), row),
            pl.BlockSpec((tm, D_CONV), row),
            pl.BlockSpec((1, D_ATTN), fix),
            pl.BlockSpec((D_MODEL, D_MODEL), fix, **once),
            pl.BlockSpec((1, D_MODEL), fix),
            pl.BlockSpec((D_MODEL, D_FF), fix, **once),
            pl.BlockSpec((D_MODEL, D_FF), fix, **once),
            pl.BlockSpec((3, D_FF), fix),
            pl.BlockSpec((1, D_FF), fix),
            pl.BlockSpec((D_FF, D_MODEL), fix, **once),
            pl.BlockSpec((1, D_MODEL), fix),
        ],
        out_specs=pl.BlockSpec((tm, D_MODEL), row),
        out_shape=jax.ShapeDtypeStruct((n, D_MODEL), F32),
        scratch_shapes=[pltpu.VMEM((tm + 8, D_FF), F32), pltpu.VMEM((tm, D_FF), BF16)],
        compiler_params=pltpu.CompilerParams(dimension_semantics=("arbitrary",), vmem_limit_bytes=VMEM_LIMIT),
    )(x2, oa, mc, ga, wo, gf, wg, wu, cw, cb, wd, gl)


def _alibi_cols(pos):
    cols = np.zeros((len(pos), AUG_ALIBI_ROWS), np.float32)
    cols[:, 0:3] = (pos % SEL_BLOCK)[:, None]
    cols[:, 3:6] = (pos // SEL_BLOCK)[:, None]
    cols[:, 6] = 1.0
    return cols


def _key_consts(t, with_sel):
    pos = np.arange(t)
    c = np.zeros((t, AUG_K), np.float32)
    c[:, AUG_ALIBI:AUG_SEL] = _alibi_cols(pos)
    if with_sel:
        c[pos, AUG_SEL + pos // SEL_BLOCK] = 1.0
    return jnp.asarray(c, BF16)


def _cmp_consts(nc):
    c = np.zeros((nc, AUG_K - HEAD_DIM), np.float32)
    c[:, 0:AUG_ALIBI_ROWS] = _alibi_cols(np.arange(nc) * CMP_STRIDE + (CMP_BLOCK - 1))
    return jnp.asarray(c, BF16)


def _sel_map_t(t, nc):
    n_cmp = (t - CMP_BLOCK) // CMP_STRIDE + 1
    n_sel = t // SEL_BLOCK
    cs = np.arange(n_cmp)[:, None] * CMP_STRIDE
    ss = np.arange(n_sel)[None, :] * SEL_BLOCK
    ov = np.maximum(0, np.minimum(cs + CMP_BLOCK, ss + SEL_BLOCK) - np.maximum(cs, ss)) / CMP_BLOCK
    m = np.zeros((n_sel, nc), np.float32)
    m[:, :n_cmp] = ov.T
    return jnp.asarray(m, BF16)


def kernel(x, norm_mix_g, w_in, pos_ck, w_ck1, w_ck2, pos_cv, w_cv1, w_cv2, conv_mix_w, norm_out_attn_g,
           norm_out_conv_g, w_out, norm_ffn_g, w_gate, w_up, ffn_conv_w, ffn_conv_b, w_down, norm_final_g):
    b, t, _ = x.shape
    hh, dk = N_KV_HEADS, HEAD_DIM
    assert t % ROW_TILE == 0 and t % Q_TILE == 0 and WINDOW % Q_TILE == 0 and t // SEL_BLOCK <= AUG_K - AUG_SEL
    nc = t // CMP_STRIDE
    nsel = t // SEL_BLOCK
    nkb = t // Q_TILE
    depth = w_in.shape[0]
    assert depth == 1
    xx = x
    for l in range(depth):
        wi = w_in[l]
        col = np.cumsum([0, D_ATTN] + [D_KV] * 6 + [3 * N_HEADS_ATTN] + [D_CONV] * 3)
        kv_slabs = [wi[:, col[3 + 2 * br] + hd * dk:col[3 + 2 * br] + (hd + 1) * dk] if part == 0 else
                    wi[:, col[4 + 2 * br] + hd * dk:col[4 + 2 * br] + (hd + 1) * dk]
                    for br in range(2) for hd in range(hh) for part in range(2)]
        gate_cols = [jnp.concatenate([wi[:, col[7] + 12 * hd:col[7] + 12 * (hd + 1)], jnp.zeros((D_MODEL, 4), F32)], axis=1)
                     for hd in range(hh)]
        w_p = jnp.concatenate([wi[:, 0:D_ATTN]] + kv_slabs + [wi[:, col[1]:col[3]], wi[:, col[8]:col[11]]] + gate_cols
                              + [jnp.zeros((D_MODEL, 128 - 2 * GATE_ROWS), F32)], axis=1).astype(BF16)
        assert w_p.shape[1] == PROJ_COLS
        qt, gt, ksa, kwa, vst, vwt, kvc, mixed_conv = _proj_call(
            xx, norm_mix_g[l][None], w_p, conv_mix_w[l], norm_out_conv_g[l][None],
            _key_consts(t, True), _key_consts(t, False))

        xc = kvc.reshape(b, nc, CMP_STRIDE, 4, dk).transpose(0, 3, 1, 2, 4).reshape(b, 4, nc, 16 * dk)
        half = CMP_STRIDE * dk
        w1ab = jnp.stack([jnp.concatenate([w[:half], w[half:]], axis=1) for w in (w_ck1[l], w_cv1[l])]).astype(BF16)
        pos8 = jnp.stack([jnp.concatenate([p.reshape(2, half), jnp.zeros((6, half), F32)], axis=0)
                          for p in (pos_ck[l], pos_cv[l])]).astype(BF16)
        w2 = jnp.stack([w_ck2[l], w_cv2[l]]).astype(BF16)
        cmp = _compress_call(xc, w1ab, pos8, w2)
        kc = jnp.concatenate([cmp[:, 0:2].astype(BF16),
                              jnp.broadcast_to(_cmp_consts(nc), (b, hh, nc, AUG_K - dk))], axis=-1)
        vc = cmp[:, 2:4].astype(BF16).transpose(0, 1, 3, 2)

        o_attn = _attn_call(qt, gt, ksa.reshape(b, hh, nkb, Q_TILE, AUG_K), kwa.reshape(b, hh, nkb, Q_TILE, AUG_K),
                            vst, vwt, kc, vc, _sel_map_t(t, nc), min(SEL_TOPK, nsel))

        xx = _ffn_call(xx.reshape(b * t, D_MODEL), o_attn.reshape(b * t, D_ATTN), mixed_conv.reshape(b * t, D_CONV),
                       norm_out_attn_g[l][None], w_out[l].astype(BF16), norm_ffn_g[l][None], w_gate[l].astype(BF16),
                       w_up[l].astype(BF16), ffn_conv_w[l], ffn_conv_b[l][None], w_down[l].astype(BF16),
                       norm_final_g[None], t)
    return xx.reshape(b, t, D_MODEL)
```

```python
import functools

import jax
import jax.numpy as jnp
import numpy as np
from jax import lax
from jax.experimental import pallas as pl
from jax.experimental.pallas import tpu as pltpu

F32 = jnp.float32
BF16 = jnp.bfloat16

D_MODEL = 1024
N_KV_HEADS = 2
GQA_GROUP = 4
N_HEADS_ATTN = N_KV_HEADS * GQA_GROUP
HEAD_DIM = 64
D_ATTN = N_HEADS_ATTN * HEAD_DIM
D_KV = N_KV_HEADS * HEAD_DIM
D_CONV = D_MODEL - D_ATTN
CMP_BLOCK = 32
CMP_STRIDE = 16
CMP_HIDDEN = 2 * HEAD_DIM
SEL_BLOCK = 64
SEL_TOPK = 16
WINDOW = 512
D_FF = 2816
EPS = 1e-6
NEG = -1e30
BIG = 1e30

AUG_K = 256
AUG_ALIBI = HEAD_DIM
AUG_ALIBI_ROWS = 16
AUG_SEL = AUG_ALIBI + AUG_ALIBI_ROWS
V_ROWS = 80

ROW_TILE = 512
Q_TILE = 256
FF_CHUNK = 256
PROJ_COLS = 2944
VMEM_LIMIT = 56 * 1024 * 1024

LOG2E = 1.4426950408889634


def _bf16_terms(x, n):
    out = []
    for _ in range(n):
        t = float(np.asarray(x, np.float32).astype(jnp.bfloat16).astype(np.float32))
        out.append(t)
        x = x - t
    return tuple(out)


LOG2E_3 = _bf16_terms(LOG2E, 3)


def _rms(x, g):
    return x * lax.rsqrt(jnp.mean(x * x, axis=-1, keepdims=True) + EPS) * g


COL_Q = 0
COL_KV = D_ATTN
COL_CMP = COL_KV + 4 * 2 * HEAD_DIM
COL_B = COL_CMP + 2 * D_KV
COL_C = COL_B + D_CONV
COL_U = COL_C + D_CONV
COL_GATE = COL_U + D_CONV
GATE_ROWS = 16


def _proj_kernel(x_ref, g_ref, w_ref, cw_ref, gc_ref, csel_ref, cwin_ref,
                 qt_ref, gt_ref, ksa_ref, kwa_ref, vst_ref, vwt_ref, kvc_ref, mc_ref, cbuf, *, tiles_per_seq):
    i = pl.program_id(0)
    tm = x_ref.shape[0]
    tq = vst_ref.shape[-1]
    h = _rms(x_ref[...], g_ref[...])
    p = jnp.dot(h.astype(BF16), w_ref[...], preferred_element_type=F32)

    for hd in range(N_KV_HEADS):
        qs = p[:, COL_Q + hd * 256:COL_Q + (hd + 1) * 256] * (HEAD_DIM ** -0.5 * LOG2E)
        qt_ref[hd] = qs.T.astype(BF16)
    gt = jax.nn.sigmoid(p[:, COL_GATE:COL_GATE + 128]).T
    for hd in range(N_KV_HEADS):
        gt_ref[hd] = gt[hd * GATE_ROWS:(hd + 1) * GATE_ROWS, :]

    lane = lax.broadcasted_iota(jnp.int32, (tm, 128), 1)
    ones_rows = jnp.where(lax.broadcasted_iota(jnp.int32, (V_ROWS - HEAD_DIM, tq), 0) == 0, 1.0, 0.0).astype(BF16)
    for branch, (ka_ref, vt_ref, c_ref) in enumerate(((ksa_ref, vst_ref, csel_ref), (kwa_ref, vwt_ref, cwin_ref))):
        for hd in range(N_KV_HEADS):
            c0 = COL_KV + (2 * branch + hd) * 128
            slab = p[:, c0:c0 + 128]
            ka_ref[hd, :, 0:128] = jnp.where(lane < HEAD_DIM, slab.astype(BF16), c_ref[:, 0:128])
            ka_ref[hd, :, 128:AUG_K] = c_ref[:, 128:AUG_K]
            vt = slab.T[HEAD_DIM:128, :].astype(BF16)
            for kt in range(tm // tq):
                vt_ref[hd, kt, 0:HEAD_DIM, :] = vt[:, kt * tq:(kt + 1) * tq]
                vt_ref[hd, kt, HEAD_DIM:V_ROWS, :] = ones_rows

    kvc_ref[...] = p[:, COL_CMP:COL_CMP + 2 * D_KV].astype(BF16)
    b = p[:, COL_B:COL_B + D_CONV]
    cu = p[:, COL_C:COL_C + D_CONV] * p[:, COL_U:COL_U + D_CONV]

    @pl.when(i % tiles_per_seq == 0)
    def _():
        cbuf[0:8, :] = jnp.zeros((8, D_CONV), F32)

    @pl.when(i % tiles_per_seq != 0)
    def _():
        cbuf[0:8, :] = cbuf[tm:tm + 8, :]

    cbuf[8:tm + 8, :] = cu
    y = cw_ref[0:1, :] * cbuf[6:tm + 6, :] + cw_ref[1:2, :] * cbuf[7:tm + 7, :] + cw_ref[2:3, :] * cu
    mc_ref[...] = _rms(b * y, gc_ref[...]).astype(BF16)


def _proj_call(x3, g, w, cw, gc, csel, cwin):
    bsz, seq, _ = x3.shape
    tm = ROW_TILE
    tq = Q_TILE
    tps = seq // tm
    hh = N_KV_HEADS
    row = lambda i: (i // tps, i % tps, 0)
    fix = lambda i: (0, 0)
    seq_tile = lambda i: (i % tps, 0)
    tok_minor = lambda i: (i // tps, 0, 0, i % tps)
    tok_major = lambda i: (i // tps, 0, i % tps, 0)
    tok_tiles = lambda i: (i // tps, 0, i % tps, 0, 0)
    return pl.pallas_call(
        functools.partial(_proj_kernel, tiles_per_seq=tps),
        grid=(bsz * tps,),
        in_specs=[
            pl.BlockSpec((None, tm, D_MODEL), row),
            pl.BlockSpec((1, D_MODEL), fix),
            pl.BlockSpec((D_MODEL, PROJ_COLS), fix),
            pl.BlockSpec((3, D_CONV), fix),
            pl.BlockSpec((1, D_CONV), fix),
            pl.BlockSpec((tm, AUG_K), seq_tile),
            pl.BlockSpec((tm, AUG_K), seq_tile),
        ],
        out_specs=[
            pl.BlockSpec((None, hh, GQA_GROUP * HEAD_DIM, tm), tok_minor),
            pl.BlockSpec((None, hh, GATE_ROWS, tm), tok_minor),
            pl.BlockSpec((None, hh, tm, AUG_K), tok_major),
            pl.BlockSpec((None, hh, tm, AUG_K), tok_major),
            pl.BlockSpec((None, hh, tm // tq, V_ROWS, tq), tok_tiles),
            pl.BlockSpec((None, hh, tm // tq, V_ROWS, tq), tok_tiles),
            pl.BlockSpec((None, tm, 2 * D_KV), row),
            pl.BlockSpec((None, tm, D_CONV), row),
        ],
        out_shape=[
            jax.ShapeDtypeStruct((bsz, hh, GQA_GROUP * HEAD_DIM, seq), BF16),
            jax.ShapeDtypeStruct((bsz, hh, GATE_ROWS, seq), F32),
            jax.ShapeDtypeStruct((bsz, hh, seq, AUG_K), BF16),
            jax.ShapeDtypeStruct((bsz, hh, seq, AUG_K), BF16),
            jax.ShapeDtypeStruct((bsz, hh, seq // tq, V_ROWS, tq), BF16),
            jax.ShapeDtypeStruct((bsz, hh, seq // tq, V_ROWS, tq), BF16),
            jax.ShapeDtypeStruct((bsz, seq, 2 * D_KV), BF16),
            jax.ShapeDtypeStruct((bsz, seq, D_CONV), BF16),
        ],
        scratch_shapes=[pltpu.VMEM((tm + 8, D_CONV), F32)],
        compiler_params=pltpu.CompilerParams(dimension_semantics=("arbitrary",), vmem_limit_bytes=VMEM_LIMIT),
    )(x3, g, w, cw, gc, csel, cwin)


def _compress_kernel(x_ref, w1_ref, pos_ref, w2_ref, o_ref, sbuf):
    nc = x_ref.shape[0]
    y = jnp.dot(x_ref[...], w1_ref[...], preferred_element_type=F32)
    pb = jnp.dot(pos_ref[...], w1_ref[...], preferred_element_type=F32)
    posb = pb[0:1, 0:CMP_HIDDEN] + pb[1:2, CMP_HIDDEN:2 * CMP_HIDDEN]
    sbuf[0:nc, :] = y[:, CMP_HIDDEN:2 * CMP_HIDDEN]
    sbuf[nc:nc + 8, :] = jnp.zeros((8, CMP_HIDDEN), F32)
    hid = y[:, 0:CMP_HIDDEN] + sbuf[1:nc + 1, :] + posb
    act = jax.nn.gelu(hid)
    out = jnp.dot(act.astype(BF16), w2_ref[...], preferred_element_type=F32)
    rowi = lax.broadcasted_iota(jnp.int32, out.shape, 0)
    o_ref[...] = jnp.where(rowi < nc - 1, out, 0.0)


def _compress_call(xc, w1ab, pos8, w2):
    b, _, nc, _ = xc.shape
    return pl.pallas_call(
        _compress_kernel,
        grid=(b, 4),
        in_specs=[
            pl.BlockSpec((None, None, nc, 16 * HEAD_DIM), lambda i, s: (i, s, 0, 0)),
            pl.BlockSpec((None, 16 * HEAD_DIM, 2 * CMP_HIDDEN), lambda i, s: (s // 2, 0, 0)),
            pl.BlockSpec((None, 8, 16 * HEAD_DIM), lambda i, s: (s // 2, 0, 0)),
            pl.BlockSpec((None, CMP_HIDDEN, HEAD_DIM), lambda i, s: (s // 2, 0, 0)),
        ],
        out_specs=pl.BlockSpec((None, None, nc, HEAD_DIM), lambda i, s: (i, s, 0, 0)),
        out_shape=jax.ShapeDtypeStruct((b, 4, nc, HEAD_DIM), F32),
        scratch_shapes=[pltpu.VMEM((nc + 8, CMP_HIDDEN), F32)],
        compiler_params=pltpu.CompilerParams(dimension_semantics=("arbitrary", "arbitrary")),
    )(xc, w1ab, pos8, w2)


def _attn_kernel(qt_ref, gt_ref, ks_ref, kw_ref, vs_ref, vw_ref, kc_ref, vc_ref, msel_ref, cmask_ref, o_ref,
                 qaug, acc_ref, accw_ref, sbuf, pbuf, klist, *, topk):
    h = pl.program_id(1)
    qi = pl.program_id(2)
    tq = qt_ref.shape[1]
    r = GQA_GROUP * tq
    nc = kc_ref.shape[0]
    nsel = msel_ref.shape[0]
    nkb = ks_ref.shape[0]
    t0 = qi * tq

    lane16 = lax.broadcasted_iota(jnp.int32, (AUG_ALIBI_ROWS, r), 1)
    sub16 = lax.broadcasted_iota(jnp.int32, (AUG_ALIBI_ROWS, r), 0)
    gl = lane16 // tq
    off = (lane16 % tq).astype(F32)
    base = jnp.where(h == 0, 0.5, 0.03125).astype(F32)
    slope = jnp.where(gl == 0, base, jnp.where(gl == 1, base * 0.5, jnp.where(gl == 2, base * 0.25, base * 0.125)))
    blk0 = (t0 // SEL_BLOCK).astype(F32)
    c3 = jnp.where(sub16 % 3 == 0, LOG2E_3[0], jnp.where(sub16 % 3 == 1, LOG2E_3[1], LOG2E_3[2]))
    arow = jnp.where(sub16 < 3, slope * c3,
                     jnp.where(sub16 < 6, 64.0 * slope * c3,
                               jnp.where(sub16 == 6, -slope * LOG2E * (64.0 * blk0 + off), 0.0)))
    for g in range(GQA_GROUP):
        qaug[0:HEAD_DIM, g * tq:(g + 1) * tq] = qt_ref[g * HEAD_DIM:(g + 1) * HEAD_DIM, :]
    qaug[AUG_ALIBI:AUG_SEL, :] = arow.astype(BF16)
    qaug[AUG_SEL:AUG_K, :] = jnp.zeros((AUG_K - AUG_SEL, r), BF16)

    lane_q = lax.broadcasted_iota(jnp.int32, (1, r), 1) % tq
    key_i = lax.broadcasted_iota(jnp.int32, (tq, r), 0)
    causal = key_i <= lane_q

    q_nosel = qaug[...]
    nwin = WINDOW // tq

    def win_scores(w):
        kb = qi - nwin + w
        s = jnp.dot(kw_ref[jnp.maximum(kb, 0)], q_nosel, preferred_element_type=F32)
        if w == 0:
            return jnp.where((key_i > lane_q) & (kb >= 0), s, NEG)
        if w == nwin:
            return jnp.where(causal, s, NEG)
        return jnp.where(kb >= 0, s, NEG)

    def win_update(w, s, m_old):
        m_new = jnp.maximum(m_old, jnp.max(s, axis=0, keepdims=True))
        p = jnp.exp2(s - m_new).astype(BF16)
        pv_w = jnp.dot(vw_ref[jnp.maximum(qi - nwin + w, 0)], p, preferred_element_type=F32)
        accw_ref[...] = pv_w if w == 0 else accw_ref[...] * jnp.exp2(m_old - m_new) + pv_w
        return m_new

    sc = jnp.dot(kc_ref[...], q_nosel, preferred_element_type=F32)
    win_s = [win_scores(0)]
    mask_c = cmask_ref[...] <= t0
    sc = jnp.where(mask_c, sc, NEG)
    m_c = jnp.max(sc, axis=0, keepdims=True)
    e_c = jnp.exp2(sc - m_c)
    l_c = jnp.sum(e_c, axis=0, keepdims=True)
    p_c = e_c * jnp.where(m_c > 0.5 * NEG, 1.0 / l_c, 0.0)
    o_cmp = jnp.dot(vc_ref[...], p_c.astype(BF16), preferred_element_type=F32)

    p_sum = p_c[:, 0:tq]
    for g in range(1, GQA_GROUP):
        p_sum = p_sum + p_c[:, g * tq:(g + 1) * tq]
    p1 = p_sum.astype(BF16)
    r1 = p_sum - p1.astype(F32)
    p2 = r1.astype(BF16)
    p3 = (r1 - p2.astype(F32)).astype(BF16)
    msel = msel_ref[...]
    imp = (jnp.dot(msel, p1, preferred_element_type=F32) + jnp.dot(msel, p2, preferred_element_type=F32)
           + jnp.dot(msel, p3, preferred_element_type=F32))

    m_w = jnp.full((1, r), NEG, F32)
    for w in range(nwin):
        win_s.append(win_scores(w + 1))
        m_w = win_update(w, win_s[w], m_w)

    jj =lax.broadcasted_iota(jnp.int32, (nsel, tq), 0)
    jt = (t0 + lax.broadcasted_iota(jnp.int32, (nsel, tq), 1)) // SEL_BLOCK
    imp = jnp.where((jj == 0) | (jj == jt) | (jj == jt - 1), BIG, imp)
    imp = jnp.where(jj > jt, NEG, imp)
    sub8 = lax.broadcasted_iota(jnp.int32, (8, tq), 0)
    groups = [imp[8 * gi:8 * gi + 8, :] for gi in range(nsel // 8)]
    cnts = [jnp.zeros((8, tq), jnp.int32) for _ in groups]
    for jp in range(nsel):
        rowv = jnp.broadcast_to(imp[jp:jp + 1, :], (8, tq))
        for gi, grp in enumerate(groups):
            if 8 * gi > jp:
                beats = rowv >= grp
            elif 8 * gi + 7 <= jp:
                beats = rowv > grp
            else:
                beats = (rowv > grp) | ((rowv == grp) & (sub8 + 8 * gi > jp))
            cnts[gi] = cnts[gi] + jnp.where(beats, 1, 0)
    sels = [c < topk for c in cnts]
    selbias = jnp.concatenate([jnp.where(sl, 0.0, NEG) for sl in sels], axis=0).astype(BF16)
    for g in range(GQA_GROUP):
        qaug[AUG_SEL:AUG_SEL + nsel, g * tq:(g + 1) * tq] = selbias

    q_all = qaug[...]
    sbuf[0] = jnp.dot(ks_ref[0], q_all, preferred_element_type=F32)
    win_update(nwin, win_s[nwin], m_w)
    acc_w = accw_ref[...]
    o_win = acc_w[0:HEAD_DIM, :] * (1.0 / acc_w[HEAD_DIM:HEAD_DIM + 1, :])

    bpt = tq // SEL_BLOCK
    n_use = jnp.int32(0)
    for gi, sl in enumerate(sels):
        hit = jnp.max(jnp.where(sl, 1.0, 0.0), axis=1, keepdims=True)
        for part in range(8 // bpt):
            kb = gi * (8 // bpt) + part
            if kb < nkb - 1:
                klist[n_use] = kb
                used = (jnp.max(hit[part * bpt:(part + 1) * bpt, :]) > 0.0) & (kb < qi)
                n_use = n_use + used.astype(jnp.int32)
    klist[n_use] = qi

    def qk(kb):
        return jnp.dot(ks_ref[kb], q_all, preferred_element_type=F32)

    def softmax_tile(s, m_old):
        m_new = jnp.maximum(m_old, jnp.max(s, axis=0, keepdims=True))
        return m_new, jnp.exp2(s - m_new).astype(BF16), jnp.exp2(m_old - m_new)

    def pv(kb, p, alpha):
        acc_ref[...] = acc_ref[...] * alpha + jnp.dot(vs_ref[kb], p, preferred_element_type=F32)

    acc_ref[...] = jnp.zeros(acc_ref.shape, F32)
    pbuf[1] = jnp.zeros((tq, r), BF16)

    def step(j, cur, carry):
        m_old, alpha_prev = carry
        s = sbuf[cur]
        sbuf[1 - cur] = qk(klist[j + 1])
        pv(klist[jnp.maximum(j - 1, 0)], pbuf[1 - cur], alpha_prev)
        m_new, p, alpha = softmax_tile(s, m_old)
        pbuf[cur] = p
        return m_new, alpha

    def finish(cur, carry):
        m_old, alpha_prev = carry
        _, p_last, alpha_last = softmax_tile(jnp.where(causal, sbuf[cur], NEG), m_old)
        pv(klist[jnp.maximum(n_use - 1, 0)], pbuf[1 - cur], alpha_prev)
        pv(qi, p_last, alpha_last)

    def pair(i, carry):
        return step(2 * i + 1, 1, step(2 * i, 0, carry))

    carry0 = (jnp.full((1, r), NEG, F32), jnp.ones((1, r), F32))

    @pl.when(n_use % 2 == 0)
    def _():
        finish(0, lax.fori_loop(0, n_use // 2, pair, carry0))

    @pl.when(n_use % 2 == 1)
    def _():
        finish(1, step(n_use - 1, 0, lax.fori_loop(0, n_use // 2, pair, carry0)))

    acc_s = acc_ref[...]
    o_sel = acc_s[0:HEAD_DIM, :] * (1.0 / acc_s[HEAD_DIM:HEAD_DIM + 1, :])

    gt = gt_ref[...]
    for g in range(GQA_GROUP):
        sl = slice(g * tq, (g + 1) * tq)
        og = (gt[3 * g:3 * g + 1, :] * o_cmp[:, sl] + gt[3 * g + 1:3 * g + 2, :] * o_sel[:, sl]
              + gt[3 * g + 2:3 * g + 3, :] * o_win[:, sl])
        o_ref[:, g * HEAD_DIM:(g + 1) * HEAD_DIM] = og.T


def _attn_call(qt, gt, ks, kw, vs, vw, kc, vc, msel, cmask, topk):
    b, hh, _, t = qt.shape
    tq = Q_TILE
    nkb = t // tq
    nc = kc.shape[2]
    nsel = msel.shape[0]
    per_q = lambda i, j, k: (i, j, 0, k)
    per_bh4 = lambda i, j, k: (i, j, 0, 0)
    per_bh5 = lambda i, j, k: (i, j, 0, 0, 0)
    return pl.pallas_call(
        functools.partial(_attn_kernel, topk=topk),
        grid=(b, hh, nkb),
        in_specs=[
            pl.BlockSpec((None, None, GQA_GROUP * HEAD_DIM, tq), per_q),
            pl.BlockSpec((None, None, GATE_ROWS, tq), per_q),
            pl.BlockSpec((None, None, nkb, tq, AUG_K), per_bh5),
            pl.BlockSpec((None, None, nkb, tq, AUG_K), per_bh5),
            pl.BlockSpec((None, None, nkb, V_ROWS, tq), per_bh5),
            pl.BlockSpec((None, None, nkb, V_ROWS, tq), per_bh5),
            pl.BlockSpec((None, None, nc, AUG_K), per_bh4),
            pl.BlockSpec((None, None, HEAD_DIM, nc), per_bh4),
            pl.BlockSpec((nsel, nc), lambda i, j, k: (0, 0)),
            pl.BlockSpec((nc, GQA_GROUP * tq), lambda i, j, k: (0, 0)),
        ],
        out_specs=pl.BlockSpec((None, tq, GQA_GROUP * HEAD_DIM), lambda i, j, k: (i, k, j)),
        out_shape=jax.ShapeDtypeStruct((b, t, D_ATTN), F32),
        scratch_shapes=[pltpu.VMEM((AUG_K, GQA_GROUP * tq), BF16), pltpu.VMEM((V_ROWS, GQA_GROUP * tq), F32),
                        pltpu.VMEM((V_ROWS, GQA_GROUP * tq), F32),
                        pltpu.VMEM((2, tq, GQA_GROUP * tq), F32), pltpu.VMEM((2, tq, GQA_GROUP * tq), BF16),
                        pltpu.SMEM((nkb + 1,), jnp.int32)],
        compiler_params=pltpu.CompilerParams(dimension_semantics=("arbitrary", "arbitrary", "arbitrary"),
                                             vmem_limit_bytes=VMEM_LIMIT),
    )(qt, gt, ks, kw, vs, vw, kc, vc, msel, cmask)


def _ffn_kernel(x_ref, oa_ref, mc_ref, ga_ref, wo_ref, gf_ref, wg_ref, wu_ref, cw_ref, cb_ref, wd_ref, gl_ref,
                o_ref, gbuf, ybuf, *, tiles_per_seq):
    i = pl.program_id(0)
    tm = x_ref.shape[0]
    ma = _rms(oa_ref[...], ga_ref[...]).astype(BF16)
    x1 = (x_ref[...] + jnp.dot(ma, wo_ref[0:D_ATTN, :], preferred_element_type=F32)
          + jnp.dot(mc_ref[...], wo_ref[D_ATTN:D_MODEL, :], preferred_element_type=F32))
    h2 = _rms(x1, gf_ref[...]).astype(BF16)

    @pl.when(i % tiles_per_seq == 0)
    def _():
        gbuf[0:8, :] = jnp.zeros((8, D_FF), F32)

    @pl.when(i % tiles_per_seq != 0)
    def _():
        gbuf[0:8, :] = gbuf[tm:tm + 8, :]

    for c in range(D_FF // FF_CHUNK):
        cs = slice(c * FF_CHUNK, (c + 1) * FF_CHUNK)
        gpre = jnp.dot(h2, wg_ref[:, cs], preferred_element_type=F32)
        up = jnp.dot(h2, wu_ref[:, cs], preferred_element_type=F32)
        gbuf[8:tm + 8, cs] = gpre
        gate = (cw_ref[0:1, cs] * gbuf[6:tm + 6, cs] + cw_ref[1:2, cs] * gbuf[7:tm + 7, cs]
                + cw_ref[2:3, cs] * gpre + cb_ref[:, cs])
        ybuf[:, cs] = (jax.nn.silu(gate) * up).astype(BF16)
    acc = x1 + jnp.dot(ybuf[...], wd_ref[...], preferred_element_type=F32)
    o_ref[...] = _rms(acc, gl_ref[...])


def _ffn_call(x2, oa, mc, ga, wo, gf, wg, wu, cw, cb, wd, gl, seq):
    n = x2.shape[0]
    tm = ROW_TILE
    row = lambda i: (i, 0)
    fix = lambda i: (0, 0)
    once = dict(pipeline_mode=pl.Buffered(1))
    return pl.pallas_call(
        functools.partial(_ffn_kernel, tiles_per_seq=seq // tm),
        grid=(n // tm,),
        in_specs=[
            pl.BlockSpec((tm, D_MODEL), row),
            pl.BlockSpec((tm, D_ATTN), row),
            pl.BlockSpec((tm, D_CONV), row),
            pl.BlockSpec((1, D_ATTN), fix),
            pl.BlockSpec((D_MODEL, D_MODEL), fix, **once),
            pl.BlockSpec((1, D_MODEL), fix),
            pl.BlockSpec((D_MODEL, D_FF), fix, **once),
            pl.BlockSpec((D_MODEL, D_FF), fix, **once),
            pl.BlockSpec((3, D_FF), fix),
            pl.BlockSpec((1, D_FF), fix),
            pl.BlockSpec((D_FF, D_MODEL), fix, **once),
            pl.BlockSpec((1, D_MODEL), fix),
        ],
        out_specs=pl.BlockSpec((tm, D_MODEL), row),
        out_shape=jax.ShapeDtypeStruct((n, D_MODEL), F32),
        scratch_shapes=[pltpu.VMEM((tm + 8, D_FF), F32), pltpu.VMEM((tm, D_FF), BF16)],
        compiler_params=pltpu.CompilerParams(dimension_semantics=("arbitrary",), vmem_limit_bytes=VMEM_LIMIT),
    )(x2, oa, mc, ga, wo, gf, wg, wu, cw, cb, wd, gl)


def _alibi_cols(pos):
    cols = np.zeros((len(pos), AUG_ALIBI_ROWS), np.float32)
    cols[:, 0:3] = (pos % SEL_BLOCK)[:, None]
    cols[:, 3:6] = (pos // SEL_BLOCK)[:, None]
    cols[:, 6] = 1.0
    return cols


def _key_consts(t, with_sel):
    pos = np.arange(t)
    c = np.zeros((t, AUG_K), np.float32)
    c[:, AUG_ALIBI:AUG_SEL] = _alibi_cols(pos)
    if with_sel:
        c[pos, AUG_SEL + pos // SEL_BLOCK] = 1.0
    return jnp.asarray(c, BF16)


def _cmp_consts(nc):
    c = np.zeros((nc, AUG_K - HEAD_DIM), np.float32)
    c[:, 0:AUG_ALIBI_ROWS] = _alibi_cols(np.arange(nc) * CMP_STRIDE + (CMP_BLOCK - 1))
    return jnp.asarray(c, BF16)


def _cmp_limits(nc):
    end = np.arange(nc)[:, None] * CMP_STRIDE + (CMP_BLOCK - 1)
    off = np.arange(GQA_GROUP * Q_TILE)[None, :] % Q_TILE
    return jnp.asarray(end - off, jnp.int32)


def _sel_map_t(t, nc):
    n_cmp = (t - CMP_BLOCK) // CMP_STRIDE + 1
    n_sel = t // SEL_BLOCK
    cs = np.arange(n_cmp)[:, None] * CMP_STRIDE
    ss = np.arange(n_sel)[None, :] * SEL_BLOCK
    ov = np.maximum(0, np.minimum(cs + CMP_BLOCK, ss + SEL_BLOCK) - np.maximum(cs, ss)) / CMP_BLOCK
    m = np.zeros((n_sel, nc), np.float32)
    m[:, :n_cmp] = ov.T
    return jnp.asarray(m, BF16)


def kernel(x, norm_mix_g, w_in, pos_ck, w_ck1, w_ck2, pos_cv, w_cv1, w_cv2, conv_mix_w, norm_out_attn_g,
           norm_out_conv_g, w_out, norm_ffn_g, w_gate, w_up, ffn_conv_w, ffn_conv_b, w_down, norm_final_g):
    b, t, _ = x.shape
    hh, dk = N_KV_HEADS, HEAD_DIM
    assert t % ROW_TILE == 0 and t % Q_TILE == 0 and WINDOW % Q_TILE == 0 and t // SEL_BLOCK <= AUG_K - AUG_SEL
    nc = t // CMP_STRIDE
    nsel = t // SEL_BLOCK
    nkb = t // Q_TILE
    depth = w_in.shape[0]
    assert depth == 1
    xx = x
    for l in range(depth):
        wi = w_in[l]
        col = np.cumsum([0, D_ATTN] + [D_KV] * 6 + [3 * N_HEADS_ATTN] + [D_CONV] * 3)
        kv_slabs = [wi[:, col[3 + 2 * br] + hd * dk:col[3 + 2 * br] + (hd + 1) * dk] if part == 0 else
                    wi[:, col[4 + 2 * br] + hd * dk:col[4 + 2 * br] + (hd + 1) * dk]
                    for br in range(2) for hd in range(hh) for part in range(2)]
        gate_cols = [jnp.concatenate([wi[:, col[7] + 12 * hd:col[7] + 12 * (hd + 1)], jnp.zeros((D_MODEL, 4), F32)], axis=1)
                     for hd in range(hh)]
        w_p = jnp.concatenate([wi[:, 0:D_ATTN]] + kv_slabs + [wi[:, col[1]:col[3]], wi[:, col[8]:col[11]]] + gate_cols
                              + [jnp.zeros((D_MODEL, 128 - 2 * GATE_ROWS), F32)], axis=1).astype(BF16)
        assert w_p.shape[1] == PROJ_COLS
        qt, gt, ksa, kwa, vst, vwt, kvc, mixed_conv = _proj_call(
            xx, norm_mix_g[l][None], w_p, conv_mix_w[l], norm_out_conv_g[l][None],
            _key_consts(t, True), _key_consts(t, False))

        xc = kvc.reshape(b, nc, CMP_STRIDE, 4, dk).transpose(0, 3, 1, 2, 4).reshape(b, 4, nc, 16 * dk)
        half = CMP_STRIDE * dk
        w1ab = jnp.stack([jnp.concatenate([w[:half], w[half:]], axis=1) for w in (w_ck1[l], w_cv1[l])]).astype(BF16)
        pos8 = jnp.stack([jnp.concatenate([p.reshape(2, half), jnp.zeros((6, half), F32)], axis=0)
                          for p in (pos_ck[l], pos_cv[l])]).astype(BF16)
        w2 = jnp.stack([w_ck2[l], w_cv2[l]]).astype(BF16)
        cmp = _compress_call(xc, w1ab, pos8, w2)
        kc = jnp.concatenate([cmp[:, 0:2].astype(BF16),
                              jnp.broadcast_to(_cmp_consts(nc), (b, hh, nc, AUG_K - dk))], axis=-1)
        vc = cmp[:, 2:4].astype(BF16).transpose(0, 1, 3, 2)

        o_attn = _attn_call(qt, gt, ksa.reshape(b, hh, nkb, Q_TILE, AUG_K), kwa.reshape(b, hh, nkb, Q_TILE, AUG_K),
                            vst, vwt, kc, vc, _sel_map_t(t, nc), _cmp_limits(nc), min(SEL_TOPK, nsel))

        xx = _ffn_call(xx.reshape(b * t, D_MODEL), o_attn.reshape(b * t, D_ATTN), mixed_conv.reshape(b * t, D_CONV),
                       norm_out_attn_g[l][None], w_out[l].astype(BF16), norm_ffn_g[l][None], w_gate[l].astype(BF16),
                       w_up[l].astype(BF16), ffn_conv_w[l], ffn_conv_b[l][None], w_down[l].astype(BF16),
                       norm_final_g[None], t)
    return xx.reshape(b, t, D_MODEL)
```

```python
import functools

import jax
import jax.numpy as jnp
import numpy as np
from jax import lax
from jax.experimental import pallas as pl
from jax.experimental.pallas import tpu as pltpu

F32 = jnp.float32
BF16 = jnp.bfloat16

D_MODEL = 1024
N_KV_HEADS = 2
GQA_GROUP = 4
N_HEADS_ATTN = N_KV_HEADS * GQA_GROUP
HEAD_DIM = 64
D_ATTN = N_HEADS_ATTN * HEAD_DIM
D_KV = N_KV_HEADS * HEAD_DIM
D_CONV = D_MODEL - D_ATTN
CMP_BLOCK = 32
CMP_STRIDE = 16
CMP_HIDDEN = 2 * HEAD_DIM
SEL_BLOCK = 64
SEL_TOPK = 16
WINDOW = 512
D_FF = 2816
EPS = 1e-6
NEG = -1e30
BIG = 1e30

AUG_K = 256
AUG_ALIBI = HEAD_DIM
AUG_ALIBI_ROWS = 16
AUG_SEL = AUG_ALIBI + AUG_ALIBI_ROWS
V_ROWS = 80

ROW_TILE = 512
SUB_ROWS = 256
Q_TILE = 256
FF_CHUNK = 256
PROJ_COLS = 2944
VMEM_LIMIT = 56 * 1024 * 1024

LOG2E = 1.4426950408889634


def _bf16_terms(x, n):
    out = []
    for _ in range(n):
        t = float(np.asarray(x, np.float32).astype(jnp.bfloat16).astype(np.float32))
        out.append(t)
        x = x - t
    return tuple(out)


LOG2E_3 = _bf16_terms(LOG2E, 3)


def _rms(x, g):
    return x * lax.rsqrt(jnp.mean(x * x, axis=-1, keepdims=True) + EPS) * g


COL_Q = 0
COL_KV = D_ATTN
COL_CMP = COL_KV + 4 * 2 * HEAD_DIM
COL_B = COL_CMP + 2 * D_KV
COL_C = COL_B + D_CONV
COL_U = COL_C + D_CONV
COL_GATE = COL_U + D_CONV
GATE_ROWS = 16


def _proj_kernel(x_ref, g_ref, w_ref, cw_ref, gc_ref, csel_ref, cwin_ref,
                 qt_ref, gt_ref, ksa_ref, kwa_ref, vst_ref, vwt_ref, kvc_ref, mc_ref, cbuf, *, tiles_per_seq):
    i = pl.program_id(0)
    tm = x_ref.shape[0]
    tq = vst_ref.shape[-1]

    @pl.when(i % tiles_per_seq == 0)
    def _():
        cbuf[0:8, :] = jnp.zeros((8, D_CONV), F32)

    @pl.when(i % tiles_per_seq != 0)
    def _():
        cbuf[0:8, :] = cbuf[tm:tm + 8, :]

    lane = lax.broadcasted_iota(jnp.int32, (SUB_ROWS, 128), 1)
    ones_rows = jnp.where(lax.broadcasted_iota(jnp.int32, (V_ROWS - HEAD_DIM, tq), 0) == 0, 1.0, 0.0).astype(BF16)
    for sub in range(tm // SUB_ROWS):
        r0 = sub * SUB_ROWS
        rows = slice(r0, r0 + SUB_ROWS)
        h = _rms(x_ref[rows, :], g_ref[...])
        p = jnp.dot(h.astype(BF16), w_ref[...], preferred_element_type=F32)

        for hd in range(N_KV_HEADS):
            qs = p[:, COL_Q + hd * 256:COL_Q + (hd + 1) * 256] * (HEAD_DIM ** -0.5 * LOG2E)
            qt_ref[hd, :, rows] = qs.T.astype(BF16)
        gt = jax.nn.sigmoid(p[:, COL_GATE:COL_GATE + 128]).T
        for hd in range(N_KV_HEADS):
            gt_ref[hd, :, rows] = gt[hd * GATE_ROWS:(hd + 1) * GATE_ROWS, :]

        for branch, (ka_ref, vt_ref, c_ref) in enumerate(((ksa_ref, vst_ref, csel_ref), (kwa_ref, vwt_ref, cwin_ref))):
            for hd in range(N_KV_HEADS):
                c0 = COL_KV + (2 * branch + hd) * 128
                slab = p[:, c0:c0 + 128]
                ka_ref[hd, rows, 0:128] = jnp.where(lane < HEAD_DIM, slab.astype(BF16), c_ref[rows, 0:128])
                ka_ref[hd, rows, 128:AUG_K] = c_ref[rows, 128:AUG_K]
                vt = slab.T[HEAD_DIM:128, :].astype(BF16)
                for kt in range(SUB_ROWS // tq):
                    vt_ref[hd, r0 // tq + kt, 0:HEAD_DIM, :] = vt[:, kt * tq:(kt + 1) * tq]
                    vt_ref[hd, r0 // tq + kt, HEAD_DIM:V_ROWS, :] = ones_rows

        kvc_ref[rows, :] = p[:, COL_CMP:COL_CMP + 2 * D_KV].astype(BF16)
        b = p[:, COL_B:COL_B + D_CONV]
        cu = p[:, COL_C:COL_C + D_CONV] * p[:, COL_U:COL_U + D_CONV]
        cbuf[8 + r0:8 + r0 + SUB_ROWS, :] = cu
        y = (cw_ref[0:1, :] * cbuf[6 + r0:6 + r0 + SUB_ROWS, :] + cw_ref[1:2, :] * cbuf[7 + r0:7 + r0 + SUB_ROWS, :]
             + cw_ref[2:3, :] * cu)
        mc_ref[rows, :] = _rms(b * y, gc_ref[...]).astype(BF16)


def _proj_call(x3, g, w, cw, gc, csel, cwin):
    bsz, seq, _ = x3.shape
    tm = ROW_TILE
    tq = Q_TILE
    tps = seq // tm
    hh = N_KV_HEADS
    row = lambda i: (i // tps, i % tps, 0)
    fix = lambda i: (0, 0)
    seq_tile = lambda i: (i % tps, 0)
    tok_minor = lambda i: (i // tps, 0, 0, i % tps)
    tok_major = lambda i: (i // tps, 0, i % tps, 0)
    tok_tiles = lambda i: (i // tps, 0, i % tps, 0, 0)
    return pl.pallas_call(
        functools.partial(_proj_kernel, tiles_per_seq=tps),
        grid=(bsz * tps,),
        in_specs=[
            pl.BlockSpec((None, tm, D_MODEL), row),
            pl.BlockSpec((1, D_MODEL), fix),
            pl.BlockSpec((D_MODEL, PROJ_COLS), fix),
            pl.BlockSpec((3, D_CONV), fix),
            pl.BlockSpec((1, D_CONV), fix),
            pl.BlockSpec((tm, AUG_K), seq_tile),
            pl.BlockSpec((tm, AUG_K), seq_tile),
        ],
        out_specs=[
            pl.BlockSpec((None, hh, GQA_GROUP * HEAD_DIM, tm), tok_minor),
            pl.BlockSpec((None, hh, GATE_ROWS, tm), tok_minor),
            pl.BlockSpec((None, hh, tm, AUG_K), tok_major),
            pl.BlockSpec((None, hh, tm, AUG_K), tok_major),
            pl.BlockSpec((None, hh, tm // tq, V_ROWS, tq), tok_tiles),
            pl.BlockSpec((None, hh, tm // tq, V_ROWS, tq), tok_tiles),
            pl.BlockSpec((None, tm, 2 * D_KV), row),
            pl.BlockSpec((None, tm, D_CONV), row),
        ],
        out_shape=[
            jax.ShapeDtypeStruct((bsz, hh, GQA_GROUP * HEAD_DIM, seq), BF16),
            jax.ShapeDtypeStruct((bsz, hh, GATE_ROWS, seq), F32),
            jax.ShapeDtypeStruct((bsz, hh, seq, AUG_K), BF16),
            jax.ShapeDtypeStruct((bsz, hh, seq, AUG_K), BF16),
            jax.ShapeDtypeStruct((bsz, hh, seq // tq, V_ROWS, tq), BF16),
            jax.ShapeDtypeStruct((bsz, hh, seq // tq, V_ROWS, tq), BF16),
            jax.ShapeDtypeStruct((bsz, seq, 2 * D_KV), BF16),
            jax.ShapeDtypeStruct((bsz, seq, D_CONV), BF16),
        ],
        scratch_shapes=[pltpu.VMEM((tm + 8, D_CONV), F32)],
        compiler_params=pltpu.CompilerParams(dimension_semantics=("arbitrary",), vmem_limit_bytes=VMEM_LIMIT),
    )(x3, g, w, cw, gc, csel, cwin)


def _compress_kernel(x_ref, w1_ref, pos_ref, w2_ref, o_ref, sbuf):
    nc = x_ref.shape[0]
    y = jnp.dot(x_ref[...], w1_ref[...], preferred_element_type=F32)
    pb = jnp.dot(pos_ref[...], w1_ref[...], preferred_element_type=F32)
    posb = pb[0:1, 0:CMP_HIDDEN] + pb[1:2, CMP_HIDDEN:2 * CMP_HIDDEN]
    sbuf[0:nc, :] = y[:, CMP_HIDDEN:2 * CMP_HIDDEN]
    sbuf[nc:nc + 8, :] = jnp.zeros((8, CMP_HIDDEN), F32)
    hid = y[:, 0:CMP_HIDDEN] + sbuf[1:nc + 1, :] + posb
    act = jax.nn.gelu(hid)
    out = jnp.dot(act.astype(BF16), w2_ref[...], preferred_element_type=F32)
    rowi = lax.broadcasted_iota(jnp.int32, out.shape, 0)
    o_ref[...] = jnp.where(rowi < nc - 1, out, 0.0)


def _compress_call(xc, w1ab, pos8, w2):
    b, _, nc, _ = xc.shape
    return pl.pallas_call(
        _compress_kernel,
        grid=(b, 4),
        in_specs=[
            pl.BlockSpec((None, None, nc, 16 * HEAD_DIM), lambda i, s: (i, s, 0, 0)),
            pl.BlockSpec((None, 16 * HEAD_DIM, 2 * CMP_HIDDEN), lambda i, s: (s // 2, 0, 0)),
            pl.BlockSpec((None, 8, 16 * HEAD_DIM), lambda i, s: (s // 2, 0, 0)),
            pl.BlockSpec((None, CMP_HIDDEN, HEAD_DIM), lambda i, s: (s // 2, 0, 0)),
        ],
        out_specs=pl.BlockSpec((None, None, nc, HEAD_DIM), lambda i, s: (i, s, 0, 0)),
        out_shape=jax.ShapeDtypeStruct((b, 4, nc, HEAD_DIM), F32),
        scratch_shapes=[pltpu.VMEM((nc + 8, CMP_HIDDEN), F32)],
        compiler_params=pltpu.CompilerParams(dimension_semantics=("arbitrary", "arbitrary")),
    )(xc, w1ab, pos8, w2)


def _attn_kernel(qt_ref, gt_ref, ks_ref, kw_ref, vs_ref, vw_ref, kc_ref, vc_ref, msel_ref, cmask_ref, o_ref,
                 qaug, acc_ref, accw_ref, sbuf, pbuf, imp_ref, selm_ref, klist, *, topk):
    h = pl.program_id(1)
    qi = pl.program_id(2)
    tq = qt_ref.shape[1]
    r = GQA_GROUP * tq
    nc = kc_ref.shape[0]
    nsel = msel_ref.shape[0]
    nkb = ks_ref.shape[0]
    t0 = qi * tq

    lane16 = lax.broadcasted_iota(jnp.int32, (AUG_ALIBI_ROWS, r), 1)
    sub16 = lax.broadcasted_iota(jnp.int32, (AUG_ALIBI_ROWS, r), 0)
    gl = lane16 // tq
    off = (lane16 % tq).astype(F32)
    base = jnp.where(h == 0, 0.5, 0.03125).astype(F32)
    slope = jnp.where(gl == 0, base, jnp.where(gl == 1, base * 0.5, jnp.where(gl == 2, base * 0.25, base * 0.125)))
    blk0 = (t0 // SEL_BLOCK).astype(F32)
    c3 = jnp.where(sub16 % 3 == 0, LOG2E_3[0], jnp.where(sub16 % 3 == 1, LOG2E_3[1], LOG2E_3[2]))
    arow = jnp.where(sub16 < 3, slope * c3,
                     jnp.where(sub16 < 6, 64.0 * slope * c3,
                               jnp.where(sub16 == 6, -slope * LOG2E * (64.0 * blk0 + off), 0.0)))
    for g in range(GQA_GROUP):
        qaug[0:HEAD_DIM, g * tq:(g + 1) * tq] = qt_ref[g * HEAD_DIM:(g + 1) * HEAD_DIM, :]
    qaug[AUG_ALIBI:AUG_SEL, :] = arow.astype(BF16)
    qaug[AUG_SEL:AUG_K, :] = jnp.zeros((AUG_K - AUG_SEL, r), BF16)

    lane_q = lax.broadcasted_iota(jnp.int32, (1, r), 1) % tq
    key_i = lax.broadcasted_iota(jnp.int32, (tq, r), 0)
    causal = key_i <= lane_q

    q_nosel = qaug[...]
    nwin = WINDOW // tq

    def win_scores(w):
        kb = qi - nwin + w
        s = jnp.dot(kw_ref[jnp.maximum(kb, 0)], q_nosel, preferred_element_type=F32)
        if w == 0:
            return jnp.where((key_i > lane_q) & (kb >= 0), s, NEG)
        if w == nwin:
            return jnp.where(causal, s, NEG)
        return jnp.where(kb >= 0, s, NEG)

    def win_update(w, s, m_old):
        m_new = jnp.maximum(m_old, jnp.max(s, axis=0, keepdims=True))
        p = jnp.exp2(s - m_new).astype(BF16)
        pv_w = jnp.dot(vw_ref[jnp.maximum(qi - nwin + w, 0)], p, preferred_element_type=F32)
        accw_ref[...] = pv_w if w == 0 else accw_ref[...] * jnp.exp2(m_old - m_new) + pv_w
        return m_new

    sc = jnp.dot(kc_ref[...], q_nosel, preferred_element_type=F32)
    win_s = [win_scores(0)]
    mask_c = cmask_ref[...] <= t0
    sc = jnp.where(mask_c, sc, NEG)
    m_c = jnp.max(sc, axis=0, keepdims=True)
    e_c = jnp.exp2(sc - m_c)
    l_c = jnp.sum(e_c, axis=0, keepdims=True)
    p_c = e_c * jnp.where(m_c > 0.5 * NEG, 1.0 / l_c, 0.0)
    o_cmp = jnp.dot(vc_ref[...], p_c.astype(BF16), preferred_element_type=F32)

    p_sum = p_c[:, 0:tq]
    for g in range(1, GQA_GROUP):
        p_sum = p_sum + p_c[:, g * tq:(g + 1) * tq]
    p1 = p_sum.astype(BF16)
    r1 = p_sum - p1.astype(F32)
    p2 = r1.astype(BF16)
    p3 = (r1 - p2.astype(F32)).astype(BF16)
    msel = msel_ref[...]
    imp = (jnp.dot(msel, p1, preferred_element_type=F32) + jnp.dot(msel, p2, preferred_element_type=F32)
           + jnp.dot(msel, p3, preferred_element_type=F32))

    m_w = jnp.full((1, r), NEG, F32)
    for w in range(nwin):
        win_s.append(win_scores(w + 1))
        m_w = win_update(w, win_s[w], m_w)

    jj =lax.broadcasted_iota(jnp.int32, (nsel, tq), 0)
    jt = (t0 + lax.broadcasted_iota(jnp.int32, (nsel, tq), 1)) // SEL_BLOCK
    imp = jnp.where((jj == 0) | (jj == jt) | (jj == jt - 1), BIG, imp)
    imp = jnp.where(jj > jt, NEG, imp)
    imp_ref[...] = imp
    bpt = tq // SEL_BLOCK
    sub8 = lax.broadcasted_iota(jnp.int32, (8, tq), 0)

    def rank_select(nblk):
        groups = [imp_ref[8 * gi:8 * gi + 8, :] for gi in range(nblk // 8)]
        cnts = [jnp.zeros((8, tq), jnp.int32) for _ in groups]
        for jp in range(nblk):
            rowv = jnp.broadcast_to(imp_ref[jp:jp + 1, :], (8, tq))
            for gi, grp in enumerate(groups):
                if 8 * gi > jp:
                    beats = rowv >= grp
                elif 8 * gi + 7 <= jp:
                    beats = rowv > grp
                else:
                    beats = (rowv > grp) | ((rowv == grp) & (sub8 + 8 * gi > jp))
                cnts[gi] = cnts[gi] + jnp.where(beats, 1, 0)
        selm = [jnp.where(c < topk, 1.0, 0.0) for c in cnts] + [jnp.zeros((nsel - nblk, tq), F32)] * (nblk < nsel)
        selm = jnp.concatenate(selm, axis=0)
        selm_ref[...] = selm
        selbias = jnp.where(selm > 0.0, 0.0, NEG).astype(BF16)
        for g in range(GQA_GROUP):
            qaug[AUG_SEL:AUG_SEL + nsel, g * tq:(g + 1) * tq] = selbias

    for idx in range(nsel // 16):
        pl.when((qi * bpt) // 16 == idx)(functools.partial(rank_select, 16 * (idx + 1)))

    q_all = qaug[...]
    sbuf[0] = jnp.dot(ks_ref[0], q_all, preferred_element_type=F32)
    win_update(nwin, win_s[nwin], m_w)
    acc_w = accw_ref[...]
    o_win = acc_w[0:HEAD_DIM, :] * (1.0 / acc_w[HEAD_DIM:HEAD_DIM + 1, :])

    n_use = jnp.int32(0)
    for gi in range(nsel // 8):
        hit = jnp.max(selm_ref[8 * gi:8 * gi + 8, :], axis=1, keepdims=True)
        for part in range(8 // bpt):
            kb = gi * (8 // bpt) + part
            if kb < nkb - 1:
                klist[n_use] = kb
                used = (jnp.max(hit[part * bpt:(part + 1) * bpt, :]) > 0.0) & (kb < qi)
                n_use = n_use + used.astype(jnp.int32)
    klist[n_use] = qi

    def qk(kb):
        return jnp.dot(ks_ref[kb], q_all, preferred_element_type=F32)

    def softmax_tile(s, m_old):
        m_new = jnp.maximum(m_old, jnp.max(s, axis=0, keepdims=True))
        return m_new, jnp.exp2(s - m_new).astype(BF16), jnp.exp2(m_old - m_new)

    def pv(kb, p, alpha):
        acc_ref[...] = acc_ref[...] * alpha + jnp.dot(vs_ref[kb], p, preferred_element_type=F32)

    acc_ref[...] = jnp.zeros(acc_ref.shape, F32)
    pbuf[1] = jnp.zeros((tq, r), BF16)

    def step(j, cur, carry):
        m_old, alpha_prev = carry
        s = sbuf[cur]
        sbuf[1 - cur] = qk(klist[j + 1])
        pv(klist[jnp.maximum(j - 1, 0)], pbuf[1 - cur], alpha_prev)
        m_new, p, alpha = softmax_tile(s, m_old)
        pbuf[cur] = p
        return m_new, alpha

    def finish(cur, carry):
        m_old, alpha_prev = carry
        _, p_last, alpha_last = softmax_tile(jnp.where(causal, sbuf[cur], NEG), m_old)
        pv(klist[jnp.maximum(n_use - 1, 0)], pbuf[1 - cur], alpha_prev)
        pv(qi, p_last, alpha_last)

    def pair(i, carry):
        return step(2 * i + 1, 1, step(2 * i, 0, carry))

    carry0 = (jnp.full((1, r), NEG, F32), jnp.ones((1, r), F32))

    @pl.when(n_use % 2 == 0)
    def _():
        finish(0, lax.fori_loop(0, n_use // 2, pair, carry0))

    @pl.when(n_use % 2 == 1)
    def _():
        finish(1, step(n_use - 1, 0, lax.fori_loop(0, n_use // 2, pair, carry0)))

    acc_s = acc_ref[...]
    o_sel = acc_s[0:HEAD_DIM, :] * (1.0 / acc_s[HEAD_DIM:HEAD_DIM + 1, :])

    gt = gt_ref[...]
    for g in range(GQA_GROUP):
        sl = slice(g * tq, (g + 1) * tq)
        og = (gt[3 * g:3 * g + 1, :] * o_cmp[:, sl] + gt[3 * g + 1:3 * g + 2, :] * o_sel[:, sl]
              + gt[3 * g + 2:3 * g + 3, :] * o_win[:, sl])
        o_ref[g * HEAD_DIM:(g + 1) * HEAD_DIM, :] = og


def _attn_call(qt, gt, ks, kw, vs, vw, kc, vc, msel, cmask, topk):
    b, hh, _, t = qt.shape
    tq = Q_TILE
    nkb = t // tq
    nc = kc.shape[2]
    nsel = msel.shape[0]
    per_q = lambda i, j, k: (i, j, 0, k)
    per_bh4 = lambda i, j, k: (i, j, 0, 0)
    per_bh5 = lambda i, j, k: (i, j, 0, 0, 0)
    return pl.pallas_call(
        functools.partial(_attn_kernel, topk=topk),
        grid=(b, hh, nkb),
        in_specs=[
            pl.BlockSpec((None, None, GQA_GROUP * HEAD_DIM, tq), per_q),
            pl.BlockSpec((None, None, GATE_ROWS, tq), per_q),
            pl.BlockSpec((None, None, nkb, tq, AUG_K), per_bh5),
            pl.BlockSpec((None, None, nkb, tq, AUG_K), per_bh5),
            pl.BlockSpec((None, None, nkb, V_ROWS, tq), per_bh5),
            pl.BlockSpec((None, None, nkb, V_ROWS, tq), per_bh5),
            pl.BlockSpec((None, None, nc, AUG_K), per_bh4),
            pl.BlockSpec((None, None, HEAD_DIM, nc), per_bh4),
            pl.BlockSpec((nsel, nc), lambda i, j, k: (0, 0)),
            pl.BlockSpec((nc, GQA_GROUP * tq), lambda i, j, k: (0, 0)),
        ],
        out_specs=pl.BlockSpec((None, GQA_GROUP * HEAD_DIM, tq), lambda i, j, k: (i, j, k)),
        out_shape=jax.ShapeDtypeStruct((b, D_ATTN, t), F32),
        scratch_shapes=[pltpu.VMEM((AUG_K, GQA_GROUP * tq), BF16), pltpu.VMEM((V_ROWS, GQA_GROUP * tq), F32),
                        pltpu.VMEM((V_ROWS, GQA_GROUP * tq), F32),
                        pltpu.VMEM((2, tq, GQA_GROUP * tq), F32), pltpu.VMEM((2, tq, GQA_GROUP * tq), BF16),
                        pltpu.VMEM((nsel, tq), F32), pltpu.VMEM((nsel, tq), F32),
                        pltpu.SMEM((nkb + 1,), jnp.int32)],
        compiler_params=pltpu.CompilerParams(dimension_semantics=("arbitrary", "arbitrary", "arbitrary"),
                                             vmem_limit_bytes=VMEM_LIMIT),
    )(qt, gt, ks, kw, vs, vw, kc, vc, msel, cmask)


def _ffn_kernel(x_ref, oa_ref, mc_ref, ga_ref, wo_ref, gf_ref, wg_ref, wu_ref, cw_ref, cb_ref, wd_ref, gl_ref,
                o_ref, gbuf, ybuf, *, tiles_per_seq):
    i = pl.program_id(0)
    tm = x_ref.shape[0]

    @pl.when(i % tiles_per_seq == 0)
    def _():
        gbuf[0:8, :] = jnp.zeros((8, D_FF), F32)

    @pl.when(i % tiles_per_seq != 0)
    def _():
        gbuf[0:8, :] = gbuf[tm:tm + 8, :]

    subs = [slice(r0, r0 + SUB_ROWS) for r0 in range(0, tm, SUB_ROWS)]
    x1s, h2s = [], []
    for rows in subs:
        oat = oa_ref[:, rows]
        mat = (oat * lax.rsqrt(jnp.mean(oat * oat, axis=0, keepdims=True) + EPS) * ga_ref[...]).astype(BF16)
        x1 = (x_ref[rows, :] + lax.dot_general(mat, wo_ref[0:D_ATTN, :], (((0,), (0,)), ((), ())),
                                                preferred_element_type=F32)
              + jnp.dot(mc_ref[rows, :], wo_ref[D_ATTN:D_MODEL, :], preferred_element_type=F32))
        x1s.append(x1)
        h2s.append(_rms(x1, gf_ref[...]).astype(BF16))
    for rows, x1, h2 in zip(subs, x1s, h2s):
        r0 = rows.start
        for c in range(D_FF // FF_CHUNK):
            cs = slice(c * FF_CHUNK, (c + 1) * FF_CHUNK)
            gpre = jnp.dot(h2, wg_ref[:, cs], preferred_element_type=F32)
            up = jnp.dot(h2, wu_ref[:, cs], preferred_element_type=F32)
            gbuf[8 + r0:8 + r0 + SUB_ROWS, cs] = gpre
            gate = (cw_ref[0:1, cs] * gbuf[6 + r0:6 + r0 + SUB_ROWS, cs]
                    + cw_ref[1:2, cs] * gbuf[7 + r0:7 + r0 + SUB_ROWS, cs] + cw_ref[2:3, cs] * gpre + cb_ref[:, cs])
            ybuf[rows, cs] = (jax.nn.silu(gate) * up).astype(BF16)
        acc = x1 + jnp.dot(ybuf[rows, :], wd_ref[...], preferred_element_type=F32)
        o_ref[rows, :] = _rms(acc, gl_ref[...])


def _ffn_call(x2, oat, mc, ga, wo, gf, wg, wu, cw, cb, wd, gl):
    n = x2.shape[0]
    seq = oat.shape[2]
    tm = ROW_TILE
    tps = seq // tm
    row = lambda i: (i, 0)
    fix = lambda i: (0, 0)
    once = dict(pipeline_mode=pl.Buffered(1))
    return pl.pallas_call(
        functools.partial(_ffn_kernel, tiles_per_seq=seq // tm),
        grid=(n // tm,),
        in_specs=[
            pl.BlockSpec((tm, D_MODEL), row),
            pl.BlockSpec((None, D_ATTN, tm), lambda i: (i // tps, 0, i % tps)),
            pl.BlockSpec((tm, D_CONV), row),
            pl.BlockSpec((D_ATTN, 1), fix),
            pl.BlockSpec((D_MODEL, D_MODEL), fix, **once),
            pl.BlockSpec((1, D_MODEL), fix),
            pl.BlockSpec((D_MODEL, D_FF), fix, **once),
            pl.BlockSpec((D_MODEL, D_FF), fix, **once),
            pl.BlockSpec((3, D_FF), fix),
            pl.BlockSpec((1, D_FF), fix),
            pl.BlockSpec((D_FF, D_MODEL), fix, **once),
            pl.BlockSpec((1, D_MODEL), fix),
        ],
        out_specs=pl.BlockSpec((tm, D_MODEL), row),
        out_shape=jax.ShapeDtypeStruct((n, D_MODEL), F32),
        scratch_shapes=[pltpu.VMEM((tm + 8, D_FF), F32), pltpu.VMEM((tm, D_FF), BF16)],
        compiler_params=pltpu.CompilerParams(dimension_semantics=("arbitrary",), vmem_limit_bytes=VMEM_LIMIT),
    )(x2, oat, mc, ga, wo, gf, wg, wu, cw, cb, wd, gl)


def _alibi_cols(pos):
    cols = np.zeros((len(pos), AUG_ALIBI_ROWS), np.float32)
    cols[:, 0:3] = (pos % SEL_BLOCK)[:, None]
    cols[:, 3:6] = (pos // SEL_BLOCK)[:, None]
    cols[:, 6] = 1.0
    return cols


def _key_consts(t, with_sel):
    pos = np.arange(t)
    c = np.zeros((t, AUG_K), np.float32)
    c[:, AUG_ALIBI:AUG_SEL] = _alibi_cols(pos)
    if with_sel:
        c[pos, AUG_SEL + pos // SEL_BLOCK] = 1.0
    return jnp.asarray(c, BF16)


def _cmp_consts(nc):
    c = np.zeros((nc, AUG_K - HEAD_DIM), np.float32)
    c[:, 0:AUG_ALIBI_ROWS] = _alibi_cols(np.arange(nc) * CMP_STRIDE + (CMP_BLOCK - 1))
    return jnp.asarray(c, BF16)


def _cmp_limits(nc):
    end = np.arange(nc)[:, None] * CMP_STRIDE + (CMP_BLOCK - 1)
    off = np.arange(GQA_GROUP * Q_TILE)[None, :] % Q_TILE
    return jnp.asarray(end - off, jnp.int32)


def _sel_map_t(t, nc):
    n_cmp = (t - CMP_BLOCK) // CMP_STRIDE + 1
    n_sel = t // SEL_BLOCK
    cs = np.arange(n_cmp)[:, None] * CMP_STRIDE
    ss = np.arange(n_sel)[None, :] * SEL_BLOCK
    ov = np.maximum(0, np.minimum(cs + CMP_BLOCK, ss + SEL_BLOCK) - np.maximum(cs, ss)) / CMP_BLOCK
    m = np.zeros((n_sel, nc), np.float32)
    m[:, :n_cmp] = ov.T
    return jnp.asarray(m, BF16)


def kernel(x, norm_mix_g, w_in, pos_ck, w_ck1, w_ck2, pos_cv, w_cv1, w_cv2, conv_mix_w, norm_out_attn_g,
           norm_out_conv_g, w_out, norm_ffn_g, w_gate, w_up, ffn_conv_w, ffn_conv_b, w_down, norm_final_g):
    b, t, _ = x.shape
    hh, dk = N_KV_HEADS, HEAD_DIM
    assert t % ROW_TILE == 0 and t % Q_TILE == 0 and WINDOW % Q_TILE == 0 and t // SEL_BLOCK <= AUG_K - AUG_SEL
    assert ROW_TILE % SUB_ROWS == 0 and SUB_ROWS % Q_TILE == 0 and (t // SEL_BLOCK) % 16 == 0
    nc = t // CMP_STRIDE
    nsel = t // SEL_BLOCK
    nkb = t // Q_TILE
    depth = w_in.shape[0]
    assert depth == 1
    xx = x
    for l in range(depth):
        wi = w_in[l]
        col = np.cumsum([0, D_ATTN] + [D_KV] * 6 + [3 * N_HEADS_ATTN] + [D_CONV] * 3)
        kv_slabs = [wi[:, col[3 + 2 * br] + hd * dk:col[3 + 2 * br] + (hd + 1) * dk] if part == 0 else
                    wi[:, col[4 + 2 * br] + hd * dk:col[4 + 2 * br] + (hd + 1) * dk]
                    for br in range(2) for hd in range(hh) for part in range(2)]
        gate_cols = [jnp.concatenate([wi[:, col[7] + 12 * hd:col[7] + 12 * (hd + 1)], jnp.zeros((D_MODEL, 4), F32)], axis=1)
                     for hd in range(hh)]
        w_p = jnp.concatenate([wi[:, 0:D_ATTN]] + kv_slabs + [wi[:, col[1]:col[3]], wi[:, col[8]:col[11]]] + gate_cols
                              + [jnp.zeros((D_MODEL, 128 - 2 * GATE_ROWS), F32)], axis=1).astype(BF16)
        assert w_p.shape[1] == PROJ_COLS
        qt, gt, ksa, kwa, vst, vwt, kvc, mixed_conv = _proj_call(
            xx, norm_mix_g[l][None], w_p, conv_mix_w[l], norm_out_conv_g[l][None],
            _key_consts(t, True), _key_consts(t, False))

        xc = kvc.reshape(b, nc, CMP_STRIDE, 4, dk).transpose(0, 3, 1, 2, 4).reshape(b, 4, nc, 16 * dk)
        half = CMP_STRIDE * dk
        w1ab = jnp.stack([jnp.concatenate([w[:half], w[half:]], axis=1) for w in (w_ck1[l], w_cv1[l])]).astype(BF16)
        pos8 = jnp.stack([jnp.concatenate([p.reshape(2, half), jnp.zeros((6, half), F32)], axis=0)
                          for p in (pos_ck[l], pos_cv[l])]).astype(BF16)
        w2 = jnp.stack([w_ck2[l], w_cv2[l]]).astype(BF16)
        cmp = _compress_call(xc, w1ab, pos8, w2)
        kc = jnp.concatenate([cmp[:, 0:2].astype(BF16),
                              jnp.broadcast_to(_cmp_consts(nc), (b, hh, nc, AUG_K - dk))], axis=-1)
        vc = cmp[:, 2:4].astype(BF16).transpose(0, 1, 3, 2)

        o_attn = _attn_call(qt, gt, ksa.reshape(b, hh, nkb, Q_TILE, AUG_K), kwa.reshape(b, hh, nkb, Q_TILE, AUG_K),
                            vst, vwt, kc, vc, _sel_map_t(t, nc), _cmp_limits(nc), min(SEL_TOPK, nsel))

        xx = _ffn_call(xx.reshape(b * t, D_MODEL), o_attn, mixed_conv.reshape(b * t, D_CONV),
                       norm_out_attn_g[l][:, None], w_out[l].astype(BF16), norm_ffn_g[l][None], w_gate[l].astype(BF16),
                       w_up[l].astype(BF16), ffn_conv_w[l], ffn_conv_b[l][None], w_down[l].astype(BF16),
                       norm_final_g[None])
    return xx.reshape(b, t, D_MODEL)
```

```python
import functools

import jax
import jax.numpy as jnp
import numpy as np
from jax import lax
from jax.experimental import pallas as pl
from jax.experimental.pallas import tpu as pltpu

F32 = jnp.float32
BF16 = jnp.bfloat16

D_MODEL = 1024
N_KV_HEADS = 2
GQA_GROUP = 4
N_HEADS_ATTN = N_KV_HEADS * GQA_GROUP
HEAD_DIM = 64
D_ATTN = N_HEADS_ATTN * HEAD_DIM
D_KV = N_KV_HEADS * HEAD_DIM
D_CONV = D_MODEL - D_ATTN
CMP_BLOCK = 32
CMP_STRIDE = 16
CMP_HIDDEN = 2 * HEAD_DIM
SEL_BLOCK = 64
SEL_TOPK = 16
WINDOW = 512
D_FF = 2816
EPS = 1e-6
NEG = -1e30
BIG = 1e30

AUG_K = 256
AUG_ALIBI = HEAD_DIM
AUG_ALIBI_ROWS = 16
AUG_SEL = AUG_ALIBI + AUG_ALIBI_ROWS
V_ROWS = 80

ROW_TILE = 512
SUB_ROWS = 256
Q_TILE = 256
SEL_UNROLL = 4
FF_CHUNK = 256
PROJ_COLS = 2944
VMEM_LIMIT = 56 * 1024 * 1024

LOG2E = 1.4426950408889634


def _bf16_terms(x, n):
    out = []
    for _ in range(n):
        t = float(np.asarray(x, np.float32).astype(jnp.bfloat16).astype(np.float32))
        out.append(t)
        x = x - t
    return tuple(out)


LOG2E_3 = _bf16_terms(LOG2E, 3)


def _rms(x, g):
    return x * lax.rsqrt(jnp.mean(x * x, axis=-1, keepdims=True) + EPS) * g


COL_Q = 0
COL_KV = D_ATTN
COL_CMP = COL_KV + 4 * 2 * HEAD_DIM
COL_B = COL_CMP + 2 * D_KV
COL_C = COL_B + D_CONV
COL_U = COL_C + D_CONV
COL_GATE = COL_U + D_CONV
GATE_ROWS = 16


def _proj_kernel(x_ref, g_ref, w_ref, cw_ref, gc_ref, csel_ref, cwin_ref,
                 qt_ref, gt_ref, ksa_ref, kwa_ref, vst_ref, vwt_ref, kvc_ref, mc_ref, cbuf, *, tiles_per_seq):
    i = pl.program_id(0)
    tm = x_ref.shape[0]
    tq = vst_ref.shape[-1]

    @pl.when(i % tiles_per_seq == 0)
    def _():
        cbuf[0:8, :] = jnp.zeros((8, D_CONV), F32)

    @pl.when(i % tiles_per_seq != 0)
    def _():
        cbuf[0:8, :] = cbuf[tm:tm + 8, :]

    lane = lax.broadcasted_iota(jnp.int32, (SUB_ROWS, 128), 1)
    ones_rows = jnp.where(lax.broadcasted_iota(jnp.int32, (V_ROWS - HEAD_DIM, tq), 0) == 0, 1.0, 0.0).astype(BF16)
    for sub in range(tm // SUB_ROWS):
        r0 = sub * SUB_ROWS
        rows = slice(r0, r0 + SUB_ROWS)
        h = _rms(x_ref[rows, :], g_ref[...])
        p = jnp.dot(h.astype(BF16), w_ref[...], preferred_element_type=F32)

        for hd in range(N_KV_HEADS):
            qs = p[:, COL_Q + hd * 256:COL_Q + (hd + 1) * 256] * (HEAD_DIM ** -0.5 * LOG2E)
            qt_ref[hd, :, rows] = qs.T.astype(BF16)
        gt = jax.nn.sigmoid(p[:, COL_GATE:COL_GATE + 128]).T
        for hd in range(N_KV_HEADS):
            gt_ref[hd, :, rows] = gt[hd * GATE_ROWS:(hd + 1) * GATE_ROWS, :]

        for branch, (ka_ref, vt_ref, c_ref) in enumerate(((ksa_ref, vst_ref, csel_ref), (kwa_ref, vwt_ref, cwin_ref))):
            for hd in range(N_KV_HEADS):
                c0 = COL_KV + (2 * branch + hd) * 128
                slab = p[:, c0:c0 + 128]
                ka_ref[hd, rows, 0:128] = jnp.where(lane < HEAD_DIM, slab.astype(BF16), c_ref[rows, 0:128])
                ka_ref[hd, rows, 128:AUG_K] = c_ref[rows, 128:AUG_K]
                vt = slab.T[HEAD_DIM:128, :].astype(BF16)
                for kt in range(SUB_ROWS // tq):
                    vt_ref[hd, r0 // tq + kt, 0:HEAD_DIM, :] = vt[:, kt * tq:(kt + 1) * tq]
                    vt_ref[hd, r0 // tq + kt, HEAD_DIM:V_ROWS, :] = ones_rows

        kvc_ref[rows, :] = p[:, COL_CMP:COL_CMP + 2 * D_KV].astype(BF16)
        b = p[:, COL_B:COL_B + D_CONV]
        cu = p[:, COL_C:COL_C + D_CONV] * p[:, COL_U:COL_U + D_CONV]
        cbuf[8 + r0:8 + r0 + SUB_ROWS, :] = cu
        y = (cw_ref[0:1, :] * cbuf[6 + r0:6 + r0 + SUB_ROWS, :] + cw_ref[1:2, :] * cbuf[7 + r0:7 + r0 + SUB_ROWS, :]
             + cw_ref[2:3, :] * cu)
        mc_ref[rows, :] = _rms(b * y, gc_ref[...]).astype(BF16)


def _proj_call(x3, g, w, cw, gc, csel, cwin):
    bsz, seq, _ = x3.shape
    tm = ROW_TILE
    tq = Q_TILE
    tps = seq // tm
    hh = N_KV_HEADS
    row = lambda i: (i // tps, i % tps, 0)
    fix = lambda i: (0, 0)
    seq_tile = lambda i: (i % tps, 0)
    tok_minor = lambda i: (i // tps, 0, 0, i % tps)
    tok_major = lambda i: (i // tps, 0, i % tps, 0)
    tok_tiles = lambda i: (i // tps, 0, i % tps, 0, 0)
    return pl.pallas_call(
        functools.partial(_proj_kernel, tiles_per_seq=tps),
        grid=(bsz * tps,),
        in_specs=[
            pl.BlockSpec((None, tm, D_MODEL), row),
            pl.BlockSpec((1, D_MODEL), fix),
            pl.BlockSpec((D_MODEL, PROJ_COLS), fix),
            pl.BlockSpec((3, D_CONV), fix),
            pl.BlockSpec((1, D_CONV), fix),
            pl.BlockSpec((tm, AUG_K), seq_tile),
            pl.BlockSpec((tm, AUG_K), seq_tile),
        ],
        out_specs=[
            pl.BlockSpec((None, hh, GQA_GROUP * HEAD_DIM, tm), tok_minor),
            pl.BlockSpec((None, hh, GATE_ROWS, tm), tok_minor),
            pl.BlockSpec((None, hh, tm, AUG_K), tok_major),
            pl.BlockSpec((None, hh, tm, AUG_K), tok_major),
            pl.BlockSpec((None, hh, tm // tq, V_ROWS, tq), tok_tiles),
            pl.BlockSpec((None, hh, tm // tq, V_ROWS, tq), tok_tiles),
            pl.BlockSpec((None, tm, 2 * D_KV), row),
            pl.BlockSpec((None, tm, D_CONV), row),
        ],
        out_shape=[
            jax.ShapeDtypeStruct((bsz, hh, GQA_GROUP * HEAD_DIM, seq), BF16),
            jax.ShapeDtypeStruct((bsz, hh, GATE_ROWS, seq), F32),
            jax.ShapeDtypeStruct((bsz, hh, seq, AUG_K), BF16),
            jax.ShapeDtypeStruct((bsz, hh, seq, AUG_K), BF16),
            jax.ShapeDtypeStruct((bsz, hh, seq // tq, V_ROWS, tq), BF16),
            jax.ShapeDtypeStruct((bsz, hh, seq // tq, V_ROWS, tq), BF16),
            jax.ShapeDtypeStruct((bsz, seq, 2 * D_KV), BF16),
            jax.ShapeDtypeStruct((bsz, seq, D_CONV), BF16),
        ],
        scratch_shapes=[pltpu.VMEM((tm + 8, D_CONV), F32)],
        compiler_params=pltpu.CompilerParams(dimension_semantics=("arbitrary",), vmem_limit_bytes=VMEM_LIMIT),
    )(x3, g, w, cw, gc, csel, cwin)


def _compress_kernel(x_ref, w1_ref, pos_ref, w2_ref, o_ref, sbuf):
    nc = x_ref.shape[0]
    y = jnp.dot(x_ref[...], w1_ref[...], preferred_element_type=F32)
    pb = jnp.dot(pos_ref[...], w1_ref[...], preferred_element_type=F32)
    posb = pb[0:1, 0:CMP_HIDDEN] + pb[1:2, CMP_HIDDEN:2 * CMP_HIDDEN]
    sbuf[0:nc, :] = y[:, CMP_HIDDEN:2 * CMP_HIDDEN]
    sbuf[nc:nc + 8, :] = jnp.zeros((8, CMP_HIDDEN), F32)
    hid = y[:, 0:CMP_HIDDEN] + sbuf[1:nc + 1, :] + posb
    act = jax.nn.gelu(hid)
    out = jnp.dot(act.astype(BF16), w2_ref[...], preferred_element_type=F32)
    rowi = lax.broadcasted_iota(jnp.int32, out.shape, 0)
    o_ref[...] = jnp.where(rowi < nc - 1, out, 0.0)


def _compress_call(xc, w1ab, pos8, w2):
    b, _, nc, _ = xc.shape
    return pl.pallas_call(
        _compress_kernel,
        grid=(b, 4),
        in_specs=[
            pl.BlockSpec((None, None, nc, 16 * HEAD_DIM), lambda i, s: (i, s, 0, 0)),
            pl.BlockSpec((None, 16 * HEAD_DIM, 2 * CMP_HIDDEN), lambda i, s: (s // 2, 0, 0)),
            pl.BlockSpec((None, 8, 16 * HEAD_DIM), lambda i, s: (s // 2, 0, 0)),
            pl.BlockSpec((None, CMP_HIDDEN, HEAD_DIM), lambda i, s: (s // 2, 0, 0)),
        ],
        out_specs=pl.BlockSpec((None, None, nc, HEAD_DIM), lambda i, s: (i, s, 0, 0)),
        out_shape=jax.ShapeDtypeStruct((b, 4, nc, HEAD_DIM), F32),
        scratch_shapes=[pltpu.VMEM((nc + 8, CMP_HIDDEN), F32)],
        compiler_params=pltpu.CompilerParams(dimension_semantics=("arbitrary", "arbitrary")),
    )(xc, w1ab, pos8, w2)


def _attn_kernel(qt_ref, gt_ref, ks_ref, kw_ref, vs_ref, vw_ref, kc_ref, vc_ref, msel_ref, cmask_ref, o_ref,
                 qaug, acc_ref, accw_ref, sbuf, pbuf, imp_ref, selm_ref, klist, *, topk):
    h = pl.program_id(1)
    qi = pl.program_id(2)
    tq = qt_ref.shape[1]
    r = GQA_GROUP * tq
    nc = kc_ref.shape[0]
    nsel = msel_ref.shape[0]
    nkb = ks_ref.shape[0]
    t0 = qi * tq

    lane16 = lax.broadcasted_iota(jnp.int32, (AUG_ALIBI_ROWS, r), 1)
    sub16 = lax.broadcasted_iota(jnp.int32, (AUG_ALIBI_ROWS, r), 0)
    gl = lane16 // tq
    off = (lane16 % tq).astype(F32)
    base = jnp.where(h == 0, 0.5, 0.03125).astype(F32)
    slope = jnp.where(gl == 0, base, jnp.where(gl == 1, base * 0.5, jnp.where(gl == 2, base * 0.25, base * 0.125)))
    blk0 = (t0 // SEL_BLOCK).astype(F32)
    c3 = jnp.where(sub16 % 3 == 0, LOG2E_3[0], jnp.where(sub16 % 3 == 1, LOG2E_3[1], LOG2E_3[2]))
    arow = jnp.where(sub16 < 3, slope * c3,
                     jnp.where(sub16 < 6, 64.0 * slope * c3,
                               jnp.where(sub16 == 6, -slope * LOG2E * (64.0 * blk0 + off), 0.0)))
    for g in range(GQA_GROUP):
        qaug[0:HEAD_DIM, g * tq:(g + 1) * tq] = qt_ref[g * HEAD_DIM:(g + 1) * HEAD_DIM, :]
    qaug[AUG_ALIBI:AUG_SEL, :] = arow.astype(BF16)
    qaug[AUG_SEL:AUG_K, :] = jnp.zeros((AUG_K - AUG_SEL, r), BF16)

    lane_q = lax.broadcasted_iota(jnp.int32, (1, r), 1) % tq
    key_i = lax.broadcasted_iota(jnp.int32, (tq, r), 0)
    causal = key_i <= lane_q

    q_nosel = qaug[...]
    nwin = WINDOW // tq

    def win_scores(w):
        kb = qi - nwin + w
        s = jnp.dot(kw_ref[jnp.maximum(kb, 0)], q_nosel, preferred_element_type=F32)
        if w == 0:
            return jnp.where((key_i > lane_q) & (kb >= 0), s, NEG)
        if w == nwin:
            return jnp.where(causal, s, NEG)
        return jnp.where(kb >= 0, s, NEG)

    def win_update(w, s, m_old):
        m_new = jnp.maximum(m_old, jnp.max(s, axis=0, keepdims=True))
        p = jnp.exp2(s - m_new).astype(BF16)
        pv_w = jnp.dot(vw_ref[jnp.maximum(qi - nwin + w, 0)], p, preferred_element_type=F32)
        accw_ref[...] = pv_w if w == 0 else accw_ref[...] * jnp.exp2(m_old - m_new) + pv_w
        return m_new

    sc = jnp.dot(kc_ref[...], q_nosel, preferred_element_type=F32)
    win_s = [win_scores(0)]
    mask_c = cmask_ref[...] <= t0
    sc = jnp.where(mask_c, sc, NEG)
    m_c = jnp.max(sc, axis=0, keepdims=True)
    e_c = jnp.exp2(sc - m_c)
    l_c = jnp.sum(e_c, axis=0, keepdims=True)
    p_c = e_c * jnp.where(m_c > 0.5 * NEG, 1.0 / l_c, 0.0)
    o_cmp = jnp.dot(vc_ref[...], p_c.astype(BF16), preferred_element_type=F32)

    p_sum = p_c[:, 0:tq]
    for g in range(1, GQA_GROUP):
        p_sum = p_sum + p_c[:, g * tq:(g + 1) * tq]
    p1 = p_sum.astype(BF16)
    r1 = p_sum - p1.astype(F32)
    p2 = r1.astype(BF16)
    p3 = (r1 - p2.astype(F32)).astype(BF16)
    msel = msel_ref[...]
    imp = (jnp.dot(msel, p1, preferred_element_type=F32) + jnp.dot(msel, p2, preferred_element_type=F32)
           + jnp.dot(msel, p3, preferred_element_type=F32))

    m_w = jnp.full((1, r), NEG, F32)
    for w in range(nwin):
        win_s.append(win_scores(w + 1))
        m_w = win_update(w, win_s[w], m_w)

    jj =lax.broadcasted_iota(jnp.int32, (nsel, tq), 0)
    jt = (t0 + lax.broadcasted_iota(jnp.int32, (nsel, tq), 1)) // SEL_BLOCK
    imp = jnp.where((jj == 0) | (jj == jt) | (jj == jt - 1), BIG, imp)
    imp = jnp.where(jj > jt, NEG, imp)
    imp_ref[...] = imp
    bpt = tq // SEL_BLOCK
    sub8 = lax.broadcasted_iota(jnp.int32, (8, tq), 0)

    def rank_select(nblk):
        groups = [imp_ref[8 * gi:8 * gi + 8, :] for gi in range(nblk // 8)]
        cnts = [jnp.zeros((8, tq), jnp.int32) for _ in groups]
        for jp in range(nblk):
            rowv = jnp.broadcast_to(imp_ref[jp:jp + 1, :], (8, tq))
            for gi, grp in enumerate(groups):
                if 8 * gi > jp:
                    beats = rowv >= grp
                elif 8 * gi + 7 <= jp:
                    beats = rowv > grp
                else:
                    beats = (rowv > grp) | ((rowv == grp) & (sub8 + 8 * gi > jp))
                cnts[gi] = cnts[gi] + jnp.where(beats, 1, 0)
        selm = [jnp.where(c < topk, 1.0, 0.0) for c in cnts] + [jnp.zeros((nsel - nblk, tq), F32)] * (nblk < nsel)
        selm = jnp.concatenate(selm, axis=0)
        selm_ref[...] = selm
        selbias = jnp.where(selm > 0.0, 0.0, NEG).astype(BF16)
        for g in range(GQA_GROUP):
            qaug[AUG_SEL:AUG_SEL + nsel, g * tq:(g + 1) * tq] = selbias

    for idx in range(nsel // 16):
        pl.when((qi * bpt) // 16 == idx)(functools.partial(rank_select, 16 * (idx + 1)))

    q_all = qaug[...]
    sbuf[0] = jnp.dot(ks_ref[0], q_all, preferred_element_type=F32)
    win_update(nwin, win_s[nwin], m_w)
    acc_w = accw_ref[...]
    o_win = acc_w[0:HEAD_DIM, :] * (1.0 / acc_w[HEAD_DIM:HEAD_DIM + 1, :])

    n_use = jnp.int32(0)
    for gi in range(nsel // 8):
        hit = jnp.max(selm_ref[8 * gi:8 * gi + 8, :], axis=1, keepdims=True)
        for part in range(8 // bpt):
            kb = gi * (8 // bpt) + part
            if kb < nkb - 1:
                klist[n_use] = kb
                used = (jnp.max(hit[part * bpt:(part + 1) * bpt, :]) > 0.0) & (kb < qi)
                n_use = n_use + used.astype(jnp.int32)
    klist[n_use] = qi

    def qk(kb):
        return jnp.dot(ks_ref[kb], q_all, preferred_element_type=F32)

    def softmax_tile(s, m_old):
        m_new = jnp.maximum(m_old, jnp.max(s, axis=0, keepdims=True))
        return m_new, jnp.exp2(s - m_new).astype(BF16), jnp.exp2(m_old - m_new)

    def pv(kb, p, alpha):
        acc_ref[...] = acc_ref[...] * alpha + jnp.dot(vs_ref[kb], p, preferred_element_type=F32)

    acc_ref[...] = jnp.zeros(acc_ref.shape, F32)
    pbuf[1] = jnp.zeros((tq, r), BF16)

    def step(j, cur, carry):
        m_old, alpha_prev = carry
        k_next = ks_ref[klist[j + 1]]
        v_prev = vs_ref[klist[jnp.maximum(j - 1, 0)]]
        m_news, alphas = [], []
        for g in range(GQA_GROUP):
            cols = slice(g * tq, (g + 1) * tq)
            sbuf[1 - cur, :, cols] = jnp.dot(k_next, qaug[:, cols], preferred_element_type=F32)
            acc_ref[:, cols] = (acc_ref[:, cols] * alpha_prev[:, cols]
                                + jnp.dot(v_prev, pbuf[1 - cur, :, cols], preferred_element_type=F32))
            m_new, p, alpha = softmax_tile(sbuf[cur, :, cols], m_old[:, cols])
            pbuf[cur, :, cols] = p
            m_news.append(m_new)
            alphas.append(alpha)
        return jnp.concatenate(m_news, axis=1), jnp.concatenate(alphas, axis=1)

    def finish(cur, carry):
        m_old, alpha_prev = carry
        _, p_last, alpha_last = softmax_tile(jnp.where(causal, sbuf[cur], NEG), m_old)
        pv(klist[jnp.maximum(n_use - 1, 0)], pbuf[1 - cur], alpha_prev)
        pv(qi, p_last, alpha_last)

    def unrolled(i, carry):
        for u in range(SEL_UNROLL):
            carry = step(SEL_UNROLL * i + u, u % 2, carry)
        return carry

    carry0 = (jnp.full((1, r), NEG, F32), jnp.ones((1, r), F32))
    n_main = n_use // SEL_UNROLL
    carry_main = lax.fori_loop(0, n_main, unrolled, carry0)
    for rem in range(SEL_UNROLL):
        @pl.when(n_use % SEL_UNROLL == rem)
        def _(rem=rem):
            carry = carry_main
            for u in range(rem):
                carry = step(SEL_UNROLL * n_main + u, u % 2, carry)
            finish(rem % 2, carry)

    acc_s = acc_ref[...]
    o_sel = acc_s[0:HEAD_DIM, :] * (1.0 / acc_s[HEAD_DIM:HEAD_DIM + 1, :])

    gt = gt_ref[...]
    for g in range(GQA_GROUP):
        sl = slice(g * tq, (g + 1) * tq)
        og = (gt[3 * g:3 * g + 1, :] * o_cmp[:, sl] + gt[3 * g + 1:3 * g + 2, :] * o_sel[:, sl]
              + gt[3 * g + 2:3 * g + 3, :] * o_win[:, sl])
        o_ref[g * HEAD_DIM:(g + 1) * HEAD_DIM, :] = og


def _attn_call(qt, gt, ks, kw, vs, vw, kc, vc, msel, cmask, topk):
    b, hh, _, t = qt.shape
    tq = Q_TILE
    nkb = t // tq
    nc = kc.shape[2]
    nsel = msel.shape[0]
    per_q = lambda i, j, k: (i, j, 0, k)
    per_bh4 = lambda i, j, k: (i, j, 0, 0)
    per_bh5 = lambda i, j, k: (i, j, 0, 0, 0)
    return pl.pallas_call(
        functools.partial(_attn_kernel, topk=topk),
        grid=(b, hh, nkb),
        in_specs=[
            pl.BlockSpec((None, None, GQA_GROUP * HEAD_DIM, tq), per_q),
            pl.BlockSpec((None, None, GATE_ROWS, tq), per_q),
            pl.BlockSpec((None, None, nkb, tq, AUG_K), per_bh5),
            pl.BlockSpec((None, None, nkb, tq, AUG_K), per_bh5),
            pl.BlockSpec((None, None, nkb, V_ROWS, tq), per_bh5),
            pl.BlockSpec((None, None, nkb, V_ROWS, tq), per_bh5),
            pl.BlockSpec((None, None, nc, AUG_K), per_bh4),
            pl.BlockSpec((None, None, HEAD_DIM, nc), per_bh4),
            pl.BlockSpec((nsel, nc), lambda i, j, k: (0, 0)),
            pl.BlockSpec((nc, GQA_GROUP * tq), lambda i, j, k: (0, 0)),
        ],
        out_specs=pl.BlockSpec((None, GQA_GROUP * HEAD_DIM, tq), lambda i, j, k: (i, j, k)),
        out_shape=jax.ShapeDtypeStruct((b, D_ATTN, t), F32),
        scratch_shapes=[pltpu.VMEM((AUG_K, GQA_GROUP * tq), BF16), pltpu.VMEM((V_ROWS, GQA_GROUP * tq), F32),
                        pltpu.VMEM((V_ROWS, GQA_GROUP * tq), F32),
                        pltpu.VMEM((2, tq, GQA_GROUP * tq), F32), pltpu.VMEM((2, tq, GQA_GROUP * tq), BF16),
                        pltpu.VMEM((nsel, tq), F32), pltpu.VMEM((nsel, tq), F32),
                        pltpu.SMEM((nkb + 1,), jnp.int32)],
        compiler_params=pltpu.CompilerParams(dimension_semantics=("arbitrary", "arbitrary", "arbitrary"),
                                             vmem_limit_bytes=VMEM_LIMIT),
    )(qt, gt, ks, kw, vs, vw, kc, vc, msel, cmask)


def _ffn_kernel(x_ref, oa_ref, mc_ref, ga_ref, wo_ref, gf_ref, wg_ref, wu_ref, cw_ref, cb_ref, wd_ref, gl_ref,
                o_ref, gbuf, ybuf, *, tiles_per_seq):
    i = pl.program_id(0)
    tm = x_ref.shape[0]

    @pl.when(i % tiles_per_seq == 0)
    def _():
        gbuf[0:8, :] = jnp.zeros((8, D_FF), F32)

    @pl.when(i % tiles_per_seq != 0)
    def _():
        gbuf[0:8, :] = gbuf[tm:tm + 8, :]

    subs = [slice(r0, r0 + SUB_ROWS) for r0 in range(0, tm, SUB_ROWS)]
    x1s, h2s = [], []
    for rows in subs:
        oat = oa_ref[:, rows]
        mat = (oat * lax.rsqrt(jnp.mean(oat * oat, axis=0, keepdims=True) + EPS) * ga_ref[...]).astype(BF16)
        x1 = (x_ref[rows, :] + lax.dot_general(mat, wo_ref[0:D_ATTN, :], (((0,), (0,)), ((), ())),
                                                preferred_element_type=F32)
              + jnp.dot(mc_ref[rows, :], wo_ref[D_ATTN:D_MODEL, :], preferred_element_type=F32))
        x1s.append(x1)
        h2s.append(_rms(x1, gf_ref[...]).astype(BF16))
    for rows, x1, h2 in zip(subs, x1s, h2s):
        r0 = rows.start
        for c in range(D_FF // FF_CHUNK):
            cs = slice(c * FF_CHUNK, (c + 1) * FF_CHUNK)
            gpre = jnp.dot(h2, wg_ref[:, cs], preferred_element_type=F32)
            up = jnp.dot(h2, wu_ref[:, cs], preferred_element_type=F32)
            gbuf[8 + r0:8 + r0 + SUB_ROWS, cs] = gpre
            gate = (cw_ref[0:1, cs] * gbuf[6 + r0:6 + r0 + SUB_ROWS, cs]
                    + cw_ref[1:2, cs] * gbuf[7 + r0:7 + r0 + SUB_ROWS, cs] + cw_ref[2:3, cs] * gpre + cb_ref[:, cs])
            ybuf[rows, cs] = (jax.nn.silu(gate) * up).astype(BF16)
        acc = x1 + jnp.dot(ybuf[rows, :], wd_ref[...], preferred_element_type=F32)
        o_ref[rows, :] = _rms(acc, gl_ref[...])


def _ffn_call(x2, oat, mc, ga, wo, gf, wg, wu, cw, cb, wd, gl):
    n = x2.shape[0]
    seq = oat.shape[2]
    tm = ROW_TILE
    tps = seq // tm
    row = lambda i: (i, 0)
    fix = lambda i: (0, 0)
    once = dict(pipeline_mode=pl.Buffered(1))
    return pl.pallas_call(
        functools.partial(_ffn_kernel, tiles_per_seq=seq // tm),
        grid=(n // tm,),
        in_specs=[
            pl.BlockSpec((tm, D_MODEL), row),
            pl.BlockSpec((None, D_ATTN, tm), lambda i: (i // tps, 0, i % tps)),
            pl.BlockSpec((tm, D_CONV), row),
            pl.BlockSpec((D_ATTN, 1), fix),
            pl.BlockSpec((D_MODEL, D_MODEL), fix, **once),
            pl.BlockSpec((1, D_MODEL), fix),
            pl.BlockSpec((D_MODEL, D_FF), fix, **once),
            pl.BlockSpec((D_MODEL, D_FF), fix, **once),
            pl.BlockSpec((3, D_FF), fix),
            pl.BlockSpec((1, D_FF), fix),
            pl.BlockSpec((D_FF, D_MODEL), fix, **once),
            pl.BlockSpec((1, D_MODEL), fix),
        ],
        out_specs=pl.BlockSpec((tm, D_MODEL), row),
        out_shape=jax.ShapeDtypeStruct((n, D_MODEL), F32),
        scratch_shapes=[pltpu.VMEM((tm + 8, D_FF), F32), pltpu.VMEM((tm, D_FF), BF16)],
        compiler_params=pltpu.CompilerParams(dimension_semantics=("arbitrary",), vmem_limit_bytes=VMEM_LIMIT),
    )(x2, oat, mc, ga, wo, gf, wg, wu, cw, cb, wd, gl)


def _alibi_cols(pos):
    cols = np.zeros((len(pos), AUG_ALIBI_ROWS), np.float32)
    cols[:, 0:3] = (pos % SEL_BLOCK)[:, None]
    cols[:, 3:6] = (pos // SEL_BLOCK)[:, None]
    cols[:, 6] = 1.0
    return cols


def _key_consts(t, with_sel):
    pos = np.arange(t)
    c = np.zeros((t, AUG_K), np.float32)
    c[:, AUG_ALIBI:AUG_SEL] = _alibi_cols(pos)
    if with_sel:
        c[pos, AUG_SEL + pos // SEL_BLOCK] = 1.0
    return jnp.asarray(c, BF16)


def _cmp_consts(nc):
    c = np.zeros((nc, AUG_K - HEAD_DIM), np.float32)
    c[:, 0:AUG_ALIBI_ROWS] = _alibi_cols(np.arange(nc) * CMP_STRIDE + (CMP_BLOCK - 1))
    return jnp.asarray(c, BF16)


def _cmp_limits(nc):
    end = np.arange(nc)[:, None] * CMP_STRIDE + (CMP_BLOCK - 1)
    off = np.arange(GQA_GROUP * Q_TILE)[None, :] % Q_TILE
    return jnp.asarray(end - off, jnp.int32)


def _sel_map_t(t, nc):
    n_cmp = (t - CMP_BLOCK) // CMP_STRIDE + 1
    n_sel = t // SEL_BLOCK
    cs = np.arange(n_cmp)[:, None] * CMP_STRIDE
    ss = np.arange(n_sel)[None, :] * SEL_BLOCK
    ov = np.maximum(0, np.minimum(cs + CMP_BLOCK, ss + SEL_BLOCK) - np.maximum(cs, ss)) / CMP_BLOCK
    m = np.zeros((n_sel, nc), np.float32)
    m[:, :n_cmp] = ov.T
    return jnp.asarray(m, BF16)


def kernel(x, norm_mix_g, w_in, pos_ck, w_ck1, w_ck2, pos_cv, w_cv1, w_cv2, conv_mix_w, norm_out_attn_g,
           norm_out_conv_g, w_out, norm_ffn_g, w_gate, w_up, ffn_conv_w, ffn_conv_b, w_down, norm_final_g):
    b, t, _ = x.shape
    hh, dk = N_KV_HEADS, HEAD_DIM
    assert t % ROW_TILE == 0 and t % Q_TILE == 0 and WINDOW % Q_TILE == 0 and t // SEL_BLOCK <= AUG_K - AUG_SEL
    assert ROW_TILE % SUB_ROWS == 0 and SUB_ROWS % Q_TILE == 0 and (t // SEL_BLOCK) % 16 == 0
    nc = t // CMP_STRIDE
    nsel = t // SEL_BLOCK
    nkb = t // Q_TILE
    depth = w_in.shape[0]
    assert depth == 1
    xx = x
    for l in range(depth):
        wi = w_in[l]
        col = np.cumsum([0, D_ATTN] + [D_KV] * 6 + [3 * N_HEADS_ATTN] + [D_CONV] * 3)
        kv_slabs = [wi[:, col[3 + 2 * br] + hd * dk:col[3 + 2 * br] + (hd + 1) * dk] if part == 0 else
                    wi[:, col[4 + 2 * br] + hd * dk:col[4 + 2 * br] + (hd + 1) * dk]
                    for br in range(2) for hd in range(hh) for part in range(2)]
        gate_cols = [jnp.concatenate([wi[:, col[7] + 12 * hd:col[7] + 12 * (hd + 1)], jnp.zeros((D_MODEL, 4), F32)], axis=1)
                     for hd in range(hh)]
        w_p = jnp.concatenate([wi[:, 0:D_ATTN]] + kv_slabs + [wi[:, col[1]:col[3]], wi[:, col[8]:col[11]]] + gate_cols
                              + [jnp.zeros((D_MODEL, 128 - 2 * GATE_ROWS), F32)], axis=1).astype(BF16)
        assert w_p.shape[1] == PROJ_COLS
        qt, gt, ksa, kwa, vst, vwt, kvc, mixed_conv = _proj_call(
            xx, norm_mix_g[l][None], w_p, conv_mix_w[l], norm_out_conv_g[l][None],
            _key_consts(t, True), _key_consts(t, False))

        xc = kvc.reshape(b, nc, CMP_STRIDE, 4, dk).transpose(0, 3, 1, 2, 4).reshape(b, 4, nc, 16 * dk)
        half = CMP_STRIDE * dk
        w1ab = jnp.stack([jnp.concatenate([w[:half], w[half:]], axis=1) for w in (w_ck1[l], w_cv1[l])]).astype(BF16)
        pos8 = jnp.stack([jnp.concatenate([p.reshape(2, half), jnp.zeros((6, half), F32)], axis=0)
                          for p in (pos_ck[l], pos_cv[l])]).astype(BF16)
        w2 = jnp.stack([w_ck2[l], w_cv2[l]]).astype(BF16)
        cmp = _compress_call(xc, w1ab, pos8, w2)
        kc = jnp.concatenate([cmp[:, 0:2].astype(BF16),
                              jnp.broadcast_to(_cmp_consts(nc), (b, hh, nc, AUG_K - dk))], axis=-1)
        vc = cmp[:, 2:4].astype(BF16).transpose(0, 1, 3, 2)

        o_attn = _attn_call(qt, gt, ksa.reshape(b, hh, nkb, Q_TILE, AUG_K), kwa.reshape(b, hh, nkb, Q_TILE, AUG_K),
                            vst, vwt, kc, vc, _sel_map_t(t, nc), _cmp_limits(nc), min(SEL_TOPK, nsel))

        xx = _ffn_call(xx.reshape(b * t, D_MODEL), o_attn, mixed_conv.reshape(b * t, D_CONV),
                       norm_out_attn_g[l][:, None], w_out[l].astype(BF16), norm_ffn_g[l][None], w_gate[l].astype(BF16),
                       w_up[l].astype(BF16), ffn_conv_w[l], ffn_conv_b[l][None], w_down[l].astype(BF16),
                       norm_final_g[None])
    return xx.reshape(b, t, D_MODEL)
```

```python
import functools

import jax
import jax.numpy as jnp
import numpy as np
from jax import lax
from jax.experimental import pallas as pl
from jax.experimental.pallas import tpu as pltpu

F32 = jnp.float32
BF16 = jnp.bfloat16

D_MODEL = 1024
N_KV_HEADS = 2
GQA_GROUP = 4
N_HEADS_ATTN = N_KV_HEADS * GQA_GROUP
HEAD_DIM = 64
D_ATTN = N_HEADS_ATTN * HEAD_DIM
D_KV = N_KV_HEADS * HEAD_DIM
D_CONV = D_MODEL - D_ATTN
CMP_BLOCK = 32
CMP_STRIDE = 16
CMP_HIDDEN = 2 * HEAD_DIM
SEL_BLOCK = 64
SEL_TOPK = 16
WINDOW = 512
D_FF = 2816
EPS = 1e-6
NEG = -1e30
BIG = 1e30

AUG_K = 256
AUG_ALIBI = HEAD_DIM
AUG_ALIBI_ROWS = 16
AUG_SEL = AUG_ALIBI + AUG_ALIBI_ROWS
V_ROWS = 80

ROW_TILE = 512
SUB_ROWS = 256
Q_TILE = 256
SEL_UNROLL = 4
FF_CHUNK = 256
PROJ_COLS = 2944
VMEM_LIMIT = 56 * 1024 * 1024

LOG2E = 1.4426950408889634


def _bf16_terms(x, n):
    out = []
    for _ in range(n):
        t = float(np.asarray(x, np.float32).astype(jnp.bfloat16).astype(np.float32))
        out.append(t)
        x = x - t
    return tuple(out)


LOG2E_3 = _bf16_terms(LOG2E, 3)


def _rms(x, g):
    return x * lax.rsqrt(jnp.mean(x * x, axis=-1, keepdims=True) + EPS) * g


COL_Q = 0
COL_KV = D_ATTN
COL_CMP = COL_KV + 4 * 2 * HEAD_DIM
COL_B = COL_CMP + 2 * D_KV
COL_C = COL_B + D_CONV
COL_U = COL_C + D_CONV
COL_GATE = COL_U + D_CONV
GATE_ROWS = 16


def _proj_kernel(x_ref, g_ref, w_ref, cw_ref, gc_ref, csel_ref, cwin_ref,
                 qt_ref, gt_ref, ksa_ref, kwa_ref, vst_ref, vwt_ref, kvc_ref, mc_ref, cbuf, cst, *, tiles_per_seq):
    i = pl.program_id(0)
    tm = x_ref.shape[0]
    tq = vst_ref.shape[-1]

    @pl.when(i % tiles_per_seq == 0)
    def _():
        cbuf[0:8, :] = jnp.zeros((8, D_CONV), F32)

    @pl.when(i % tiles_per_seq != 0)
    def _():
        cbuf[0:8, :] = cbuf[tm:tm + 8, :]

    lane = lax.broadcasted_iota(jnp.int32, (SUB_ROWS, 128), 1)
    ones_rows = jnp.where(lax.broadcasted_iota(jnp.int32, (V_ROWS - HEAD_DIM, tq), 0) == 0, 1.0, 0.0).astype(BF16)
    for sub in range(tm // SUB_ROWS):
        r0 = sub * SUB_ROWS
        rows = slice(r0, r0 + SUB_ROWS)
        h = _rms(x_ref[rows, :], g_ref[...])
        p = jnp.dot(h.astype(BF16), w_ref[...], preferred_element_type=F32)

        for hd in range(N_KV_HEADS):
            qs = p[:, COL_Q + hd * 256:COL_Q + (hd + 1) * 256] * (HEAD_DIM ** -0.5 * LOG2E)
            qt_ref[hd, :, rows] = qs.T.astype(BF16)
        gt = jax.nn.sigmoid(p[:, COL_GATE:COL_GATE + 128]).T
        for hd in range(N_KV_HEADS):
            gt_ref[hd, :, rows] = gt[hd * GATE_ROWS:(hd + 1) * GATE_ROWS, :]

        for branch, (ka_ref, vt_ref, c_ref) in enumerate(((ksa_ref, vst_ref, csel_ref), (kwa_ref, vwt_ref, cwin_ref))):
            for hd in range(N_KV_HEADS):
                c0 = COL_KV + (2 * branch + hd) * 128
                slab = p[:, c0:c0 + 128]
                ka_ref[hd, rows, 0:128] = jnp.where(lane < HEAD_DIM, slab.astype(BF16), c_ref[rows, 0:128])
                ka_ref[hd, rows, 128:AUG_K] = c_ref[rows, 128:AUG_K]
                vt = slab.T[HEAD_DIM:128, :].astype(BF16)
                for kt in range(SUB_ROWS // tq):
                    vt_ref[hd, r0 // tq + kt, 0:HEAD_DIM, :] = vt[:, kt * tq:(kt + 1) * tq]
                    vt_ref[hd, r0 // tq + kt, HEAD_DIM:V_ROWS, :] = ones_rows

        nch = SUB_ROWS // CMP_STRIDE
        for part in range(2):
            cst[part, rows, :] = p[:, COL_CMP + part * D_KV:COL_CMP + (part + 1) * D_KV]
            for tok in range(CMP_STRIDE):
                c0 = (2 * tok + part) * D_KV
                kvc_ref[r0 // CMP_STRIDE:r0 // CMP_STRIDE + nch, c0:c0 + D_KV] = (
                    cst[part, pl.ds(r0 + tok, nch, stride=CMP_STRIDE), :].astype(BF16))
        b = p[:, COL_B:COL_B + D_CONV]
        cu = p[:, COL_C:COL_C + D_CONV] * p[:, COL_U:COL_U + D_CONV]
        cbuf[8 + r0:8 + r0 + SUB_ROWS, :] = cu
        y = (cw_ref[0:1, :] * cbuf[6 + r0:6 + r0 + SUB_ROWS, :] + cw_ref[1:2, :] * cbuf[7 + r0:7 + r0 + SUB_ROWS, :]
             + cw_ref[2:3, :] * cu)
        mc_ref[rows, :] = _rms(b * y, gc_ref[...]).astype(BF16)


def _proj_call(x3, g, w, cw, gc, csel, cwin):
    bsz, seq, _ = x3.shape
    tm = ROW_TILE
    tq = Q_TILE
    tps = seq // tm
    hh = N_KV_HEADS
    row = lambda i: (i // tps, i % tps, 0)
    fix = lambda i: (0, 0)
    seq_tile = lambda i: (i % tps, 0)
    tok_minor = lambda i: (i // tps, 0, 0, i % tps)
    tok_major = lambda i: (i // tps, 0, i % tps, 0)
    tok_tiles = lambda i: (i // tps, 0, i % tps, 0, 0)
    return pl.pallas_call(
        functools.partial(_proj_kernel, tiles_per_seq=tps),
        grid=(bsz * tps,),
        in_specs=[
            pl.BlockSpec((None, tm, D_MODEL), row),
            pl.BlockSpec((1, D_MODEL), fix),
            pl.BlockSpec((D_MODEL, PROJ_COLS), fix),
            pl.BlockSpec((3, D_CONV), fix),
            pl.BlockSpec((1, D_CONV), fix),
            pl.BlockSpec((tm, AUG_K), seq_tile),
            pl.BlockSpec((tm, AUG_K), seq_tile),
        ],
        out_specs=[
            pl.BlockSpec((None, hh, GQA_GROUP * HEAD_DIM, tm), tok_minor),
            pl.BlockSpec((None, hh, GATE_ROWS, tm), tok_minor),
            pl.BlockSpec((None, hh, tm, AUG_K), tok_major),
            pl.BlockSpec((None, hh, tm, AUG_K), tok_major),
            pl.BlockSpec((None, hh, tm // tq, V_ROWS, tq), tok_tiles),
            pl.BlockSpec((None, hh, tm // tq, V_ROWS, tq), tok_tiles),
            pl.BlockSpec((None, tm // CMP_STRIDE, CMP_STRIDE * 2 * D_KV), row),
            pl.BlockSpec((None, tm, D_CONV), row),
        ],
        out_shape=[
            jax.ShapeDtypeStruct((bsz, hh, GQA_GROUP * HEAD_DIM, seq), BF16),
            jax.ShapeDtypeStruct((bsz, hh, GATE_ROWS, seq), F32),
            jax.ShapeDtypeStruct((bsz, hh, seq, AUG_K), BF16),
            jax.ShapeDtypeStruct((bsz, hh, seq, AUG_K), BF16),
            jax.ShapeDtypeStruct((bsz, hh, seq // tq, V_ROWS, tq), BF16),
            jax.ShapeDtypeStruct((bsz, hh, seq // tq, V_ROWS, tq), BF16),
            jax.ShapeDtypeStruct((bsz, seq // CMP_STRIDE, CMP_STRIDE * 2 * D_KV), BF16),
            jax.ShapeDtypeStruct((bsz, seq, D_CONV), BF16),
        ],
        scratch_shapes=[pltpu.VMEM((tm + 8, D_CONV), F32), pltpu.VMEM((2, tm, D_KV), F32)],
        compiler_params=pltpu.CompilerParams(dimension_semantics=("arbitrary",), vmem_limit_bytes=VMEM_LIMIT),
    )(x3, g, w, cw, gc, csel, cwin)


N_STREAMS = 2 * N_KV_HEADS


def _compress_kernel(x_ref, w1_ref, pos_ref, w2_ref, c_ref, kc_ref, vc_ref, sbuf):
    nc = x_ref.shape[0]
    y = jnp.dot(x_ref[...], w1_ref[...], preferred_element_type=F32)
    pb = jnp.dot(pos_ref[...], w1_ref[...], preferred_element_type=F32)
    rowi = lax.broadcasted_iota(jnp.int32, (nc, 128), 0)
    lane = lax.broadcasted_iota(jnp.int32, (nc, 128), 1)
    sbuf[:, nc:nc + 8, :] = jnp.zeros((N_STREAMS, 8, CMP_HIDDEN), F32)
    for st in range(N_STREAMS):
        c0 = st * 2 * CMP_HIDDEN
        posb = pb[0:1, c0:c0 + CMP_HIDDEN] + pb[1:2, c0 + CMP_HIDDEN:c0 + 2 * CMP_HIDDEN]
        sbuf[st, 0:nc, :] = y[:, c0 + CMP_HIDDEN:c0 + 2 * CMP_HIDDEN]
        hid = y[:, c0:c0 + CMP_HIDDEN] + sbuf[st, 1:nc + 1, :] + posb
        out = jnp.dot(jax.nn.gelu(hid).astype(BF16), w2_ref[st], preferred_element_type=F32)
        out = jnp.where(rowi < nc - 1, out, 0.0)
        if st < N_KV_HEADS:
            kc_ref[st, :, 0:128] = jnp.where(lane < HEAD_DIM, out.astype(BF16), c_ref[:, 0:128])
            kc_ref[st, :, 128:AUG_K] = c_ref[:, 128:AUG_K]
        else:
            vc_ref[st - N_KV_HEADS] = out.T[0:HEAD_DIM, :].astype(BF16)


def _compress_call(xc, w1, pos8, w2, consts):
    b, nc, width = xc.shape
    hh = N_KV_HEADS
    fix2 = lambda i: (0, 0)
    return pl.pallas_call(
        _compress_kernel,
        grid=(b,),
        in_specs=[
            pl.BlockSpec((None, nc, width), lambda i: (i, 0, 0)),
            pl.BlockSpec((width, N_STREAMS * 2 * CMP_HIDDEN), fix2),
            pl.BlockSpec((8, width), fix2),
            pl.BlockSpec((N_STREAMS, CMP_HIDDEN, 128), lambda i: (0, 0, 0)),
            pl.BlockSpec((nc, AUG_K), fix2),
        ],
        out_specs=[
            pl.BlockSpec((None, hh, nc, AUG_K), lambda i: (i, 0, 0, 0)),
            pl.BlockSpec((None, hh, HEAD_DIM, nc), lambda i: (i, 0, 0, 0)),
        ],
        out_shape=[
            jax.ShapeDtypeStruct((b, hh, nc, AUG_K), BF16),
            jax.ShapeDtypeStruct((b, hh, HEAD_DIM, nc), BF16),
        ],
        scratch_shapes=[pltpu.VMEM((N_STREAMS, nc + 8, CMP_HIDDEN), F32)],
        compiler_params=pltpu.CompilerParams(dimension_semantics=("arbitrary",), vmem_limit_bytes=VMEM_LIMIT),
    )(xc, w1, pos8, w2, consts)


def _attn_kernel(qt_ref, gt_ref, ks_ref, kw_ref, vs_ref, vw_ref, kc_ref, vc_ref, msel_ref, cmask_ref, o_ref,
                 qaug, acc_ref, accw_ref, sbuf, pbuf, imp_ref, selm_ref, klist, *, topk):
    h = pl.program_id(1)
    qi = pl.program_id(2)
    tq = qt_ref.shape[1]
    r = GQA_GROUP * tq
    nc = kc_ref.shape[0]
    nsel = msel_ref.shape[0]
    nkb = ks_ref.shape[0]
    t0 = qi * tq

    lane16 = lax.broadcasted_iota(jnp.int32, (AUG_ALIBI_ROWS, r), 1)
    sub16 = lax.broadcasted_iota(jnp.int32, (AUG_ALIBI_ROWS, r), 0)
    gl = lane16 // tq
    off = (lane16 % tq).astype(F32)
    base = jnp.where(h == 0, 0.5, 0.03125).astype(F32)
    slope = jnp.where(gl == 0, base, jnp.where(gl == 1, base * 0.5, jnp.where(gl == 2, base * 0.25, base * 0.125)))
    blk0 = (t0 // SEL_BLOCK).astype(F32)
    c3 = jnp.where(sub16 % 3 == 0, LOG2E_3[0], jnp.where(sub16 % 3 == 1, LOG2E_3[1], LOG2E_3[2]))
    arow = jnp.where(sub16 < 3, slope * c3,
                     jnp.where(sub16 < 6, 64.0 * slope * c3,
                               jnp.where(sub16 == 6, -slope * LOG2E * (64.0 * blk0 + off), 0.0)))
    for g in range(GQA_GROUP):
        qaug[0:HEAD_DIM, g * tq:(g + 1) * tq] = qt_ref[g * HEAD_DIM:(g + 1) * HEAD_DIM, :]
    qaug[AUG_ALIBI:AUG_SEL, :] = arow.astype(BF16)
    qaug[AUG_SEL:AUG_K, :] = jnp.zeros((AUG_K - AUG_SEL, r), BF16)

    lane_q = lax.broadcasted_iota(jnp.int32, (1, r), 1) % tq
    key_i = lax.broadcasted_iota(jnp.int32, (tq, r), 0)
    causal = key_i <= lane_q

    q_nosel = qaug[...]
    nwin = WINDOW // tq

    def win_scores(w):
        kb = qi - nwin + w
        s = jnp.dot(kw_ref[jnp.maximum(kb, 0)], q_nosel, preferred_element_type=F32)
        if w == 0:
            return jnp.where((key_i > lane_q) & (kb >= 0), s, NEG)
        if w == nwin:
            return jnp.where(causal, s, NEG)
        return jnp.where(kb >= 0, s, NEG)

    def win_update(w, s, m_old):
        m_new = jnp.maximum(m_old, jnp.max(s, axis=0, keepdims=True))
        p = jnp.exp2(s - m_new).astype(BF16)
        pv_w = jnp.dot(vw_ref[jnp.maximum(qi - nwin + w, 0)], p, preferred_element_type=F32)
        accw_ref[...] = pv_w if w == 0 else accw_ref[...] * jnp.exp2(m_old - m_new) + pv_w
        return m_new

    sc = jnp.dot(kc_ref[...], q_nosel, preferred_element_type=F32)
    win_s = [win_scores(0)]
    mask_c = cmask_ref[...] <= t0
    sc = jnp.where(mask_c, sc, NEG)
    m_c = jnp.max(sc, axis=0, keepdims=True)
    e_c = jnp.exp2(sc - m_c)
    l_c = jnp.sum(e_c, axis=0, keepdims=True)
    p_c = e_c * jnp.where(m_c > 0.5 * NEG, 1.0 / l_c, 0.0)
    o_cmp = jnp.dot(vc_ref[...], p_c.astype(BF16), preferred_element_type=F32)

    p_sum = p_c[:, 0:tq]
    for g in range(1, GQA_GROUP):
        p_sum = p_sum + p_c[:, g * tq:(g + 1) * tq]
    p1 = p_sum.astype(BF16)
    r1 = p_sum - p1.astype(F32)
    p2 = r1.astype(BF16)
    p3 = (r1 - p2.astype(F32)).astype(BF16)
    msel = msel_ref[...]
    imp = (jnp.dot(msel, p1, preferred_element_type=F32) + jnp.dot(msel, p2, preferred_element_type=F32)
           + jnp.dot(msel, p3, preferred_element_type=F32))

    m_w = jnp.full((1, r), NEG, F32)
    for w in range(nwin):
        win_s.append(win_scores(w + 1))
        m_w = win_update(w, win_s[w], m_w)

    jj =lax.broadcasted_iota(jnp.int32, (nsel, tq), 0)
    jt = (t0 + lax.broadcasted_iota(jnp.int32, (nsel, tq), 1)) // SEL_BLOCK
    imp = jnp.where((jj == 0) | (jj == jt) | (jj == jt - 1), BIG, imp)
    imp = jnp.where(jj > jt, NEG, imp)
    imp_ref[...] = imp
    bpt = tq // SEL_BLOCK
    sub8 = lax.broadcasted_iota(jnp.int32, (8, tq), 0)

    def rank_select(nblk):
        groups = [imp_ref[8 * gi:8 * gi + 8, :] for gi in range(nblk // 8)]
        cnts = [jnp.zeros((8, tq), jnp.int32) for _ in groups]
        for jp in range(nblk):
            rowv = jnp.broadcast_to(imp_ref[jp:jp + 1, :], (8, tq))
            for gi, grp in enumerate(groups):
                if 8 * gi > jp:
                    beats = rowv >= grp
                elif 8 * gi + 7 <= jp:
                    beats = rowv > grp
                else:
                    beats = (rowv > grp) | ((rowv == grp) & (sub8 + 8 * gi > jp))
                cnts[gi] = cnts[gi] + jnp.where(beats, 1, 0)
        selm = [jnp.where(c < topk, 1.0, 0.0) for c in cnts] + [jnp.zeros((nsel - nblk, tq), F32)] * (nblk < nsel)
        selm = jnp.concatenate(selm, axis=0)
        selm_ref[...] = selm
        selbias = jnp.where(selm > 0.0, 0.0, NEG).astype(BF16)
        for g in range(GQA_GROUP):
            qaug[AUG_SEL:AUG_SEL + nsel, g * tq:(g + 1) * tq] = selbias

    for idx in range(nsel // 16):
        pl.when((qi * bpt) // 16 == idx)(functools.partial(rank_select, 16 * (idx + 1)))

    q_all = qaug[...]
    sbuf[0] = jnp.dot(ks_ref[0], q_all, preferred_element_type=F32)
    win_update(nwin, win_s[nwin], m_w)
    acc_w = accw_ref[...]
    o_win = acc_w[0:HEAD_DIM, :] * (1.0 / acc_w[HEAD_DIM:HEAD_DIM + 1, :])

    n_use = jnp.int32(0)
    for gi in range(nsel // 8):
        hit = jnp.max(selm_ref[8 * gi:8 * gi + 8, :], axis=1, keepdims=True)
        for part in range(8 // bpt):
            kb = gi * (8 // bpt) + part
            if kb < nkb - 1:
                klist[n_use] = kb
                used = (jnp.max(hit[part * bpt:(part + 1) * bpt, :]) > 0.0) & (kb < qi)
                n_use = n_use + used.astype(jnp.int32)
    klist[n_use] = qi

    def qk(kb):
        return jnp.dot(ks_ref[kb], q_all, preferred_element_type=F32)

    def softmax_tile(s, m_old):
        m_new = jnp.maximum(m_old, jnp.max(s, axis=0, keepdims=True))
        return m_new, jnp.exp2(s - m_new).astype(BF16), jnp.exp2(m_old - m_new)

    def pv(kb, p, alpha):
        acc_ref[...] = acc_ref[...] * alpha + jnp.dot(vs_ref[kb], p, preferred_element_type=F32)

    acc_ref[...] = jnp.zeros(acc_ref.shape, F32)
    pbuf[1] = jnp.zeros((tq, r), BF16)

    def step(j, cur, carry):
        m_old, alpha_prev = carry
        k_next = ks_ref[klist[j + 1]]
        v_prev = vs_ref[klist[jnp.maximum(j - 1, 0)]]
        m_news, alphas = [], []
        for g in range(GQA_GROUP):
            cols = slice(g * tq, (g + 1) * tq)
            sbuf[1 - cur, :, cols] = jnp.dot(k_next, qaug[:, cols], preferred_element_type=F32)
            acc_ref[:, cols] = (acc_ref[:, cols] * alpha_prev[:, cols]
                                + jnp.dot(v_prev, pbuf[1 - cur, :, cols], preferred_element_type=F32))
            m_new, p, alpha = softmax_tile(sbuf[cur, :, cols], m_old[:, cols])
            pbuf[cur, :, cols] = p
            m_news.append(m_new)
            alphas.append(alpha)
        return jnp.concatenate(m_news, axis=1), jnp.concatenate(alphas, axis=1)

    def finish(cur, carry):
        m_old, alpha_prev = carry
        _, p_last, alpha_last = softmax_tile(jnp.where(causal, sbuf[cur], NEG), m_old)
        pv(klist[jnp.maximum(n_use - 1, 0)], pbuf[1 - cur], alpha_prev)
        pv(qi, p_last, alpha_last)

    def unrolled(i, carry):
        for u in range(SEL_UNROLL):
            carry = step(SEL_UNROLL * i + u, u % 2, carry)
        return carry

    carry0 = (jnp.full((1, r), NEG, F32), jnp.ones((1, r), F32))
    n_main = n_use // SEL_UNROLL
    carry_main = lax.fori_loop(0, n_main, unrolled, carry0)
    for rem in range(SEL_UNROLL):
        @pl.when(n_use % SEL_UNROLL == rem)
        def _(rem=rem):
            carry = carry_main
            for u in range(rem):
                carry = step(SEL_UNROLL * n_main + u, u % 2, carry)
            finish(rem % 2, carry)

    acc_s = acc_ref[...]
    o_sel = acc_s[0:HEAD_DIM, :] * (1.0 / acc_s[HEAD_DIM:HEAD_DIM + 1, :])

    gt = gt_ref[...]
    for g in range(GQA_GROUP):
        sl = slice(g * tq, (g + 1) * tq)
        og = (gt[3 * g:3 * g + 1, :] * o_cmp[:, sl] + gt[3 * g + 1:3 * g + 2, :] * o_sel[:, sl]
              + gt[3 * g + 2:3 * g + 3, :] * o_win[:, sl])
        o_ref[g * HEAD_DIM:(g + 1) * HEAD_DIM, :] = og


def _attn_call(qt, gt, ks, kw, vs, vw, kc, vc, msel, cmask, topk):
    b, hh, _, t = qt.shape
    tq = Q_TILE
    nkb = t // tq
    nc = kc.shape[2]
    nsel = msel.shape[0]
    per_q = lambda i, j, k: (i, j, 0, k)
    per_bh4 = lambda i, j, k: (i, j, 0, 0)
    per_bh5 = lambda i, j, k: (i, j, 0, 0, 0)
    return pl.pallas_call(
        functools.partial(_attn_kernel, topk=topk),
        grid=(b, hh, nkb),
        in_specs=[
            pl.BlockSpec((None, None, GQA_GROUP * HEAD_DIM, tq), per_q),
            pl.BlockSpec((None, None, GATE_ROWS, tq), per_q),
            pl.BlockSpec((None, None, nkb, tq, AUG_K), per_bh5),
            pl.BlockSpec((None, None, nkb, tq, AUG_K), per_bh5),
            pl.BlockSpec((None, None, nkb, V_ROWS, tq), per_bh5),
            pl.BlockSpec((None, None, nkb, V_ROWS, tq), per_bh5),
            pl.BlockSpec((None, None, nc, AUG_K), per_bh4),
            pl.BlockSpec((None, None, HEAD_DIM, nc), per_bh4),
            pl.BlockSpec((nsel, nc), lambda i, j, k: (0, 0)),
            pl.BlockSpec((nc, GQA_GROUP * tq), lambda i, j, k: (0, 0)),
        ],
        out_specs=pl.BlockSpec((None, GQA_GROUP * HEAD_DIM, tq), lambda i, j, k: (i, j, k)),
        out_shape=jax.ShapeDtypeStruct((b, D_ATTN, t), F32),
        scratch_shapes=[pltpu.VMEM((AUG_K, GQA_GROUP * tq), BF16), pltpu.VMEM((V_ROWS, GQA_GROUP * tq), F32),
                        pltpu.VMEM((V_ROWS, GQA_GROUP * tq), F32),
                        pltpu.VMEM((2, tq, GQA_GROUP * tq), F32), pltpu.VMEM((2, tq, GQA_GROUP * tq), BF16),
                        pltpu.VMEM((nsel, tq), F32), pltpu.VMEM((nsel, tq), F32),
                        pltpu.SMEM((nkb + 1,), jnp.int32)],
        compiler_params=pltpu.CompilerParams(dimension_semantics=("arbitrary", "arbitrary", "arbitrary"),
                                             vmem_limit_bytes=VMEM_LIMIT),
    )(qt, gt, ks, kw, vs, vw, kc, vc, msel, cmask)


def _ffn_kernel(x_ref, oa_ref, mc_ref, ga_ref, wo_ref, gf_ref, wg_ref, wu_ref, cw_ref, cb_ref, wd_ref, gl_ref,
                o_ref, gbuf, ybuf, *, tiles_per_seq):
    i = pl.program_id(0)
    tm = x_ref.shape[0]

    @pl.when(i % tiles_per_seq == 0)
    def _():
        gbuf[0:8, :] = jnp.zeros((8, D_FF), F32)

    @pl.when(i % tiles_per_seq != 0)
    def _():
        gbuf[0:8, :] = gbuf[tm:tm + 8, :]

    subs = [slice(r0, r0 + SUB_ROWS) for r0 in range(0, tm, SUB_ROWS)]
    x1s, h2s = [], []
    for rows in subs:
        oat = oa_ref[:, rows]
        mat = (oat * lax.rsqrt(jnp.mean(oat * oat, axis=0, keepdims=True) + EPS) * ga_ref[...]).astype(BF16)
        x1 = (x_ref[rows, :] + lax.dot_general(mat, wo_ref[0:D_ATTN, :], (((0,), (0,)), ((), ())),
                                                preferred_element_type=F32)
              + jnp.dot(mc_ref[rows, :], wo_ref[D_ATTN:D_MODEL, :], preferred_element_type=F32))
        x1s.append(x1)
        h2s.append(_rms(x1, gf_ref[...]).astype(BF16))
    for rows, x1, h2 in zip(subs, x1s, h2s):
        r0 = rows.start
        for c in range(D_FF // FF_CHUNK):
            cs = slice(c * FF_CHUNK, (c + 1) * FF_CHUNK)
            gpre = jnp.dot(h2, wg_ref[:, cs], preferred_element_type=F32)
            up = jnp.dot(h2, wu_ref[:, cs], preferred_element_type=F32)
            gbuf[8 + r0:8 + r0 + SUB_ROWS, cs] = gpre
            gate = (cw_ref[0:1, cs] * gbuf[6 + r0:6 + r0 + SUB_ROWS, cs]
                    + cw_ref[1:2, cs] * gbuf[7 + r0:7 + r0 + SUB_ROWS, cs] + cw_ref[2:3, cs] * gpre + cb_ref[:, cs])
            ybuf[rows, cs] = (jax.nn.silu(gate) * up).astype(BF16)
        acc = x1 + jnp.dot(ybuf[rows, :], wd_ref[...], preferred_element_type=F32)
        o_ref[rows, :] = _rms(acc, gl_ref[...])


def _ffn_call(x2, oat, mc, ga, wo, gf, wg, wu, cw, cb, wd, gl):
    n = x2.shape[0]
    seq = oat.shape[2]
    tm = ROW_TILE
    tps = seq // tm
    row = lambda i: (i, 0)
    fix = lambda i: (0, 0)
    once = dict(pipeline_mode=pl.Buffered(1))
    return pl.pallas_call(
        functools.partial(_ffn_kernel, tiles_per_seq=seq // tm),
        grid=(n // tm,),
        in_specs=[
            pl.BlockSpec((tm, D_MODEL), row),
            pl.BlockSpec((None, D_ATTN, tm), lambda i: (i // tps, 0, i % tps)),
            pl.BlockSpec((tm, D_CONV), row),
            pl.BlockSpec((D_ATTN, 1), fix),
            pl.BlockSpec((D_MODEL, D_MODEL), fix, **once),
            pl.BlockSpec((1, D_MODEL), fix),
            pl.BlockSpec((D_MODEL, D_FF), fix, **once),
            pl.BlockSpec((D_MODEL, D_FF), fix, **once),
            pl.BlockSpec((3, D_FF), fix),
            pl.BlockSpec((1, D_FF), fix),
            pl.BlockSpec((D_FF, D_MODEL), fix, **once),
            pl.BlockSpec((1, D_MODEL), fix),
        ],
        out_specs=pl.BlockSpec((tm, D_MODEL), row),
        out_shape=jax.ShapeDtypeStruct((n, D_MODEL), F32),
        scratch_shapes=[pltpu.VMEM((tm + 8, D_FF), F32), pltpu.VMEM((tm, D_FF), BF16)],
        compiler_params=pltpu.CompilerParams(dimension_semantics=("arbitrary",), vmem_limit_bytes=VMEM_LIMIT),
    )(x2, oat, mc, ga, wo, gf, wg, wu, cw, cb, wd, gl)


def _alibi_cols(pos):
    cols = np.zeros((len(pos), AUG_ALIBI_ROWS), np.float32)
    cols[:, 0:3] = (pos % SEL_BLOCK)[:, None]
    cols[:, 3:6] = (pos // SEL_BLOCK)[:, None]
    cols[:, 6] = 1.0
    return cols


def _key_consts(t, with_sel):
    pos = np.arange(t)
    c = np.zeros((t, AUG_K), np.float32)
    c[:, AUG_ALIBI:AUG_SEL] = _alibi_cols(pos)
    if with_sel:
        c[pos, AUG_SEL + pos // SEL_BLOCK] = 1.0
    return jnp.asarray(c, BF16)


def _cmp_consts(nc):
    c = np.zeros((nc, AUG_K), np.float32)
    c[:, AUG_ALIBI:AUG_SEL] = _alibi_cols(np.arange(nc) * CMP_STRIDE + (CMP_BLOCK - 1))
    return jnp.asarray(c, BF16)


def _compress_weights(w_k1, w_v1, pos_k, pos_v, w_k2, w_v2):
    hh, dk, half = N_KV_HEADS, HEAD_DIM, CMP_STRIDE
    w1 = jnp.zeros((half, N_STREAMS, dk, N_STREAMS, 2, CMP_HIDDEN), F32)
    pos = []
    for st in range(N_STREAMS):
        w = (w_k1 if st < hh else w_v1).reshape(2, half, dk, CMP_HIDDEN)
        w1 = w1.at[:, st, :, st, :, :].set(w.transpose(1, 2, 0, 3))
        pos.append((pos_k if st < hh else pos_v).reshape(2, half, dk))
    pos = jnp.stack(pos, axis=2).reshape(2, half * N_STREAMS * dk)
    pos8 = jnp.concatenate([pos, jnp.zeros((6, pos.shape[1]), F32)], axis=0)
    w2 = jnp.stack([jnp.concatenate([w_k2 if st < hh else w_v2, jnp.zeros((CMP_HIDDEN, 128 - dk), F32)], axis=1)
                    for st in range(N_STREAMS)])
    w1 = w1.reshape(half * N_STREAMS * dk, N_STREAMS * 2 * CMP_HIDDEN)
    return w1.astype(BF16), pos8.astype(BF16), w2.astype(BF16)


def _cmp_limits(nc):
    end = np.arange(nc)[:, None] * CMP_STRIDE + (CMP_BLOCK - 1)
    off = np.arange(GQA_GROUP * Q_TILE)[None, :] % Q_TILE
    return jnp.asarray(end - off, jnp.int32)


def _sel_map_t(t, nc):
    n_cmp = (t - CMP_BLOCK) // CMP_STRIDE + 1
    n_sel = t // SEL_BLOCK
    cs = np.arange(n_cmp)[:, None] * CMP_STRIDE
    ss = np.arange(n_sel)[None, :] * SEL_BLOCK
    ov = np.maximum(0, np.minimum(cs + CMP_BLOCK, ss + SEL_BLOCK) - np.maximum(cs, ss)) / CMP_BLOCK
    m = np.zeros((n_sel, nc), np.float32)
    m[:, :n_cmp] = ov.T
    return jnp.asarray(m, BF16)


def kernel(x, norm_mix_g, w_in, pos_ck, w_ck1, w_ck2, pos_cv, w_cv1, w_cv2, conv_mix_w, norm_out_attn_g,
           norm_out_conv_g, w_out, norm_ffn_g, w_gate, w_up, ffn_conv_w, ffn_conv_b, w_down, norm_final_g):
    b, t, _ = x.shape
    hh, dk = N_KV_HEADS, HEAD_DIM
    assert t % ROW_TILE == 0 and t % Q_TILE == 0 and WINDOW % Q_TILE == 0 and t // SEL_BLOCK <= AUG_K - AUG_SEL
    assert ROW_TILE % SUB_ROWS == 0 and SUB_ROWS % Q_TILE == 0 and (t // SEL_BLOCK) % 16 == 0
    nc = t // CMP_STRIDE
    nsel = t // SEL_BLOCK
    nkb = t // Q_TILE
    depth = w_in.shape[0]
    assert depth == 1
    xx = x
    for l in range(depth):
        wi = w_in[l]
        col = np.cumsum([0, D_ATTN] + [D_KV] * 6 + [3 * N_HEADS_ATTN] + [D_CONV] * 3)
        kv_slabs = [wi[:, col[3 + 2 * br] + hd * dk:col[3 + 2 * br] + (hd + 1) * dk] if part == 0 else
                    wi[:, col[4 + 2 * br] + hd * dk:col[4 + 2 * br] + (hd + 1) * dk]
                    for br in range(2) for hd in range(hh) for part in range(2)]
        gate_cols = [jnp.concatenate([wi[:, col[7] + 12 * hd:col[7] + 12 * (hd + 1)], jnp.zeros((D_MODEL, 4), F32)], axis=1)
                     for hd in range(hh)]
        w_p = jnp.concatenate([wi[:, 0:D_ATTN]] + kv_slabs + [wi[:, col[1]:col[3]], wi[:, col[8]:col[11]]] + gate_cols
                              + [jnp.zeros((D_MODEL, 128 - 2 * GATE_ROWS), F32)], axis=1).astype(BF16)
        assert w_p.shape[1] == PROJ_COLS
        qt, gt, ksa, kwa, vst, vwt, kvc, mixed_conv = _proj_call(
            xx, norm_mix_g[l][None], w_p, conv_mix_w[l], norm_out_conv_g[l][None],
            _key_consts(t, True), _key_consts(t, False))

        kc, vc = _compress_call(kvc, *_compress_weights(w_ck1[l], w_cv1[l], pos_ck[l], pos_cv[l], w_ck2[l], w_cv2[l]),
                                _cmp_consts(nc))

        o_attn = _attn_call(qt, gt, ksa.reshape(b, hh, nkb, Q_TILE, AUG_K), kwa.reshape(b, hh, nkb, Q_TILE, AUG_K),
                            vst, vwt, kc, vc, _sel_map_t(t, nc), _cmp_limits(nc), min(SEL_TOPK, nsel))

        xx = _ffn_call(xx.reshape(b * t, D_MODEL), o_attn, mixed_conv.reshape(b * t, D_CONV),
                       norm_out_attn_g[l][:, None], w_out[l].astype(BF16), norm_ffn_g[l][None], w_gate[l].astype(BF16),
                       w_up[l].astype(BF16), ffn_conv_w[l], ffn_conv_b[l][None], w_down[l].astype(BF16),
                       norm_final_g[None])
    return xx.reshape(b, t, D_MODEL)
```

```python
import functools

import jax
import jax.numpy as jnp
import numpy as np
from jax import lax
from jax.experimental import pallas as pl
from jax.experimental.pallas import tpu as pltpu

F32 = jnp.float32
BF16 = jnp.bfloat16

D_MODEL = 1024
N_KV_HEADS = 2
GQA_GROUP = 4
N_HEADS_ATTN = N_KV_HEADS * GQA_GROUP
HEAD_DIM = 64
D_ATTN = N_HEADS_ATTN * HEAD_DIM
D_KV = N_KV_HEADS * HEAD_DIM
D_CONV = D_MODEL - D_ATTN
CMP_BLOCK = 32
CMP_STRIDE = 16
CMP_HIDDEN = 2 * HEAD_DIM
SEL_BLOCK = 64
SEL_TOPK = 16
WINDOW = 512
D_FF = 2816
EPS = 1e-6
NEG = -1e30
BIG = 1e30

AUG_K = 256
AUG_ALIBI = HEAD_DIM
AUG_ALIBI_ROWS = 16
AUG_SEL = AUG_ALIBI + AUG_ALIBI_ROWS
V_ROWS = 80

ROW_TILE = 512
SUB_ROWS = 256
Q_TILE = 256
SEL_UNROLL = 4
FF_CHUNK = 256
PROJ_COLS = 2944
VMEM_LIMIT = 56 * 1024 * 1024

LOG2E = 1.4426950408889634


def _bf16_terms(x, n):
    out = []
    for _ in range(n):
        t = float(np.asarray(x, np.float32).astype(jnp.bfloat16).astype(np.float32))
        out.append(t)
        x = x - t
    return tuple(out)


LOG2E_3 = _bf16_terms(LOG2E, 3)


def _rms(x, g):
    return x * lax.rsqrt(jnp.mean(x * x, axis=-1, keepdims=True) + EPS) * g


COL_Q = 0
COL_KV = D_ATTN
COL_CMP = COL_KV + 4 * 2 * HEAD_DIM
COL_B = COL_CMP + 2 * D_KV
COL_C = COL_B + D_CONV
COL_U = COL_C + D_CONV
COL_GATE = COL_U + D_CONV
GATE_ROWS = 16


def _proj_kernel(x_ref, g_ref, w_ref, cw_ref, gc_ref, csel_ref, cwin_ref,
                 qt_ref, gt_ref, ksa_ref, kwa_ref, vst_ref, vwt_ref, kvc_ref, mc_ref, cbuf, cst, *, tiles_per_seq):
    i = pl.program_id(0)
    tm = x_ref.shape[0]
    tq = vst_ref.shape[-1]

    @pl.when(i % tiles_per_seq == 0)
    def _():
        cbuf[0:8, :] = jnp.zeros((8, D_CONV), F32)

    @pl.when(i % tiles_per_seq != 0)
    def _():
        cbuf[0:8, :] = cbuf[tm:tm + 8, :]

    lane = lax.broadcasted_iota(jnp.int32, (SUB_ROWS, 128), 1)
    ones_rows = jnp.where(lax.broadcasted_iota(jnp.int32, (V_ROWS - HEAD_DIM, tq), 0) == 0, 1.0, 0.0).astype(BF16)
    for sub in range(tm // SUB_ROWS):
        r0 = sub * SUB_ROWS
        rows = slice(r0, r0 + SUB_ROWS)
        h = _rms(x_ref[rows, :], g_ref[...])
        p = jnp.dot(h.astype(BF16), w_ref[...], preferred_element_type=F32)

        for hd in range(N_KV_HEADS):
            qs = p[:, COL_Q + hd * 256:COL_Q + (hd + 1) * 256] * (HEAD_DIM ** -0.5 * LOG2E)
            qt_ref[hd, :, rows] = qs.T.astype(BF16)
        gt = jax.nn.sigmoid(p[:, COL_GATE:COL_GATE + 128]).T
        for hd in range(N_KV_HEADS):
            gt_ref[hd, :, rows] = gt[hd * GATE_ROWS:(hd + 1) * GATE_ROWS, :]

        for branch, (ka_ref, vt_ref, c_ref) in enumerate(((ksa_ref, vst_ref, csel_ref), (kwa_ref, vwt_ref, cwin_ref))):
            for hd in range(N_KV_HEADS):
                c0 = COL_KV + (2 * branch + hd) * 128
                slab = p[:, c0:c0 + 128]
                ka_ref[hd, rows, 0:128] = jnp.where(lane < HEAD_DIM, slab.astype(BF16), c_ref[rows, 0:128])
                ka_ref[hd, rows, 128:AUG_K] = c_ref[rows, 128:AUG_K]
                vt = slab.T[HEAD_DIM:128, :].astype(BF16)
                for kt in range(SUB_ROWS // tq):
                    vt_ref[hd, r0 // tq + kt, 0:HEAD_DIM, :] = vt[:, kt * tq:(kt + 1) * tq]
                    vt_ref[hd, r0 // tq + kt, HEAD_DIM:V_ROWS, :] = ones_rows

        nch = SUB_ROWS // CMP_STRIDE
        for part in range(2):
            cst[part, rows, :] = p[:, COL_CMP + part * D_KV:COL_CMP + (part + 1) * D_KV]
            for tok in range(CMP_STRIDE):
                c0 = (2 * tok + part) * D_KV
                kvc_ref[r0 // CMP_STRIDE:r0 // CMP_STRIDE + nch, c0:c0 + D_KV] = (
                    cst[part, pl.ds(r0 + tok, nch, stride=CMP_STRIDE), :].astype(BF16))
        b = p[:, COL_B:COL_B + D_CONV]
        cu = p[:, COL_C:COL_C + D_CONV] * p[:, COL_U:COL_U + D_CONV]
        cbuf[8 + r0:8 + r0 + SUB_ROWS, :] = cu
        y = (cw_ref[0:1, :] * cbuf[6 + r0:6 + r0 + SUB_ROWS, :] + cw_ref[1:2, :] * cbuf[7 + r0:7 + r0 + SUB_ROWS, :]
             + cw_ref[2:3, :] * cu)
        mc_ref[rows, :] = _rms(b * y, gc_ref[...]).astype(BF16)


def _proj_call(x3, g, w, cw, gc, csel, cwin):
    bsz, seq, _ = x3.shape
    tm = ROW_TILE
    tq = Q_TILE
    tps = seq // tm
    hh = N_KV_HEADS
    row = lambda i: (i // tps, i % tps, 0)
    fix = lambda i: (0, 0)
    seq_tile = lambda i: (i % tps, 0)
    tok_minor = lambda i: (i // tps, 0, 0, i % tps)
    tok_major = lambda i: (i // tps, 0, i % tps, 0)
    tok_tiles = lambda i: (i // tps, 0, i % tps, 0, 0)
    return pl.pallas_call(
        functools.partial(_proj_kernel, tiles_per_seq=tps),
        grid=(bsz * tps,),
        in_specs=[
            pl.BlockSpec((None, tm, D_MODEL), row),
            pl.BlockSpec((1, D_MODEL), fix),
            pl.BlockSpec((D_MODEL, PROJ_COLS), fix),
            pl.BlockSpec((3, D_CONV), fix),
            pl.BlockSpec((1, D_CONV), fix),
            pl.BlockSpec((tm, AUG_K), seq_tile),
            pl.BlockSpec((tm, AUG_K), seq_tile),
        ],
        out_specs=[
            pl.BlockSpec((None, hh, GQA_GROUP * HEAD_DIM, tm), tok_minor),
            pl.BlockSpec((None, hh, GATE_ROWS, tm), tok_minor),
            pl.BlockSpec((None, hh, tm, AUG_K), tok_major),
            pl.BlockSpec((None, hh, tm, AUG_K), tok_major),
            pl.BlockSpec((None, hh, tm // tq, V_ROWS, tq), tok_tiles),
            pl.BlockSpec((None, hh, tm // tq, V_ROWS, tq), tok_tiles),
            pl.BlockSpec((None, tm // CMP_STRIDE, CMP_STRIDE * 2 * D_KV), row),
            pl.BlockSpec((None, tm, D_CONV), row),
        ],
        out_shape=[
            jax.ShapeDtypeStruct((bsz, hh, GQA_GROUP * HEAD_DIM, seq), BF16),
            jax.ShapeDtypeStruct((bsz, hh, GATE_ROWS, seq), F32),
            jax.ShapeDtypeStruct((bsz, hh, seq, AUG_K), BF16),
            jax.ShapeDtypeStruct((bsz, hh, seq, AUG_K), BF16),
            jax.ShapeDtypeStruct((bsz, hh, seq // tq, V_ROWS, tq), BF16),
            jax.ShapeDtypeStruct((bsz, hh, seq // tq, V_ROWS, tq), BF16),
            jax.ShapeDtypeStruct((bsz, seq // CMP_STRIDE, CMP_STRIDE * 2 * D_KV), BF16),
            jax.ShapeDtypeStruct((bsz, seq, D_CONV), BF16),
        ],
        scratch_shapes=[pltpu.VMEM((tm + 8, D_CONV), F32), pltpu.VMEM((2, tm, D_KV), F32)],
        compiler_params=pltpu.CompilerParams(dimension_semantics=("arbitrary",), vmem_limit_bytes=VMEM_LIMIT),
    )(x3, g, w, cw, gc, csel, cwin)


N_STREAMS = 2 * N_KV_HEADS


def _compress_kernel(x_ref, w1_ref, pos_ref, w2_ref, c_ref, kc_ref, vc_ref, sbuf):
    nc = x_ref.shape[0]
    y = jnp.dot(x_ref[...], w1_ref[...], preferred_element_type=F32)
    pb = jnp.dot(pos_ref[...], w1_ref[...], preferred_element_type=F32)
    rowi = lax.broadcasted_iota(jnp.int32, (nc, 128), 0)
    lane = lax.broadcasted_iota(jnp.int32, (nc, 128), 1)
    sbuf[:, nc:nc + 8, :] = jnp.zeros((N_STREAMS, 8, CMP_HIDDEN), F32)
    for st in range(N_STREAMS):
        c0 = st * 2 * CMP_HIDDEN
        posb = pb[0:1, c0:c0 + CMP_HIDDEN] + pb[1:2, c0 + CMP_HIDDEN:c0 + 2 * CMP_HIDDEN]
        sbuf[st, 0:nc, :] = y[:, c0 + CMP_HIDDEN:c0 + 2 * CMP_HIDDEN]
        hid = y[:, c0:c0 + CMP_HIDDEN] + sbuf[st, 1:nc + 1, :] + posb
        out = jnp.dot(jax.nn.gelu(hid).astype(BF16), w2_ref[st], preferred_element_type=F32)
        out = jnp.where(rowi < nc - 1, out, 0.0)
        if st < N_KV_HEADS:
            kc_ref[st, :, 0:128] = jnp.where(lane < HEAD_DIM, out.astype(BF16), c_ref[:, 0:128])
            kc_ref[st, :, 128:AUG_K] = c_ref[:, 128:AUG_K]
        else:
            vc_ref[st - N_KV_HEADS] = out.T[0:HEAD_DIM, :].astype(BF16)


def _compress_call(xc, w1, pos8, w2, consts):
    b, nc, width = xc.shape
    hh = N_KV_HEADS
    fix2 = lambda i: (0, 0)
    return pl.pallas_call(
        _compress_kernel,
        grid=(b,),
        in_specs=[
            pl.BlockSpec((None, nc, width), lambda i: (i, 0, 0)),
            pl.BlockSpec((width, N_STREAMS * 2 * CMP_HIDDEN), fix2),
            pl.BlockSpec((8, width), fix2),
            pl.BlockSpec((N_STREAMS, CMP_HIDDEN, 128), lambda i: (0, 0, 0)),
            pl.BlockSpec((nc, AUG_K), fix2),
        ],
        out_specs=[
            pl.BlockSpec((None, hh, nc, AUG_K), lambda i: (i, 0, 0, 0)),
            pl.BlockSpec((None, hh, HEAD_DIM, nc), lambda i: (i, 0, 0, 0)),
        ],
        out_shape=[
            jax.ShapeDtypeStruct((b, hh, nc, AUG_K), BF16),
            jax.ShapeDtypeStruct((b, hh, HEAD_DIM, nc), BF16),
        ],
        scratch_shapes=[pltpu.VMEM((N_STREAMS, nc + 8, CMP_HIDDEN), F32)],
        compiler_params=pltpu.CompilerParams(dimension_semantics=("arbitrary",), vmem_limit_bytes=VMEM_LIMIT),
    )(xc, w1, pos8, w2, consts)


def _attn_kernel(qt_ref, gt_ref, ks_ref, kw_ref, vs_ref, vw_ref, kc_ref, vc_ref, msel_ref, cmask_ref, o_ref,
                 qaug, acc_ref, accw_ref, sbuf, pbuf, pwbuf, pcbuf, imp_ref, selm_ref, klist, *, topk):
    h = pl.program_id(1)
    qi = pl.program_id(2)
    tq = qt_ref.shape[1]
    r = GQA_GROUP * tq
    nc = kc_ref.shape[0]
    nsel = msel_ref.shape[0]
    nkb = ks_ref.shape[0]
    t0 = qi * tq

    lane16 = lax.broadcasted_iota(jnp.int32, (AUG_ALIBI_ROWS, r), 1)
    sub16 = lax.broadcasted_iota(jnp.int32, (AUG_ALIBI_ROWS, r), 0)
    gl = lane16 // tq
    off = (lane16 % tq).astype(F32)
    base = jnp.where(h == 0, 0.5, 0.03125).astype(F32)
    slope = jnp.where(gl == 0, base, jnp.where(gl == 1, base * 0.5, jnp.where(gl == 2, base * 0.25, base * 0.125)))
    blk0 = (t0 // SEL_BLOCK).astype(F32)
    c3 = jnp.where(sub16 % 3 == 0, LOG2E_3[0], jnp.where(sub16 % 3 == 1, LOG2E_3[1], LOG2E_3[2]))
    arow = jnp.where(sub16 < 3, slope * c3,
                     jnp.where(sub16 < 6, 64.0 * slope * c3,
                               jnp.where(sub16 == 6, -slope * LOG2E * (64.0 * blk0 + off), 0.0)))
    for g in range(GQA_GROUP):
        qaug[0:HEAD_DIM, g * tq:(g + 1) * tq] = qt_ref[g * HEAD_DIM:(g + 1) * HEAD_DIM, :]
    qaug[AUG_ALIBI:AUG_SEL, :] = arow.astype(BF16)
    qaug[AUG_SEL:AUG_K, :] = jnp.zeros((AUG_K - AUG_SEL, r), BF16)

    heads = [slice(g * tq, (g + 1) * tq) for g in range(GQA_GROUP)]
    q_i = lax.broadcasted_iota(jnp.int32, (tq, tq), 1)
    key_i = lax.broadcasted_iota(jnp.int32, (tq, tq), 0)
    causal = key_i <= q_i
    nwin = WINDOW // tq

    def win_scores(w, cols):
        kb = qi - nwin + w
        s = jnp.dot(kw_ref[jnp.maximum(kb, 0)], qaug[:, cols], preferred_element_type=F32)
        if w == 0:
            return jnp.where((key_i > q_i) & (kb >= 0), s, NEG)
        if w == nwin:
            return jnp.where(causal, s, NEG)
        return jnp.where(kb >= 0, s, NEG)

    def softmax_tile(s, m_old):
        m_new = jnp.maximum(m_old, jnp.max(s, axis=0, keepdims=True))
        return m_new, jnp.exp2(s - m_new).astype(BF16), jnp.exp2(m_old - m_new)

    def win_pv(w, cols, alpha):
        pv_w = jnp.dot(vw_ref[jnp.maximum(qi - nwin + w, 0)], pwbuf[w % 2, :, cols], preferred_element_type=F32)
        accw_ref[:, cols] = pv_w if w == 0 else accw_ref[:, cols] * alpha + pv_w

    mask_c = cmask_ref[...] <= t0
    cmp_s = [jnp.dot(kc_ref[...], qaug[:, cols], preferred_element_type=F32) for cols in heads]
    win_s = [win_scores(0, cols) for cols in heads]
    p_sum = jnp.zeros((nc, tq), F32)
    for g, cols in enumerate(heads):
        sc = jnp.where(mask_c, cmp_s[g], NEG)
        m_c = jnp.max(sc, axis=0, keepdims=True)
        e_c = jnp.exp2(sc - m_c)
        l_c = jnp.sum(e_c, axis=0, keepdims=True)
        p_c = e_c * jnp.where(m_c > 0.5 * NEG, 1.0 / l_c, 0.0)
        pcbuf[:, cols] = p_c.astype(BF16)
        p_sum = p_sum + p_c

    m_w = [jnp.full((1, tq), NEG, F32)] * GQA_GROUP
    a_w = [None] * GQA_GROUP
    o_cmp = []
    for w in range(nwin):
        nxt = []
        for g, cols in enumerate(heads):
            nxt.append(win_scores(w + 1, cols))
            if w == 0:
                o_cmp.append(jnp.dot(vc_ref[...], pcbuf[:, cols], preferred_element_type=F32))
            else:
                win_pv(w - 1, cols, a_w[g])
            m_w[g], p, a_w[g] = softmax_tile(win_s[g], m_w[g])
            pwbuf[w % 2, :, cols] = p
        win_s = nxt
        if w == 0:
            p1 = p_sum.astype(BF16)
            p2 = (p_sum - p1.astype(F32)).astype(BF16)
            msel = msel_ref[...]
            imp = (jnp.dot(msel, p1, preferred_element_type=F32)
                   + jnp.dot(msel, p2, preferred_element_type=F32))

    jj =lax.broadcasted_iota(jnp.int32, (nsel, tq), 0)
    jt = (t0 + lax.broadcasted_iota(jnp.int32, (nsel, tq), 1)) // SEL_BLOCK
    imp = jnp.where((jj == 0) | (jj == jt) | (jj == jt - 1), BIG, imp)
    imp = jnp.where(jj > jt, NEG, imp)
    imp_ref[...] = imp
    bpt = tq // SEL_BLOCK
    sub8 = lax.broadcasted_iota(jnp.int32, (8, tq), 0)

    def rank_select(nblk):
        groups = [imp_ref[8 * gi:8 * gi + 8, :] for gi in range(nblk // 8)]
        cnts = [jnp.zeros((8, tq), jnp.int32) for _ in groups]
        for jp in range(nblk):
            rowv = jnp.broadcast_to(imp_ref[jp:jp + 1, :], (8, tq))
            for gi, grp in enumerate(groups):
                if 8 * gi > jp:
                    beats = rowv >= grp
                elif 8 * gi + 7 <= jp:
                    beats = rowv > grp
                else:
                    beats = (rowv > grp) | ((rowv == grp) & (sub8 + 8 * gi > jp))
                cnts[gi] = cnts[gi] + jnp.where(beats, 1, 0)
        selm = [jnp.where(c < topk, 1.0, 0.0) for c in cnts] + [jnp.zeros((nsel - nblk, tq), F32)] * (nblk < nsel)
        selm = jnp.concatenate(selm, axis=0)
        selm_ref[...] = selm
        selbias = jnp.where(selm > 0.0, 0.0, NEG).astype(BF16)
        for g in range(GQA_GROUP):
            qaug[AUG_SEL:AUG_SEL + nsel, g * tq:(g + 1) * tq] = selbias

    for idx in range(nsel // 16):
        pl.when((qi * bpt) // 16 == idx)(functools.partial(rank_select, 16 * (idx + 1)))

    a_last = [None] * GQA_GROUP
    for g, cols in enumerate(heads):
        sbuf[0, :, cols] = jnp.dot(ks_ref[0], qaug[:, cols], preferred_element_type=F32)
        win_pv(nwin - 1, cols, a_w[g])
        _, p, a_last[g] = softmax_tile(win_s[g], m_w[g])
        pwbuf[nwin % 2, :, cols] = p

    n_use = jnp.int32(0)
    for gi in range(nsel // 8):
        hit = jnp.max(selm_ref[8 * gi:8 * gi + 8, :], axis=1, keepdims=True)
        for part in range(8 // bpt):
            kb = gi * (8 // bpt) + part
            if kb < nkb - 1:
                klist[n_use] = kb
                used = (jnp.max(hit[part * bpt:(part + 1) * bpt, :]) > 0.0) & (kb < qi)
                n_use = n_use + used.astype(jnp.int32)
    klist[n_use] = qi

    acc_ref[...] = jnp.zeros(acc_ref.shape, F32)
    pbuf[1] = jnp.zeros((tq, r), BF16)

    def step(j, cur, carry):
        m_old, alpha_prev = carry
        k_next = ks_ref[klist[j + 1]]
        v_prev = vs_ref[klist[jnp.maximum(j - 1, 0)]]
        m_news, alphas = [], []
        for g in range(GQA_GROUP):
            cols = slice(g * tq, (g + 1) * tq)
            sbuf[1 - cur, :, cols] = jnp.dot(k_next, qaug[:, cols], preferred_element_type=F32)
            acc_ref[:, cols] = (acc_ref[:, cols] * alpha_prev[:, cols]
                                + jnp.dot(v_prev, pbuf[1 - cur, :, cols], preferred_element_type=F32))
            m_new, p, alpha = softmax_tile(sbuf[cur, :, cols], m_old[:, cols])
            pbuf[cur, :, cols] = p
            m_news.append(m_new)
            alphas.append(alpha)
        return jnp.concatenate(m_news, axis=1), jnp.concatenate(alphas, axis=1)

    def finish(cur, carry):
        m_old, alpha_prev = carry
        v_prev = vs_ref[klist[jnp.maximum(n_use - 1, 0)]]
        v_last = vs_ref[qi]
        gt = gt_ref[...]
        for g, cols in enumerate(heads):
            _, p_last, alpha_last = softmax_tile(jnp.where(causal, sbuf[cur, :, cols], NEG), m_old[:, cols])
            acc = (acc_ref[:, cols] * alpha_prev[:, cols]
                   + jnp.dot(v_prev, pbuf[1 - cur, :, cols], preferred_element_type=F32))
            acc = acc * alpha_last + jnp.dot(v_last, p_last, preferred_element_type=F32)
            win_pv(nwin, cols, a_last[g])
            acc_w = accw_ref[:, cols]
            o_sel = acc[0:HEAD_DIM, :] * (1.0 / acc[HEAD_DIM:HEAD_DIM + 1, :])
            o_win = acc_w[0:HEAD_DIM, :] * (1.0 / acc_w[HEAD_DIM:HEAD_DIM + 1, :])
            o_ref[g * HEAD_DIM:(g + 1) * HEAD_DIM, :] = (gt[3 * g:3 * g + 1, :] * o_cmp[g] + gt[3 * g + 1:3 * g + 2, :] * o_sel
                                                         + gt[3 * g + 2:3 * g + 3, :] * o_win)

    def unrolled(i, carry):
        for u in range(SEL_UNROLL):
            carry = step(SEL_UNROLL * i + u, u % 2, carry)
        return carry

    carry0 = (jnp.full((1, r), NEG, F32), jnp.ones((1, r), F32))
    n_main = n_use // SEL_UNROLL
    carry_main = lax.fori_loop(0, n_main, unrolled, carry0)
    for rem in range(SEL_UNROLL):
        @pl.when(n_use % SEL_UNROLL == rem)
        def _(rem=rem):
            carry = carry_main
            for u in range(rem):
                carry = step(SEL_UNROLL * n_main + u, u % 2, carry)
            finish(rem % 2, carry)


def _attn_call(qt, gt, ks, kw, vs, vw, kc, vc, msel, cmask, topk):
    b, hh, _, t = qt.shape
    tq = Q_TILE
    nkb = t // tq
    nc = kc.shape[2]
    nsel = msel.shape[0]
    per_q = lambda i, j, k: (i, j, 0, k)
    per_bh4 = lambda i, j, k: (i, j, 0, 0)
    per_bh5 = lambda i, j, k: (i, j, 0, 0, 0)
    return pl.pallas_call(
        functools.partial(_attn_kernel, topk=topk),
        grid=(b, hh, nkb),
        in_specs=[
            pl.BlockSpec((None, None, GQA_GROUP * HEAD_DIM, tq), per_q),
            pl.BlockSpec((None, None, GATE_ROWS, tq), per_q),
            pl.BlockSpec((None, None, nkb, tq, AUG_K), per_bh5),
            pl.BlockSpec((None, None, nkb, tq, AUG_K), per_bh5),
            pl.BlockSpec((None, None, nkb, V_ROWS, tq), per_bh5),
            pl.BlockSpec((None, None, nkb, V_ROWS, tq), per_bh5),
            pl.BlockSpec((None, None, nc, AUG_K), per_bh4),
            pl.BlockSpec((None, None, HEAD_DIM, nc), per_bh4),
            pl.BlockSpec((nsel, nc), lambda i, j, k: (0, 0)),
            pl.BlockSpec((nc, tq), lambda i, j, k: (0, 0)),
        ],
        out_specs=pl.BlockSpec((None, GQA_GROUP * HEAD_DIM, tq), lambda i, j, k: (i, j, k)),
        out_shape=jax.ShapeDtypeStruct((b, D_ATTN, t), F32),
        scratch_shapes=[pltpu.VMEM((AUG_K, GQA_GROUP * tq), BF16), pltpu.VMEM((V_ROWS, GQA_GROUP * tq), F32),
                        pltpu.VMEM((V_ROWS, GQA_GROUP * tq), F32),
                        pltpu.VMEM((2, tq, GQA_GROUP * tq), F32), pltpu.VMEM((2, tq, GQA_GROUP * tq), BF16),
                        pltpu.VMEM((2, tq, GQA_GROUP * tq), BF16), pltpu.VMEM((nc, GQA_GROUP * tq), BF16),
                        pltpu.VMEM((nsel, tq), F32), pltpu.VMEM((nsel, tq), F32),
                        pltpu.SMEM((nkb + 1,), jnp.int32)],
        compiler_params=pltpu.CompilerParams(dimension_semantics=("arbitrary", "arbitrary", "arbitrary"),
                                             vmem_limit_bytes=VMEM_LIMIT),
    )(qt, gt, ks, kw, vs, vw, kc, vc, msel, cmask)


def _ffn_kernel(x_ref, oa_ref, mc_ref, ga_ref, wo_ref, gf_ref, wg_ref, wu_ref, cw_ref, cb_ref, wd_ref, gl_ref,
                o_ref, gbuf, ybuf, *, tiles_per_seq):
    i = pl.program_id(0)
    tm = x_ref.shape[0]

    @pl.when(i % tiles_per_seq == 0)
    def _():
        gbuf[0:8, :] = jnp.zeros((8, D_FF), F32)

    @pl.when(i % tiles_per_seq != 0)
    def _():
        gbuf[0:8, :] = gbuf[tm:tm + 8, :]

    subs = [slice(r0, r0 + SUB_ROWS) for r0 in range(0, tm, SUB_ROWS)]
    x1s, h2s = [], []
    for rows in subs:
        oat = oa_ref[:, rows]
        mat = (oat * lax.rsqrt(jnp.mean(oat * oat, axis=0, keepdims=True) + EPS) * ga_ref[...]).astype(BF16)
        x1 = (x_ref[rows, :] + lax.dot_general(mat, wo_ref[0:D_ATTN, :], (((0,), (0,)), ((), ())),
                                                preferred_element_type=F32)
              + jnp.dot(mc_ref[rows, :], wo_ref[D_ATTN:D_MODEL, :], preferred_element_type=F32))
        x1s.append(x1)
        h2s.append(_rms(x1, gf_ref[...]).astype(BF16))
    for rows, x1, h2 in zip(subs, x1s, h2s):
        r0 = rows.start
        for c in range(D_FF // FF_CHUNK):
            cs = slice(c * FF_CHUNK, (c + 1) * FF_CHUNK)
            gpre = jnp.dot(h2, wg_ref[:, cs], preferred_element_type=F32)
            up = jnp.dot(h2, wu_ref[:, cs], preferred_element_type=F32)
            gbuf[8 + r0:8 + r0 + SUB_ROWS, cs] = gpre
            gate = (cw_ref[0:1, cs] * gbuf[6 + r0:6 + r0 + SUB_ROWS, cs]
                    + cw_ref[1:2, cs] * gbuf[7 + r0:7 + r0 + SUB_ROWS, cs] + cw_ref[2:3, cs] * gpre + cb_ref[:, cs])
            ybuf[rows, cs] = (jax.nn.silu(gate) * up).astype(BF16)
        acc = x1 + jnp.dot(ybuf[rows, :], wd_ref[...], preferred_element_type=F32)
        o_ref[rows, :] = _rms(acc, gl_ref[...])


def _ffn_call(x2, oat, mc, ga, wo, gf, wg, wu, cw, cb, wd, gl):
    n = x2.shape[0]
    seq = oat.shape[2]
    tm = ROW_TILE
    tps = seq // tm
    row = lambda i: (i, 0)
    fix = lambda i: (0, 0)
    once = dict(pipeline_mode=pl.Buffered(1))
    return pl.pallas_call(
        functools.partial(_ffn_kernel, tiles_per_seq=seq // tm),
        grid=(n // tm,),
        in_specs=[
            pl.BlockSpec((tm, D_MODEL), row),
            pl.BlockSpec((None, D_ATTN, tm), lambda i: (i // tps, 0, i % tps)),
            pl.BlockSpec((tm, D_CONV), row),
            pl.BlockSpec((D_ATTN, 1), fix),
            pl.BlockSpec((D_MODEL, D_MODEL), fix, **once),
            pl.BlockSpec((1, D_MODEL), fix),
            pl.BlockSpec((D_MODEL, D_FF), fix, **once),
            pl.BlockSpec((D_MODEL, D_FF), fix, **once),
            pl.BlockSpec((3, D_FF), fix),
            pl.BlockSpec((1, D_FF), fix),
            pl.BlockSpec((D_FF, D_MODEL), fix, **once),
            pl.BlockSpec((1, D_MODEL), fix),
        ],
        out_specs=pl.BlockSpec((tm, D_MODEL), row),
        out_shape=jax.ShapeDtypeStruct((n, D_MODEL), F32),
        scratch_shapes=[pltpu.VMEM((tm + 8, D_FF), F32), pltpu.VMEM((tm, D_FF), BF16)],
        compiler_params=pltpu.CompilerParams(dimension_semantics=("arbitrary",), vmem_limit_bytes=VMEM_LIMIT),
    )(x2, oat, mc, ga, wo, gf, wg, wu, cw, cb, wd, gl)


def _alibi_cols(pos):
    cols = np.zeros((len(pos), AUG_ALIBI_ROWS), np.float32)
    cols[:, 0:3] = (pos % SEL_BLOCK)[:, None]
    cols[:, 3:6] = (pos // SEL_BLOCK)[:, None]
    cols[:, 6] = 1.0
    return cols


def _key_consts(t, with_sel):
    pos = np.arange(t)
    c = np.zeros((t, AUG_K), np.float32)
    c[:, AUG_ALIBI:AUG_SEL] = _alibi_cols(pos)
    if with_sel:
        c[pos, AUG_SEL + pos // SEL_BLOCK] = 1.0
    return jnp.asarray(c, BF16)


def _cmp_consts(nc):
    c = np.zeros((nc, AUG_K), np.float32)
    c[:, AUG_ALIBI:AUG_SEL] = _alibi_cols(np.arange(nc) * CMP_STRIDE + (CMP_BLOCK - 1))
    return jnp.asarray(c, BF16)


def _compress_weights(w_k1, w_v1, pos_k, pos_v, w_k2, w_v2):
    hh, dk, half = N_KV_HEADS, HEAD_DIM, CMP_STRIDE
    w1 = jnp.zeros((half, N_STREAMS, dk, N_STREAMS, 2, CMP_HIDDEN), F32)
    pos = []
    for st in range(N_STREAMS):
        w = (w_k1 if st < hh else w_v1).reshape(2, half, dk, CMP_HIDDEN)
        w1 = w1.at[:, st, :, st, :, :].set(w.transpose(1, 2, 0, 3))
        pos.append((pos_k if st < hh else pos_v).reshape(2, half, dk))
    pos = jnp.stack(pos, axis=2).reshape(2, half * N_STREAMS * dk)
    pos8 = jnp.concatenate([pos, jnp.zeros((6, pos.shape[1]), F32)], axis=0)
    w2 = jnp.stack([jnp.concatenate([w_k2 if st < hh else w_v2, jnp.zeros((CMP_HIDDEN, 128 - dk), F32)], axis=1)
                    for st in range(N_STREAMS)])
    w1 = w1.reshape(half * N_STREAMS * dk, N_STREAMS * 2 * CMP_HIDDEN)
    return w1.astype(BF16), pos8.astype(BF16), w2.astype(BF16)


def _cmp_limits(nc):
    end = np.arange(nc)[:, None] * CMP_STRIDE + (CMP_BLOCK - 1)
    off = np.arange(Q_TILE)[None, :]
    return jnp.asarray(end - off, jnp.int32)


def _sel_map_t(t, nc):
    n_cmp = (t - CMP_BLOCK) // CMP_STRIDE + 1
    n_sel = t // SEL_BLOCK
    cs = np.arange(n_cmp)[:, None] * CMP_STRIDE
    ss = np.arange(n_sel)[None, :] * SEL_BLOCK
    ov = np.maximum(0, np.minimum(cs + CMP_BLOCK, ss + SEL_BLOCK) - np.maximum(cs, ss)) / CMP_BLOCK
    m = np.zeros((n_sel, nc), np.float32)
    m[:, :n_cmp] = ov.T
    return jnp.asarray(m, BF16)


def kernel(x, norm_mix_g, w_in, pos_ck, w_ck1, w_ck2, pos_cv, w_cv1, w_cv2, conv_mix_w, norm_out_attn_g,
           norm_out_conv_g, w_out, norm_ffn_g, w_gate, w_up, ffn_conv_w, ffn_conv_b, w_down, norm_final_g):
    b, t, _ = x.shape
    hh, dk = N_KV_HEADS, HEAD_DIM
    assert t % ROW_TILE == 0 and t % Q_TILE == 0 and WINDOW % Q_TILE == 0 and t // SEL_BLOCK <= AUG_K - AUG_SEL
    assert ROW_TILE % SUB_ROWS == 0 and SUB_ROWS % Q_TILE == 0 and (t // SEL_BLOCK) % 16 == 0
    nc = t // CMP_STRIDE
    nsel = t // SEL_BLOCK
    nkb = t // Q_TILE
    depth = w_in.shape[0]
    assert depth == 1
    xx = x
    for l in range(depth):
        wi = w_in[l]
        col = np.cumsum([0, D_ATTN] + [D_KV] * 6 + [3 * N_HEADS_ATTN] + [D_CONV] * 3)
        kv_slabs = [wi[:, col[3 + 2 * br] + hd * dk:col[3 + 2 * br] + (hd + 1) * dk] if part == 0 else
                    wi[:, col[4 + 2 * br] + hd * dk:col[4 + 2 * br] + (hd + 1) * dk]
                    for br in range(2) for hd in range(hh) for part in range(2)]
        gate_cols = [jnp.concatenate([wi[:, col[7] + 12 * hd:col[7] + 12 * (hd + 1)], jnp.zeros((D_MODEL, 4), F32)], axis=1)
                     for hd in range(hh)]
        w_p = jnp.concatenate([wi[:, 0:D_ATTN]] + kv_slabs + [wi[:, col[1]:col[3]], wi[:, col[8]:col[11]]] + gate_cols
                              + [jnp.zeros((D_MODEL, 128 - 2 * GATE_ROWS), F32)], axis=1).astype(BF16)
        assert w_p.shape[1] == PROJ_COLS
        qt, gt, ksa, kwa, vst, vwt, kvc, mixed_conv = _proj_call(
            xx, norm_mix_g[l][None], w_p, conv_mix_w[l], norm_out_conv_g[l][None],
            _key_consts(t, True), _key_consts(t, False))

        kc, vc = _compress_call(kvc, *_compress_weights(w_ck1[l], w_cv1[l], pos_ck[l], pos_cv[l], w_ck2[l], w_cv2[l]),
                                _cmp_consts(nc))

        o_attn = _attn_call(qt, gt, ksa.reshape(b, hh, nkb, Q_TILE, AUG_K), kwa.reshape(b, hh, nkb, Q_TILE, AUG_K),
                            vst, vwt, kc, vc, _sel_map_t(t, nc), _cmp_limits(nc), min(SEL_TOPK, nsel))

        xx = _ffn_call(xx.reshape(b * t, D_MODEL), o_attn, mixed_conv.reshape(b * t, D_CONV),
                       norm_out_attn_g[l][:, None], w_out[l].astype(BF16), norm_ffn_g[l][None], w_gate[l].astype(BF16),
                       w_up[l].astype(BF16), ffn_conv_w[l], ffn_conv_b[l][None], w_down[l].astype(BF16),
                       norm_final_g[None])
    return xx.reshape(b, t, D_MODEL)
```

```python
import functools

import jax
import jax.numpy as jnp
import numpy as np
from jax import lax
from jax.experimental import pallas as pl
from jax.experimental.pallas import tpu as pltpu

F32 = jnp.float32
BF16 = jnp.bfloat16

D_MODEL = 1024
N_KV_HEADS = 2
GQA_GROUP = 4
N_HEADS_ATTN = N_KV_HEADS * GQA_GROUP
HEAD_DIM = 64
D_ATTN = N_HEADS_ATTN * HEAD_DIM
D_KV = N_KV_HEADS * HEAD_DIM
D_CONV = D_MODEL - D_ATTN
CMP_BLOCK = 32
CMP_STRIDE = 16
CMP_HIDDEN = 2 * HEAD_DIM
SEL_BLOCK = 64
SEL_TOPK = 16
WINDOW = 512
D_FF = 2816
EPS = 1e-6
NEG = -1e30
BIG = 1e30

AUG_K = 256
AUG_ALIBI = HEAD_DIM
AUG_ALIBI_ROWS = 16
AUG_SEL = AUG_ALIBI + AUG_ALIBI_ROWS
V_ROWS = 80

ROW_TILE = 1024
SUB_ROWS = 256
Q_TILE = 256
SEL_UNROLL = 4
FF_CHUNK = 256
PROJ_COLS = 2944
VMEM_LIMIT = 56 * 1024 * 1024

LOG2E = 1.4426950408889634


def _bf16_terms(x, n):
    out = []
    for _ in range(n):
        t = float(np.asarray(x, np.float32).astype(jnp.bfloat16).astype(np.float32))
        out.append(t)
        x = x - t
    return tuple(out)


LOG2E_3 = _bf16_terms(LOG2E, 3)


def _rms(x, g):
    return x * lax.rsqrt(jnp.mean(x * x, axis=-1, keepdims=True) + EPS) * g


COL_Q = 0
COL_KV = D_ATTN
COL_CMP = COL_KV + 4 * 2 * HEAD_DIM
COL_B = COL_CMP + 2 * D_KV
COL_C = COL_B + D_CONV
COL_U = COL_C + D_CONV
COL_GATE = COL_U + D_CONV
GATE_ROWS = 16


def _proj_kernel(x_ref, g_ref, w_ref, cw_ref, gc_ref, csel_ref, cwin_ref,
                 qt_ref, gt_ref, ksa_ref, kwa_ref, vst_ref, vwt_ref, kvc_ref, mc_ref, cbuf, cst, *, tiles_per_seq):
    i = pl.program_id(0)
    tm = x_ref.shape[0]
    tq = vst_ref.shape[-1]

    @pl.when(i % tiles_per_seq == 0)
    def _():
        cbuf[0:8, :] = jnp.zeros((8, D_CONV), F32)

    @pl.when(i % tiles_per_seq != 0)
    def _():
        cbuf[0:8, :] = cbuf[SUB_ROWS:SUB_ROWS + 8, :]

    lane = lax.broadcasted_iota(jnp.int32, (SUB_ROWS, 128), 1)
    ones_rows = jnp.where(lax.broadcasted_iota(jnp.int32, (V_ROWS - HEAD_DIM, tq), 0) == 0, 1.0, 0.0).astype(BF16)
    for sub in range(tm // SUB_ROWS):
        r0 = sub * SUB_ROWS
        rows = slice(r0, r0 + SUB_ROWS)
        if sub > 0:
            cbuf[0:8, :] = cbuf[SUB_ROWS:SUB_ROWS + 8, :]
        h = _rms(x_ref[rows, :], g_ref[...])
        p = jnp.dot(h.astype(BF16), w_ref[...], preferred_element_type=F32)

        for hd in range(N_KV_HEADS):
            qs = p[:, COL_Q + hd * 256:COL_Q + (hd + 1) * 256] * (HEAD_DIM ** -0.5 * LOG2E)
            qt_ref[hd, :, rows] = qs.T.astype(BF16)
        gt = jax.nn.sigmoid(p[:, COL_GATE:COL_GATE + 128]).T
        for hd in range(N_KV_HEADS):
            gt_ref[hd, :, rows] = gt[hd * GATE_ROWS:(hd + 1) * GATE_ROWS, :]

        for branch, (ka_ref, vt_ref, c_ref) in enumerate(((ksa_ref, vst_ref, csel_ref), (kwa_ref, vwt_ref, cwin_ref))):
            for hd in range(N_KV_HEADS):
                c0 = COL_KV + (2 * branch + hd) * 128
                slab = p[:, c0:c0 + 128]
                ka_ref[hd, rows, 0:128] = jnp.where(lane < HEAD_DIM, slab.astype(BF16), c_ref[rows, 0:128])
                ka_ref[hd, rows, 128:AUG_K] = c_ref[rows, 128:AUG_K]
                vt = slab.T[HEAD_DIM:128, :].astype(BF16)
                for kt in range(SUB_ROWS // tq):
                    vt_ref[hd, r0 // tq + kt, 0:HEAD_DIM, :] = vt[:, kt * tq:(kt + 1) * tq]
                    vt_ref[hd, r0 // tq + kt, HEAD_DIM:V_ROWS, :] = ones_rows

        nch = SUB_ROWS // CMP_STRIDE
        for part in range(2):
            cst[part] = p[:, COL_CMP + part * D_KV:COL_CMP + (part + 1) * D_KV]
            for tok in range(CMP_STRIDE):
                c0 = (2 * tok + part) * D_KV
                kvc_ref[r0 // CMP_STRIDE:r0 // CMP_STRIDE + nch, c0:c0 + D_KV] = (
                    cst[part, pl.ds(tok, nch, stride=CMP_STRIDE), :].astype(BF16))
        b = p[:, COL_B:COL_B + D_CONV]
        cu = p[:, COL_C:COL_C + D_CONV] * p[:, COL_U:COL_U + D_CONV]
        cbuf[8:8 + SUB_ROWS, :] = cu
        y = cw_ref[0:1, :] * cbuf[6:6 + SUB_ROWS, :] + cw_ref[1:2, :] * cbuf[7:7 + SUB_ROWS, :] + cw_ref[2:3, :] * cu
        mc_ref[rows, :] = _rms(b * y, gc_ref[...]).astype(BF16)


def _proj_call(x3, g, w, cw, gc, csel, cwin):
    bsz, seq, _ = x3.shape
    tm = ROW_TILE
    tq = Q_TILE
    tps = seq // tm
    hh = N_KV_HEADS
    row = lambda i: (i // tps, i % tps, 0)
    fix = lambda i: (0, 0)
    seq_tile = lambda i: (i % tps, 0)
    tok_minor = lambda i: (i // tps, 0, 0, i % tps)
    tok_major = lambda i: (i // tps, 0, i % tps, 0)
    tok_tiles = lambda i: (i // tps, 0, i % tps, 0, 0)
    return pl.pallas_call(
        functools.partial(_proj_kernel, tiles_per_seq=tps),
        grid=(bsz * tps,),
        in_specs=[
            pl.BlockSpec((None, tm, D_MODEL), row),
            pl.BlockSpec((1, D_MODEL), fix),
            pl.BlockSpec((D_MODEL, PROJ_COLS), fix),
            pl.BlockSpec((3, D_CONV), fix),
            pl.BlockSpec((1, D_CONV), fix),
            pl.BlockSpec((tm, AUG_K), seq_tile),
            pl.BlockSpec((tm, AUG_K), seq_tile),
        ],
        out_specs=[
            pl.BlockSpec((None, hh, GQA_GROUP * HEAD_DIM, tm), tok_minor),
            pl.BlockSpec((None, hh, GATE_ROWS, tm), tok_minor),
            pl.BlockSpec((None, hh, tm, AUG_K), tok_major),
            pl.BlockSpec((None, hh, tm, AUG_K), tok_major),
            pl.BlockSpec((None, hh, tm // tq, V_ROWS, tq), tok_tiles),
            pl.BlockSpec((None, hh, tm // tq, V_ROWS, tq), tok_tiles),
            pl.BlockSpec((None, tm // CMP_STRIDE, CMP_STRIDE * 2 * D_KV), row),
            pl.BlockSpec((None, tm, D_CONV), row),
        ],
        out_shape=[
            jax.ShapeDtypeStruct((bsz, hh, GQA_GROUP * HEAD_DIM, seq), BF16),
            jax.ShapeDtypeStruct((bsz, hh, GATE_ROWS, seq), F32),
            jax.ShapeDtypeStruct((bsz, hh, seq, AUG_K), BF16),
            jax.ShapeDtypeStruct((bsz, hh, seq, AUG_K), BF16),
            jax.ShapeDtypeStruct((bsz, hh, seq // tq, V_ROWS, tq), BF16),
            jax.ShapeDtypeStruct((bsz, hh, seq // tq, V_ROWS, tq), BF16),
            jax.ShapeDtypeStruct((bsz, seq // CMP_STRIDE, CMP_STRIDE * 2 * D_KV), BF16),
            jax.ShapeDtypeStruct((bsz, seq, D_CONV), BF16),
        ],
        scratch_shapes=[pltpu.VMEM((SUB_ROWS + 8, D_CONV), F32), pltpu.VMEM((2, SUB_ROWS, D_KV), F32)],
        compiler_params=pltpu.CompilerParams(dimension_semantics=("arbitrary",), vmem_limit_bytes=VMEM_LIMIT),
    )(x3, g, w, cw, gc, csel, cwin)


N_STREAMS = 2 * N_KV_HEADS


def _compress_kernel(x_ref, w1_ref, pos_ref, w2_ref, c_ref, kc_ref, vc_ref, sbuf):
    nc = x_ref.shape[0]
    y = jnp.dot(x_ref[...], w1_ref[...], preferred_element_type=F32)
    pb = jnp.dot(pos_ref[...], w1_ref[...], preferred_element_type=F32)
    rowi = lax.broadcasted_iota(jnp.int32, (nc, 128), 0)
    lane = lax.broadcasted_iota(jnp.int32, (nc, 128), 1)
    sbuf[:, nc:nc + 8, :] = jnp.zeros((N_STREAMS, 8, CMP_HIDDEN), F32)
    for st in range(N_STREAMS):
        c0 = st * 2 * CMP_HIDDEN
        posb = pb[0:1, c0:c0 + CMP_HIDDEN] + pb[1:2, c0 + CMP_HIDDEN:c0 + 2 * CMP_HIDDEN]
        sbuf[st, 0:nc, :] = y[:, c0 + CMP_HIDDEN:c0 + 2 * CMP_HIDDEN]
        hid = y[:, c0:c0 + CMP_HIDDEN] + sbuf[st, 1:nc + 1, :] + posb
        out = jnp.dot(jax.nn.gelu(hid).astype(BF16), w2_ref[st], preferred_element_type=F32)
        out = jnp.where(rowi < nc - 1, out, 0.0)
        if st < N_KV_HEADS:
            kc_ref[st, :, 0:128] = jnp.where(lane < HEAD_DIM, out.astype(BF16), c_ref[:, 0:128])
            kc_ref[st, :, 128:AUG_K] = c_ref[:, 128:AUG_K]
        else:
            vc_ref[st - N_KV_HEADS] = out.T[0:HEAD_DIM, :].astype(BF16)


def _compress_call(xc, w1, pos8, w2, consts):
    b, nc, width = xc.shape
    hh = N_KV_HEADS
    fix2 = lambda i: (0, 0)
    return pl.pallas_call(
        _compress_kernel,
        grid=(b,),
        in_specs=[
            pl.BlockSpec((None, nc, width), lambda i: (i, 0, 0)),
            pl.BlockSpec((width, N_STREAMS * 2 * CMP_HIDDEN), fix2),
            pl.BlockSpec((8, width), fix2),
            pl.BlockSpec((N_STREAMS, CMP_HIDDEN, 128), lambda i: (0, 0, 0)),
            pl.BlockSpec((nc, AUG_K), fix2),
        ],
        out_specs=[
            pl.BlockSpec((None, hh, nc, AUG_K), lambda i: (i, 0, 0, 0)),
            pl.BlockSpec((None, hh, HEAD_DIM, nc), lambda i: (i, 0, 0, 0)),
        ],
        out_shape=[
            jax.ShapeDtypeStruct((b, hh, nc, AUG_K), BF16),
            jax.ShapeDtypeStruct((b, hh, HEAD_DIM, nc), BF16),
        ],
        scratch_shapes=[pltpu.VMEM((N_STREAMS, nc + 8, CMP_HIDDEN), F32)],
        compiler_params=pltpu.CompilerParams(dimension_semantics=("arbitrary",), vmem_limit_bytes=VMEM_LIMIT),
    )(xc, w1, pos8, w2, consts)


def _attn_kernel(qt_ref, gt_ref, ks_ref, kw_ref, vs_ref, vw_ref, kc_ref, vc_ref, msel_ref, cmask_ref, o_ref,
                 qaug, acc_ref, accw_ref, sbuf, pbuf, pwbuf, pcbuf, imp_ref, selm_ref, klist, *, topk):
    h = pl.program_id(1)
    qi = pl.program_id(2)
    tq = qt_ref.shape[1]
    r = GQA_GROUP * tq
    nc = kc_ref.shape[0]
    nsel = msel_ref.shape[0]
    nkb = ks_ref.shape[0]
    t0 = qi * tq

    lane16 = lax.broadcasted_iota(jnp.int32, (AUG_ALIBI_ROWS, r), 1)
    sub16 = lax.broadcasted_iota(jnp.int32, (AUG_ALIBI_ROWS, r), 0)
    gl = lane16 // tq
    off = (lane16 % tq).astype(F32)
    base = jnp.where(h == 0, 0.5, 0.03125).astype(F32)
    slope = jnp.where(gl == 0, base, jnp.where(gl == 1, base * 0.5, jnp.where(gl == 2, base * 0.25, base * 0.125)))
    blk0 = (t0 // SEL_BLOCK).astype(F32)
    c3 = jnp.where(sub16 % 3 == 0, LOG2E_3[0], jnp.where(sub16 % 3 == 1, LOG2E_3[1], LOG2E_3[2]))
    arow = jnp.where(sub16 < 3, slope * c3,
                     jnp.where(sub16 < 6, 64.0 * slope * c3,
                               jnp.where(sub16 == 6, -slope * LOG2E * (64.0 * blk0 + off), 0.0)))
    for g in range(GQA_GROUP):
        qaug[0:HEAD_DIM, g * tq:(g + 1) * tq] = qt_ref[g * HEAD_DIM:(g + 1) * HEAD_DIM, :]
    qaug[AUG_ALIBI:AUG_SEL, :] = arow.astype(BF16)
    qaug[AUG_SEL:AUG_K, :] = jnp.zeros((AUG_K - AUG_SEL, r), BF16)

    heads = [slice(g * tq, (g + 1) * tq) for g in range(GQA_GROUP)]
    q_i = lax.broadcasted_iota(jnp.int32, (tq, tq), 1)
    key_i = lax.broadcasted_iota(jnp.int32, (tq, tq), 0)
    causal = key_i <= q_i
    nwin = WINDOW // tq

    def win_scores(w, cols):
        kb = qi - nwin + w
        s = jnp.dot(kw_ref[jnp.maximum(kb, 0)], qaug[:, cols], preferred_element_type=F32)
        if w == 0:
            return jnp.where((key_i > q_i) & (kb >= 0), s, NEG)
        if w == nwin:
            return jnp.where(causal, s, NEG)
        return jnp.where(kb >= 0, s, NEG)

    def softmax_tile(s, m_old):
        m_new = jnp.maximum(m_old, jnp.max(s, axis=0, keepdims=True))
        return m_new, jnp.exp2(s - m_new).astype(BF16), jnp.exp2(m_old - m_new)

    def win_pv(w, cols, alpha):
        pv_w = jnp.dot(vw_ref[jnp.maximum(qi - nwin + w, 0)], pwbuf[w % 2, :, cols], preferred_element_type=F32)
        accw_ref[:, cols] = pv_w if w == 0 else accw_ref[:, cols] * alpha + pv_w

    mask_c = cmask_ref[...] <= t0
    cmp_s = [jnp.dot(kc_ref[...], qaug[:, cols], preferred_element_type=F32) for cols in heads]
    win_s = [win_scores(0, cols) for cols in heads]
    p_sum = jnp.zeros((nc, tq), F32)
    for g, cols in enumerate(heads):
        sc = jnp.where(mask_c, cmp_s[g], NEG)
        m_c = jnp.max(sc, axis=0, keepdims=True)
        e_c = jnp.exp2(sc - m_c)
        l_c = jnp.sum(e_c, axis=0, keepdims=True)
        p_c = e_c * jnp.where(m_c > 0.5 * NEG, 1.0 / l_c, 0.0)
        pcbuf[:, cols] = p_c.astype(BF16)
        p_sum = p_sum + p_c

    m_w = [jnp.full((1, tq), NEG, F32)] * GQA_GROUP
    a_w = [None] * GQA_GROUP
    o_cmp = []
    for w in range(nwin):
        nxt = []
        for g, cols in enumerate(heads):
            nxt.append(win_scores(w + 1, cols))
            if w == 0:
                o_cmp.append(jnp.dot(vc_ref[...], pcbuf[:, cols], preferred_element_type=F32))
            else:
                win_pv(w - 1, cols, a_w[g])
            m_w[g], p, a_w[g] = softmax_tile(win_s[g], m_w[g])
            pwbuf[w % 2, :, cols] = p
        win_s = nxt
        if w == 0:
            p1 = p_sum.astype(BF16)
            p2 = (p_sum - p1.astype(F32)).astype(BF16)
            msel = msel_ref[...]
            imp = (jnp.dot(msel, p1, preferred_element_type=F32)
                   + jnp.dot(msel, p2, preferred_element_type=F32))

    jj =lax.broadcasted_iota(jnp.int32, (nsel, tq), 0)
    jt = (t0 + lax.broadcasted_iota(jnp.int32, (nsel, tq), 1)) // SEL_BLOCK
    imp = jnp.where((jj == 0) | (jj == jt) | (jj == jt - 1), BIG, imp)
    imp = jnp.where(jj > jt, NEG, imp)
    imp_ref[...] = imp
    bpt = tq // SEL_BLOCK
    sub8 = lax.broadcasted_iota(jnp.int32, (8, tq), 0)

    def rank_select(nblk):
        groups = [imp_ref[8 * gi:8 * gi + 8, :] for gi in range(nblk // 8)]
        cnts = [jnp.zeros((8, tq), jnp.int32) for _ in groups]
        for jp in range(nblk):
            rowv = jnp.broadcast_to(imp_ref[jp:jp + 1, :], (8, tq))
            for gi, grp in enumerate(groups):
                if 8 * gi > jp:
                    beats = rowv >= grp
                elif 8 * gi + 7 <= jp:
                    beats = rowv > grp
                else:
                    beats = (rowv > grp) | ((rowv == grp) & (sub8 + 8 * gi > jp))
                cnts[gi] = cnts[gi] + jnp.where(beats, 1, 0)
        selm = [jnp.where(c < topk, 1.0, 0.0) for c in cnts] + [jnp.zeros((nsel - nblk, tq), F32)] * (nblk < nsel)
        selm = jnp.concatenate(selm, axis=0)
        selm_ref[...] = selm
        selbias = jnp.where(selm > 0.0, 0.0, NEG).astype(BF16)
        for g in range(GQA_GROUP):
            qaug[AUG_SEL:AUG_SEL + nsel, g * tq:(g + 1) * tq] = selbias

    for idx in range(nsel // 16):
        pl.when((qi * bpt) // 16 == idx)(functools.partial(rank_select, 16 * (idx + 1)))

    a_last = [None] * GQA_GROUP
    for g, cols in enumerate(heads):
        sbuf[0, :, cols] = jnp.dot(ks_ref[0], qaug[:, cols], preferred_element_type=F32)
        win_pv(nwin - 1, cols, a_w[g])
        _, p, a_last[g] = softmax_tile(win_s[g], m_w[g])
        pwbuf[nwin % 2, :, cols] = p

    n_use = jnp.int32(0)
    for gi in range(nsel // 8):
        hit = jnp.max(selm_ref[8 * gi:8 * gi + 8, :], axis=1, keepdims=True)
        for part in range(8 // bpt):
            kb = gi * (8 // bpt) + part
            if kb < nkb - 1:
                klist[n_use] = kb
                used = (jnp.max(hit[part * bpt:(part + 1) * bpt, :]) > 0.0) & (kb < qi)
                n_use = n_use + used.astype(jnp.int32)
    klist[n_use] = qi

    acc_ref[...] = jnp.zeros(acc_ref.shape, F32)
    pbuf[1] = jnp.zeros((tq, r), BF16)

    def step(j, cur, carry):
        m_old, alpha_prev = carry
        k_next = ks_ref[klist[j + 1]]
        v_prev = vs_ref[klist[jnp.maximum(j - 1, 0)]]
        m_news, alphas = [], []
        for g in range(GQA_GROUP):
            cols = slice(g * tq, (g + 1) * tq)
            sbuf[1 - cur, :, cols] = jnp.dot(k_next, qaug[:, cols], preferred_element_type=F32)
            acc_ref[:, cols] = (acc_ref[:, cols] * alpha_prev[:, cols]
                                + jnp.dot(v_prev, pbuf[1 - cur, :, cols], preferred_element_type=F32))
            m_new, p, alpha = softmax_tile(sbuf[cur, :, cols], m_old[:, cols])
            pbuf[cur, :, cols] = p
            m_news.append(m_new)
            alphas.append(alpha)
        return jnp.concatenate(m_news, axis=1), jnp.concatenate(alphas, axis=1)

    def finish(cur, carry):
        m_old, alpha_prev = carry
        v_prev = vs_ref[klist[jnp.maximum(n_use - 1, 0)]]
        v_last = vs_ref[qi]
        gt = gt_ref[...]
        for g, cols in enumerate(heads):
            _, p_last, alpha_last = softmax_tile(jnp.where(causal, sbuf[cur, :, cols], NEG), m_old[:, cols])
            acc = (acc_ref[:, cols] * alpha_prev[:, cols]
                   + jnp.dot(v_prev, pbuf[1 - cur, :, cols], preferred_element_type=F32))
            acc = acc * alpha_last + jnp.dot(v_last, p_last, preferred_element_type=F32)
            win_pv(nwin, cols, a_last[g])
            acc_w = accw_ref[:, cols]
            o_sel = acc[0:HEAD_DIM, :] * (1.0 / acc[HEAD_DIM:HEAD_DIM + 1, :])
            o_win = acc_w[0:HEAD_DIM, :] * (1.0 / acc_w[HEAD_DIM:HEAD_DIM + 1, :])
            o_ref[g * HEAD_DIM:(g + 1) * HEAD_DIM, :] = (gt[3 * g:3 * g + 1, :] * o_cmp[g] + gt[3 * g + 1:3 * g + 2, :] * o_sel
                                                         + gt[3 * g + 2:3 * g + 3, :] * o_win)

    def unrolled(i, carry):
        for u in range(SEL_UNROLL):
            carry = step(SEL_UNROLL * i + u, u % 2, carry)
        return carry

    carry0 = (jnp.full((1, r), NEG, F32), jnp.ones((1, r), F32))
    n_main = n_use // SEL_UNROLL
    carry_main = lax.fori_loop(0, n_main, unrolled, carry0)
    for rem in range(SEL_UNROLL):
        @pl.when(n_use % SEL_UNROLL == rem)
        def _(rem=rem):
            carry = carry_main
            for u in range(rem):
                carry = step(SEL_UNROLL * n_main + u, u % 2, carry)
            finish(rem % 2, carry)


def _attn_call(qt, gt, ks, kw, vs, vw, kc, vc, msel, cmask, topk):
    b, hh, _, t = qt.shape
    tq = Q_TILE
    nkb = t // tq
    nc = kc.shape[2]
    nsel = msel.shape[0]
    per_q = lambda i, j, k: (i, j, 0, k)
    per_bh4 = lambda i, j, k: (i, j, 0, 0)
    per_bh5 = lambda i, j, k: (i, j, 0, 0, 0)
    return pl.pallas_call(
        functools.partial(_attn_kernel, topk=topk),
        grid=(b, hh, nkb),
        in_specs=[
            pl.BlockSpec((None, None, GQA_GROUP * HEAD_DIM, tq), per_q),
            pl.BlockSpec((None, None, GATE_ROWS, tq), per_q),
            pl.BlockSpec((None, None, nkb, tq, AUG_K), per_bh5),
            pl.BlockSpec((None, None, nkb, tq, AUG_K), per_bh5),
            pl.BlockSpec((None, None, nkb, V_ROWS, tq), per_bh5),
            pl.BlockSpec((None, None, nkb, V_ROWS, tq), per_bh5),
            pl.BlockSpec((None, None, nc, AUG_K), per_bh4),
            pl.BlockSpec((None, None, HEAD_DIM, nc), per_bh4),
            pl.BlockSpec((nsel, nc), lambda i, j, k: (0, 0)),
            pl.BlockSpec((nc, tq), lambda i, j, k: (0, 0)),
        ],
        out_specs=pl.BlockSpec((None, GQA_GROUP * HEAD_DIM, tq), lambda i, j, k: (i, j, k)),
        out_shape=jax.ShapeDtypeStruct((b, D_ATTN, t), F32),
        scratch_shapes=[pltpu.VMEM((AUG_K, GQA_GROUP * tq), BF16), pltpu.VMEM((V_ROWS, GQA_GROUP * tq), F32),
                        pltpu.VMEM((V_ROWS, GQA_GROUP * tq), F32),
                        pltpu.VMEM((2, tq, GQA_GROUP * tq), F32), pltpu.VMEM((2, tq, GQA_GROUP * tq), BF16),
                        pltpu.VMEM((2, tq, GQA_GROUP * tq), BF16), pltpu.VMEM((nc, GQA_GROUP * tq), BF16),
                        pltpu.VMEM((nsel, tq), F32), pltpu.VMEM((nsel, tq), F32),
                        pltpu.SMEM((nkb + 1,), jnp.int32)],
        compiler_params=pltpu.CompilerParams(dimension_semantics=("arbitrary", "arbitrary", "arbitrary"),
                                             vmem_limit_bytes=VMEM_LIMIT),
    )(qt, gt, ks, kw, vs, vw, kc, vc, msel, cmask)


def _ffn_kernel(x_ref, oa_ref, mc_ref, ga_ref, wo_ref, gf_ref, wg_ref, wu_ref, cw_ref, cb_ref, wd_ref, gl_ref,
                o_ref, gbuf, ybuf, *, tiles_per_seq):
    i = pl.program_id(0)
    tm = x_ref.shape[0]

    @pl.when(i % tiles_per_seq == 0)
    def _():
        gbuf[0:8, :] = jnp.zeros((8, D_FF), F32)

    @pl.when(i % tiles_per_seq != 0)
    def _():
        gbuf[0:8, :] = gbuf[SUB_ROWS:SUB_ROWS + 8, :]

    subs = [slice(r0, r0 + SUB_ROWS) for r0 in range(0, tm, SUB_ROWS)]
    x1s, h2s = [], []
    for rows in subs:
        oat = oa_ref[:, rows]
        mat = (oat * lax.rsqrt(jnp.mean(oat * oat, axis=0, keepdims=True) + EPS) * ga_ref[...]).astype(BF16)
        x1 = (x_ref[rows, :] + lax.dot_general(mat, wo_ref[0:D_ATTN, :], (((0,), (0,)), ((), ())),
                                                preferred_element_type=F32)
              + jnp.dot(mc_ref[rows, :], wo_ref[D_ATTN:D_MODEL, :], preferred_element_type=F32))
        x1s.append(x1)
        h2s.append(_rms(x1, gf_ref[...]).astype(BF16))
    for rows, x1, h2 in zip(subs, x1s, h2s):
        if rows.start > 0:
            gbuf[0:8, :] = gbuf[SUB_ROWS:SUB_ROWS + 8, :]
        for c in range(D_FF // FF_CHUNK):
            cs = slice(c * FF_CHUNK, (c + 1) * FF_CHUNK)
            gpre = jnp.dot(h2, wg_ref[:, cs], preferred_element_type=F32)
            up = jnp.dot(h2, wu_ref[:, cs], preferred_element_type=F32)
            gbuf[8:8 + SUB_ROWS, cs] = gpre
            gate = (cw_ref[0:1, cs] * gbuf[6:6 + SUB_ROWS, cs] + cw_ref[1:2, cs] * gbuf[7:7 + SUB_ROWS, cs]
                    + cw_ref[2:3, cs] * gpre + cb_ref[:, cs])
            ybuf[:, cs] = (jax.nn.silu(gate) * up).astype(BF16)
        acc = x1 + jnp.dot(ybuf[...], wd_ref[...], preferred_element_type=F32)
        o_ref[rows, :] = _rms(acc, gl_ref[...])


def _ffn_call(x2, oat, mc, ga, wo, gf, wg, wu, cw, cb, wd, gl):
    n = x2.shape[0]
    seq = oat.shape[2]
    tm = ROW_TILE
    tps = seq // tm
    row = lambda i: (i, 0)
    fix = lambda i: (0, 0)
    once = dict(pipeline_mode=pl.Buffered(1))
    return pl.pallas_call(
        functools.partial(_ffn_kernel, tiles_per_seq=seq // tm),
        grid=(n // tm,),
        in_specs=[
            pl.BlockSpec((tm, D_MODEL), row),
            pl.BlockSpec((None, D_ATTN, tm), lambda i: (i // tps, 0, i % tps)),
            pl.BlockSpec((tm, D_CONV), row),
            pl.BlockSpec((D_ATTN, 1), fix),
            pl.BlockSpec((D_MODEL, D_MODEL), fix, **once),
            pl.BlockSpec((1, D_MODEL), fix),
            pl.BlockSpec((D_MODEL, D_FF), fix, **once),
            pl.BlockSpec((D_MODEL, D_FF), fix, **once),
            pl.BlockSpec((3, D_FF), fix),
            pl.BlockSpec((1, D_FF), fix),
            pl.BlockSpec((D_FF, D_MODEL), fix, **once),
            pl.BlockSpec((1, D_MODEL), fix),
        ],
        out_specs=pl.BlockSpec((tm, D_MODEL), row),
        out_shape=jax.ShapeDtypeStruct((n, D_MODEL), F32),
        scratch_shapes=[pltpu.VMEM((SUB_ROWS + 8, D_FF), F32), pltpu.VMEM((SUB_ROWS, D_FF), BF16)],
        compiler_params=pltpu.CompilerParams(dimension_semantics=("arbitrary",), vmem_limit_bytes=VMEM_LIMIT),
    )(x2, oat, mc, ga, wo, gf, wg, wu, cw, cb, wd, gl)


def _alibi_cols(pos):
    cols = np.zeros((len(pos), AUG_ALIBI_ROWS), np.float32)
    cols[:, 0:3] = (pos % SEL_BLOCK)[:, None]
    cols[:, 3:6] = (pos // SEL_BLOCK)[:, None]
    cols[:, 6] = 1.0
    return cols


def _key_consts(t, with_sel):
    pos = np.arange(t)
    c = np.zeros((t, AUG_K), np.float32)
    c[:, AUG_ALIBI:AUG_SEL] = _alibi_cols(pos)
    if with_sel:
        c[pos, AUG_SEL + pos // SEL_BLOCK] = 1.0
    return jnp.asarray(c, BF16)


def _cmp_consts(nc):
    c = np.zeros((nc, AUG_K), np.float32)
    c[:, AUG_ALIBI:AUG_SEL] = _alibi_cols(np.arange(nc) * CMP_STRIDE + (CMP_BLOCK - 1))
    return jnp.asarray(c, BF16)


def _compress_weights(w_k1, w_v1, pos_k, pos_v, w_k2, w_v2):
    hh, dk, half = N_KV_HEADS, HEAD_DIM, CMP_STRIDE
    w1 = jnp.zeros((half, N_STREAMS, dk, N_STREAMS, 2, CMP_HIDDEN), F32)
    pos = []
    for st in range(N_STREAMS):
        w = (w_k1 if st < hh else w_v1).reshape(2, half, dk, CMP_HIDDEN)
        w1 = w1.at[:, st, :, st, :, :].set(w.transpose(1, 2, 0, 3))
        pos.append((pos_k if st < hh else pos_v).reshape(2, half, dk))
    pos = jnp.stack(pos, axis=2).reshape(2, half * N_STREAMS * dk)
    pos8 = jnp.concatenate([pos, jnp.zeros((6, pos.shape[1]), F32)], axis=0)
    w2 = jnp.stack([jnp.concatenate([w_k2 if st < hh else w_v2, jnp.zeros((CMP_HIDDEN, 128 - dk), F32)], axis=1)
                    for st in range(N_STREAMS)])
    w1 = w1.reshape(half * N_STREAMS * dk, N_STREAMS * 2 * CMP_HIDDEN)
    return w1.astype(BF16), pos8.astype(BF16), w2.astype(BF16)


def _cmp_limits(nc):
    end = np.arange(nc)[:, None] * CMP_STRIDE + (CMP_BLOCK - 1)
    off = np.arange(Q_TILE)[None, :]
    return jnp.asarray(end - off, jnp.int32)


def _sel_map_t(t, nc):
    n_cmp = (t - CMP_BLOCK) // CMP_STRIDE + 1
    n_sel = t // SEL_BLOCK
    cs = np.arange(n_cmp)[:, None] * CMP_STRIDE
    ss = np.arange(n_sel)[None, :] * SEL_BLOCK
    ov = np.maximum(0, np.minimum(cs + CMP_BLOCK, ss + SEL_BLOCK) - np.maximum(cs, ss)) / CMP_BLOCK
    m = np.zeros((n_sel, nc), np.float32)
    m[:, :n_cmp] = ov.T
    return jnp.asarray(m, BF16)


def kernel(x, norm_mix_g, w_in, pos_ck, w_ck1, w_ck2, pos_cv, w_cv1, w_cv2, conv_mix_w, norm_out_attn_g,
           norm_out_conv_g, w_out, norm_ffn_g, w_gate, w_up, ffn_conv_w, ffn_conv_b, w_down, norm_final_g):
    b, t, _ = x.shape
    hh, dk = N_KV_HEADS, HEAD_DIM
    assert t % ROW_TILE == 0 and t % Q_TILE == 0 and WINDOW % Q_TILE == 0 and t // SEL_BLOCK <= AUG_K - AUG_SEL
    assert ROW_TILE % SUB_ROWS == 0 and SUB_ROWS % Q_TILE == 0 and (t // SEL_BLOCK) % 16 == 0
    nc = t // CMP_STRIDE
    nsel = t // SEL_BLOCK
    nkb = t // Q_TILE
    depth = w_in.shape[0]
    assert depth == 1
    xx = x
    for l in range(depth):
        wi = w_in[l]
        col = np.cumsum([0, D_ATTN] + [D_KV] * 6 + [3 * N_HEADS_ATTN] + [D_CONV] * 3)
        kv_slabs = [wi[:, col[3 + 2 * br] + hd * dk:col[3 + 2 * br] + (hd + 1) * dk] if part == 0 else
                    wi[:, col[4 + 2 * br] + hd * dk:col[4 + 2 * br] + (hd + 1) * dk]
                    for br in range(2) for hd in range(hh) for part in range(2)]
        gate_cols = [jnp.concatenate([wi[:, col[7] + 12 * hd:col[7] + 12 * (hd + 1)], jnp.zeros((D_MODEL, 4), F32)], axis=1)
                     for hd in range(hh)]
        w_p = jnp.concatenate([wi[:, 0:D_ATTN]] + kv_slabs + [wi[:, col[1]:col[3]], wi[:, col[8]:col[11]]] + gate_cols
                              + [jnp.zeros((D_MODEL, 128 - 2 * GATE_ROWS), F32)], axis=1).astype(BF16)
        assert w_p.shape[1] == PROJ_COLS
        qt, gt, ksa, kwa, vst, vwt, kvc, mixed_conv = _proj_call(
            xx, norm_mix_g[l][None], w_p, conv_mix_w[l], norm_out_conv_g[l][None],
            _key_consts(t, True), _key_consts(t, False))

        kc, vc = _compress_call(kvc, *_compress_weights(w_ck1[l], w_cv1[l], pos_ck[l], pos_cv[l], w_ck2[l], w_cv2[l]),
                                _cmp_consts(nc))

        o_attn = _attn_call(qt, gt, ksa.reshape(b, hh, nkb, Q_TILE, AUG_K), kwa.reshape(b, hh, nkb, Q_TILE, AUG_K),
                            vst, vwt, kc, vc, _sel_map_t(t, nc), _cmp_limits(nc), min(SEL_TOPK, nsel))

        xx = _ffn_call(xx.reshape(b * t, D_MODEL), o_attn, mixed_conv.reshape(b * t, D_CONV),
                       norm_out_attn_g[l][:, None], w_out[l].astype(BF16), norm_ffn_g[l][None], w_gate[l].astype(BF16),
                       w_up[l].astype(BF16), ffn_conv_w[l], ffn_conv_b[l][None], w_down[l].astype(BF16),
                       norm_final_g[None])
    return xx.reshape(b, t, D_MODEL)
```

```python
import functools

import jax
import jax.numpy as jnp
import numpy as np
from jax import lax
from jax.experimental import pallas as pl
from jax.experimental.pallas import tpu as pltpu

F32 = jnp.float32
BF16 = jnp.bfloat16

D_MODEL = 1024
N_KV_HEADS = 2
GQA_GROUP = 4
N_HEADS_ATTN = N_KV_HEADS * GQA_GROUP
HEAD_DIM = 64
D_ATTN = N_HEADS_ATTN * HEAD_DIM
D_KV = N_KV_HEADS * HEAD_DIM
D_CONV = D_MODEL - D_ATTN
CMP_BLOCK = 32
CMP_STRIDE = 16
CMP_HIDDEN = 2 * HEAD_DIM
SEL_BLOCK = 64
SEL_TOPK = 16
WINDOW = 512
D_FF = 2816
EPS = 1e-6
NEG = -1e30
BIG = 1e30

AUG_K = 256
AUG_ALIBI = HEAD_DIM
AUG_ALIBI_ROWS = 16
AUG_SEL = AUG_ALIBI + AUG_ALIBI_ROWS
V_ROWS = 80

ROW_TILE = 1024
SUB_ROWS = 256
Q_TILE = 256
SEL_UNROLL = 4
FF_CHUNK = 256
PROJ_COLS = 2944
VMEM_LIMIT = 56 * 1024 * 1024

LOG2E = 1.4426950408889634


def _bf16_terms(x, n):
    out = []
    for _ in range(n):
        t = float(np.asarray(x, np.float32).astype(jnp.bfloat16).astype(np.float32))
        out.append(t)
        x = x - t
    return tuple(out)


LOG2E_3 = _bf16_terms(LOG2E, 3)


def _rms(x, g):
    return x * lax.rsqrt(jnp.mean(x * x, axis=-1, keepdims=True) + EPS) * g


COL_Q = 0
COL_KV = D_ATTN
COL_CMP = COL_KV + 4 * 2 * HEAD_DIM
COL_B = COL_CMP + 2 * D_KV
COL_C = COL_B + D_CONV
COL_U = COL_C + D_CONV
COL_GATE = COL_U + D_CONV
GATE_ROWS = 16


def _proj_kernel(x_ref, g_ref, w_ref, cw_ref, gc_ref, csel_ref, cwin_ref,
                 qt_ref, gt_ref, ksa_ref, kwa_ref, vst_ref, vwt_ref, kvc_ref, mc_ref, cbuf, cst, *, tiles_per_seq):
    i = pl.program_id(0)
    tm = x_ref.shape[0]
    tq = vst_ref.shape[-1]

    @pl.when(i % tiles_per_seq == 0)
    def _():
        cbuf[0:8, :] = jnp.zeros((8, D_CONV), F32)

    @pl.when(i % tiles_per_seq != 0)
    def _():
        cbuf[0:8, :] = cbuf[SUB_ROWS:SUB_ROWS + 8, :]

    lane = lax.broadcasted_iota(jnp.int32, (SUB_ROWS, 128), 1)
    ones_rows = jnp.where(lax.broadcasted_iota(jnp.int32, (V_ROWS - HEAD_DIM, tq), 0) == 0, 1.0, 0.0).astype(BF16)
    for sub in range(tm // SUB_ROWS):
        r0 = sub * SUB_ROWS
        rows = slice(r0, r0 + SUB_ROWS)
        if sub > 0:
            cbuf[0:8, :] = cbuf[SUB_ROWS:SUB_ROWS + 8, :]
        h = _rms(x_ref[rows, :], g_ref[...])
        p = jnp.dot(h.astype(BF16), w_ref[...], preferred_element_type=F32)

        for hd in range(N_KV_HEADS):
            qs = p[:, COL_Q + hd * 256:COL_Q + (hd + 1) * 256] * (HEAD_DIM ** -0.5 * LOG2E)
            qt_ref[hd, :, rows] = qs.T.astype(BF16)
        gt = jax.nn.sigmoid(p[:, COL_GATE:COL_GATE + 128]).T
        for hd in range(N_KV_HEADS):
            gt_ref[hd, :, rows] = gt[hd * GATE_ROWS:(hd + 1) * GATE_ROWS, :]

        for branch, (ka_ref, vt_ref, c_ref) in enumerate(((ksa_ref, vst_ref, csel_ref), (kwa_ref, vwt_ref, cwin_ref))):
            for hd in range(N_KV_HEADS):
                c0 = COL_KV + (2 * branch + hd) * 128
                slab = p[:, c0:c0 + 128]
                ka_ref[hd, rows, 0:128] = jnp.where(lane < HEAD_DIM, slab.astype(BF16), c_ref[rows, 0:128])
                ka_ref[hd, rows, 128:AUG_K] = c_ref[rows, 128:AUG_K]
                vt = slab.T[HEAD_DIM:128, :].astype(BF16)
                for kt in range(SUB_ROWS // tq):
                    vt_ref[hd, r0 // tq + kt, 0:HEAD_DIM, :] = vt[:, kt * tq:(kt + 1) * tq]
                    vt_ref[hd, r0 // tq + kt, HEAD_DIM:V_ROWS, :] = ones_rows

        nch = SUB_ROWS // CMP_STRIDE
        for part in range(2):
            cst[part] = p[:, COL_CMP + part * D_KV:COL_CMP + (part + 1) * D_KV]
            for tok in range(CMP_STRIDE):
                c0 = (2 * tok + part) * D_KV
                kvc_ref[r0 // CMP_STRIDE:r0 // CMP_STRIDE + nch, c0:c0 + D_KV] = (
                    cst[part, pl.ds(tok, nch, stride=CMP_STRIDE), :].astype(BF16))
        b = p[:, COL_B:COL_B + D_CONV]
        cu = p[:, COL_C:COL_C + D_CONV] * p[:, COL_U:COL_U + D_CONV]
        cbuf[8:8 + SUB_ROWS, :] = cu
        y = cw_ref[0:1, :] * cbuf[6:6 + SUB_ROWS, :] + cw_ref[1:2, :] * cbuf[7:7 + SUB_ROWS, :] + cw_ref[2:3, :] * cu
        mc_ref[rows, :] = _rms(b * y, gc_ref[...]).astype(BF16)


def _proj_call(x3, g, w, cw, gc, csel, cwin):
    bsz, seq, _ = x3.shape
    tm = ROW_TILE
    tq = Q_TILE
    tps = seq // tm
    hh = N_KV_HEADS
    row = lambda i: (i // tps, i % tps, 0)
    fix = lambda i: (0, 0)
    seq_tile = lambda i: (i % tps, 0)
    tok_minor = lambda i: (i // tps, 0, 0, i % tps)
    tok_major = lambda i: (i // tps, 0, i % tps, 0)
    tok_tiles = lambda i: (i // tps, 0, i % tps, 0, 0)
    return pl.pallas_call(
        functools.partial(_proj_kernel, tiles_per_seq=tps),
        grid=(bsz * tps,),
        in_specs=[
            pl.BlockSpec((None, tm, D_MODEL), row),
            pl.BlockSpec((1, D_MODEL), fix),
            pl.BlockSpec((D_MODEL, PROJ_COLS), fix),
            pl.BlockSpec((3, D_CONV), fix),
            pl.BlockSpec((1, D_CONV), fix),
            pl.BlockSpec((tm, AUG_K), seq_tile),
            pl.BlockSpec((tm, AUG_K), seq_tile),
        ],
        out_specs=[
            pl.BlockSpec((None, hh, GQA_GROUP * HEAD_DIM, tm), tok_minor),
            pl.BlockSpec((None, hh, GATE_ROWS, tm), tok_minor),
            pl.BlockSpec((None, hh, tm, AUG_K), tok_major),
            pl.BlockSpec((None, hh, tm, AUG_K), tok_major),
            pl.BlockSpec((None, hh, tm // tq, V_ROWS, tq), tok_tiles),
            pl.BlockSpec((None, hh, tm // tq, V_ROWS, tq), tok_tiles),
            pl.BlockSpec((None, tm // CMP_STRIDE, CMP_STRIDE * 2 * D_KV), row),
            pl.BlockSpec((None, tm, D_CONV), row),
        ],
        out_shape=[
            jax.ShapeDtypeStruct((bsz, hh, GQA_GROUP * HEAD_DIM, seq), BF16),
            jax.ShapeDtypeStruct((bsz, hh, GATE_ROWS, seq), F32),
            jax.ShapeDtypeStruct((bsz, hh, seq, AUG_K), BF16),
            jax.ShapeDtypeStruct((bsz, hh, seq, AUG_K), BF16),
            jax.ShapeDtypeStruct((bsz, hh, seq // tq, V_ROWS, tq), BF16),
            jax.ShapeDtypeStruct((bsz, hh, seq // tq, V_ROWS, tq), BF16),
            jax.ShapeDtypeStruct((bsz, seq // CMP_STRIDE, CMP_STRIDE * 2 * D_KV), BF16),
            jax.ShapeDtypeStruct((bsz, seq, D_CONV), BF16),
        ],
        scratch_shapes=[pltpu.VMEM((SUB_ROWS + 8, D_CONV), F32), pltpu.VMEM((2, SUB_ROWS, D_KV), F32)],
        compiler_params=pltpu.CompilerParams(dimension_semantics=("arbitrary",), vmem_limit_bytes=VMEM_LIMIT),
    )(x3, g, w, cw, gc, csel, cwin)


N_STREAMS = 2 * N_KV_HEADS


def _compress_kernel(x_ref, w1_ref, pos_ref, w2_ref, c_ref, kc_ref, vc_ref, sbuf):
    nc = x_ref.shape[0]
    y = jnp.dot(x_ref[...], w1_ref[...], preferred_element_type=F32)
    pb = jnp.dot(pos_ref[...], w1_ref[...], preferred_element_type=F32)
    rowi = lax.broadcasted_iota(jnp.int32, (nc, 128), 0)
    lane = lax.broadcasted_iota(jnp.int32, (nc, 128), 1)
    sbuf[:, nc:nc + 8, :] = jnp.zeros((N_STREAMS, 8, CMP_HIDDEN), F32)
    for st in range(N_STREAMS):
        c0 = st * 2 * CMP_HIDDEN
        posb = pb[0:1, c0:c0 + CMP_HIDDEN] + pb[1:2, c0 + CMP_HIDDEN:c0 + 2 * CMP_HIDDEN]
        sbuf[st, 0:nc, :] = y[:, c0 + CMP_HIDDEN:c0 + 2 * CMP_HIDDEN]
        hid = y[:, c0:c0 + CMP_HIDDEN] + sbuf[st, 1:nc + 1, :] + posb
        out = jnp.dot(jax.nn.gelu(hid).astype(BF16), w2_ref[st], preferred_element_type=F32)
        out = jnp.where(rowi < nc - 1, out, 0.0)
        if st < N_KV_HEADS:
            kc_ref[st, :, 0:128] = jnp.where(lane < HEAD_DIM, out.astype(BF16), c_ref[:, 0:128])
            kc_ref[st, :, 128:AUG_K] = c_ref[:, 128:AUG_K]
        else:
            vc_ref[st - N_KV_HEADS] = out.T[0:HEAD_DIM, :].astype(BF16)


def _compress_call(xc, w1, pos8, w2, consts):
    b, nc, width = xc.shape
    hh = N_KV_HEADS
    fix2 = lambda i: (0, 0)
    return pl.pallas_call(
        _compress_kernel,
        grid=(b,),
        in_specs=[
            pl.BlockSpec((None, nc, width), lambda i: (i, 0, 0)),
            pl.BlockSpec((width, N_STREAMS * 2 * CMP_HIDDEN), fix2),
            pl.BlockSpec((8, width), fix2),
            pl.BlockSpec((N_STREAMS, CMP_HIDDEN, 128), lambda i: (0, 0, 0)),
            pl.BlockSpec((nc, AUG_K), fix2),
        ],
        out_specs=[
            pl.BlockSpec((None, hh, nc, AUG_K), lambda i: (i, 0, 0, 0)),
            pl.BlockSpec((None, hh, HEAD_DIM, nc), lambda i: (i, 0, 0, 0)),
        ],
        out_shape=[
            jax.ShapeDtypeStruct((b, hh, nc, AUG_K), BF16),
            jax.ShapeDtypeStruct((b, hh, HEAD_DIM, nc), BF16),
        ],
        scratch_shapes=[pltpu.VMEM((N_STREAMS, nc + 8, CMP_HIDDEN), F32)],
        compiler_params=pltpu.CompilerParams(dimension_semantics=("arbitrary",), vmem_limit_bytes=VMEM_LIMIT),
    )(xc, w1, pos8, w2, consts)


def _attn_kernel(qt_ref, gt_ref, ks_ref, kw_ref, vs_ref, vw_ref, kc_ref, vc_ref, msel_ref, cmask_ref, o_ref,
                 qaug, acc_ref, accw_ref, sbuf, pwbuf, pcbuf, wsbuf, ocmp_ref, imp_ref, selm_ref, klist, *, topk):
    h = pl.program_id(1)
    qi = pl.program_id(2)
    tq = qt_ref.shape[1]
    r = GQA_GROUP * tq
    nc = kc_ref.shape[0]
    nsel = msel_ref.shape[0]
    nkb = ks_ref.shape[0]
    t0 = qi * tq

    lane16 = lax.broadcasted_iota(jnp.int32, (AUG_ALIBI_ROWS, r), 1)
    sub16 = lax.broadcasted_iota(jnp.int32, (AUG_ALIBI_ROWS, r), 0)
    gl = lane16 // tq
    off = (lane16 % tq).astype(F32)
    base = jnp.where(h == 0, 0.5, 0.03125).astype(F32)
    slope = jnp.where(gl == 0, base, jnp.where(gl == 1, base * 0.5, jnp.where(gl == 2, base * 0.25, base * 0.125)))
    blk0 = (t0 // SEL_BLOCK).astype(F32)
    c3 = jnp.where(sub16 % 3 == 0, LOG2E_3[0], jnp.where(sub16 % 3 == 1, LOG2E_3[1], LOG2E_3[2]))
    arow = jnp.where(sub16 < 3, slope * c3,
                     jnp.where(sub16 < 6, 64.0 * slope * c3,
                               jnp.where(sub16 == 6, -slope * LOG2E * (64.0 * blk0 + off), 0.0)))
    for g in range(GQA_GROUP):
        qaug[0:HEAD_DIM, g * tq:(g + 1) * tq] = qt_ref[g * HEAD_DIM:(g + 1) * HEAD_DIM, :]
    qaug[AUG_ALIBI:AUG_SEL, :] = arow.astype(BF16)
    qaug[AUG_SEL:AUG_K, :] = jnp.zeros((AUG_K - AUG_SEL, r), BF16)

    heads = [slice(g * tq, (g + 1) * tq) for g in range(GQA_GROUP)]
    q_i = lax.broadcasted_iota(jnp.int32, (tq, tq), 1)
    key_i = lax.broadcasted_iota(jnp.int32, (tq, tq), 0)
    causal = key_i <= q_i
    nwin = WINDOW // tq

    def win_scores(w, cols):
        kb = qi - nwin + w
        s = jnp.dot(kw_ref[jnp.maximum(kb, 0)], qaug[:, cols], preferred_element_type=F32)
        if w == 0:
            return jnp.where((key_i > q_i) & (kb >= 0), s, NEG)
        if w == nwin:
            return jnp.where(causal, s, NEG)
        return jnp.where(kb >= 0, s, NEG)

    def softmax_tile(s, m_old):
        m_new = jnp.maximum(m_old, jnp.max(s, axis=0, keepdims=True))
        return m_new, jnp.exp2(s - m_new).astype(BF16), jnp.exp2(m_old - m_new)

    def win_pv(w, cols, alpha):
        pv_w = jnp.dot(vw_ref[jnp.maximum(qi - nwin + w, 0)], pwbuf[w % 2, :, cols], preferred_element_type=F32)
        accw_ref[:, cols] = pv_w if w == 0 else accw_ref[:, cols] * alpha + pv_w

    mask_c = cmask_ref[...] <= t0
    cmp_s = [jnp.dot(kc_ref[...], qaug[:, cols], preferred_element_type=F32) for cols in heads]
    win_s = [win_scores(0, cols) for cols in heads]
    p_sum = jnp.zeros((nc, tq), F32)
    for g, cols in enumerate(heads):
        sc = jnp.where(mask_c, cmp_s[g], NEG)
        m_c = jnp.max(sc, axis=0, keepdims=True)
        e_c = jnp.exp2(sc - m_c)
        l_c = jnp.sum(e_c, axis=0, keepdims=True)
        p_c = e_c * jnp.where(m_c > 0.5 * NEG, 1.0 / l_c, 0.0)
        pcbuf[:, cols] = p_c.astype(BF16)
        p_sum = p_sum + p_c

    m_w = [jnp.full((1, tq), NEG, F32)] * GQA_GROUP
    a_w = [None] * GQA_GROUP
    for w in range(nwin):
        nxt = []
        for g, cols in enumerate(heads):
            if w + 1 < nwin:
                nxt.append(win_scores(w + 1, cols))
            else:
                wsbuf[:, cols] = win_scores(w + 1, cols)
            if w == 0:
                ocmp_ref[g * HEAD_DIM:(g + 1) * HEAD_DIM, :] = jnp.dot(vc_ref[...], pcbuf[:, cols],
                                                                       preferred_element_type=F32)
            else:
                win_pv(w - 1, cols, a_w[g])
            m_w[g], p, a_w[g] = softmax_tile(win_s[g], m_w[g])
            pwbuf[w % 2, :, cols] = p
        win_s = nxt
        if w == 0:
            p1 = p_sum.astype(BF16)
            p2 = (p_sum - p1.astype(F32)).astype(BF16)
            msel = msel_ref[...]
            imp = (jnp.dot(msel, p1, preferred_element_type=F32)
                   + jnp.dot(msel, p2, preferred_element_type=F32))

    jj =lax.broadcasted_iota(jnp.int32, (nsel, tq), 0)
    jt = (t0 + lax.broadcasted_iota(jnp.int32, (nsel, tq), 1)) // SEL_BLOCK
    imp = jnp.where((jj == 0) | (jj == jt) | (jj == jt - 1), BIG, imp)
    imp = jnp.where(jj > jt, NEG, imp)
    imp_ref[...] = imp
    bpt = tq // SEL_BLOCK
    sub8 = lax.broadcasted_iota(jnp.int32, (8, tq), 0)

    def rank_select(nblk):
        groups = [imp_ref[8 * gi:8 * gi + 8, :] for gi in range(nblk // 8)]
        cnts = [jnp.zeros((8, tq), jnp.int32) for _ in groups]
        for jp in range(nblk):
            rowv = jnp.broadcast_to(imp_ref[jp:jp + 1, :], (8, tq))
            for gi, grp in enumerate(groups):
                if 8 * gi > jp:
                    beats = rowv >= grp
                elif 8 * gi + 7 <= jp:
                    beats = rowv > grp
                else:
                    beats = (rowv > grp) | ((rowv == grp) & (sub8 + 8 * gi > jp))
                cnts[gi] = cnts[gi] + jnp.where(beats, 1, 0)
        selm = [jnp.where(c < topk, 1.0, 0.0) for c in cnts] + [jnp.zeros((nsel - nblk, tq), F32)] * (nblk < nsel)
        selm = jnp.concatenate(selm, axis=0)
        selm_ref[...] = selm
        selbias = jnp.where(selm > 0.0, 0.0, NEG).astype(BF16)
        for g in range(GQA_GROUP):
            qaug[AUG_SEL:AUG_SEL + nsel, g * tq:(g + 1) * tq] = selbias

    for idx in range(nsel // 16):
        pl.when((qi * bpt) // 16 == idx)(functools.partial(rank_select, 16 * (idx + 1)))

    a_last = [None] * GQA_GROUP
    for g, cols in enumerate(heads):
        sbuf[0, :, cols] = jnp.dot(ks_ref[0], qaug[:, cols], preferred_element_type=F32)
        win_pv(nwin - 1, cols, a_w[g])
        _, p, a_last[g] = softmax_tile(wsbuf[:, cols], m_w[g])
        pwbuf[nwin % 2, :, cols] = p

    n_use = jnp.int32(0)
    for gi in range(nsel // 8):
        hit = jnp.max(selm_ref[8 * gi:8 * gi + 8, :], axis=1, keepdims=True)
        for part in range(8 // bpt):
            kb = gi * (8 // bpt) + part
            if kb < nkb - 1:
                klist[n_use] = kb
                used = (jnp.max(hit[part * bpt:(part + 1) * bpt, :]) > 0.0) & (kb < qi)
                n_use = n_use + used.astype(jnp.int32)
    klist[n_use] = qi

    acc_ref[...] = jnp.zeros(acc_ref.shape, F32)

    def step(j, cur, m_old):
        k_next = ks_ref[klist[j + 1]]
        v_cur = vs_ref[klist[j]]
        for g in range(GQA_GROUP):
            cols = slice(g * tq, (g + 1) * tq)
            sbuf[1 - cur, :, cols] = jnp.dot(k_next, qaug[:, cols], preferred_element_type=F32)
        m_news = []
        for g in range(GQA_GROUP):
            cols = slice(g * tq, (g + 1) * tq)
            m_new, p, alpha = softmax_tile(sbuf[cur, :, cols], m_old[:, cols])
            acc_ref[:, cols] = acc_ref[:, cols] * alpha + jnp.dot(v_cur, p, preferred_element_type=F32)
            m_news.append(m_new)
        return jnp.concatenate(m_news, axis=1)

    def finish(cur, m_old):
        v_last = vs_ref[qi]
        gt = gt_ref[...]
        for g, cols in enumerate(heads):
            _, p_last, alpha_last = softmax_tile(jnp.where(causal, sbuf[cur, :, cols], NEG), m_old[:, cols])
            acc = acc_ref[:, cols] * alpha_last + jnp.dot(v_last, p_last, preferred_element_type=F32)
            win_pv(nwin, cols, a_last[g])
            acc_w = accw_ref[:, cols]
            o_sel = acc[0:HEAD_DIM, :] * (1.0 / acc[HEAD_DIM:HEAD_DIM + 1, :])
            o_win = acc_w[0:HEAD_DIM, :] * (1.0 / acc_w[HEAD_DIM:HEAD_DIM + 1, :])
            o_ref[g * HEAD_DIM:(g + 1) * HEAD_DIM, :] = (gt[3 * g:3 * g + 1, :] * ocmp_ref[g * HEAD_DIM:(g + 1) * HEAD_DIM, :]
                                                         + gt[3 * g + 1:3 * g + 2, :] * o_sel
                                                         + gt[3 * g + 2:3 * g + 3, :] * o_win)

    def unrolled(i, carry):
        for u in range(SEL_UNROLL):
            carry = step(SEL_UNROLL * i + u, u % 2, carry)
        return carry

    carry0 = jnp.full((1, r), NEG, F32)
    n_main = n_use // SEL_UNROLL
    carry_main = lax.fori_loop(0, n_main, unrolled, carry0)
    for rem in range(SEL_UNROLL):
        @pl.when(n_use % SEL_UNROLL == rem)
        def _(rem=rem):
            carry = carry_main
            for u in range(rem):
                carry = step(SEL_UNROLL * n_main + u, u % 2, carry)
            finish(rem % 2, carry)


def _attn_call(qt, gt, ks, kw, vs, vw, kc, vc, msel, cmask, topk):
    b, hh, _, t = qt.shape
    tq = Q_TILE
    nkb = t // tq
    nc = kc.shape[2]
    nsel = msel.shape[0]
    per_q = lambda i, j, k: (i, j, 0, k)
    per_bh4 = lambda i, j, k: (i, j, 0, 0)
    per_bh5 = lambda i, j, k: (i, j, 0, 0, 0)
    return pl.pallas_call(
        functools.partial(_attn_kernel, topk=topk),
        grid=(b, hh, nkb),
        in_specs=[
            pl.BlockSpec((None, None, GQA_GROUP * HEAD_DIM, tq), per_q),
            pl.BlockSpec((None, None, GATE_ROWS, tq), per_q),
            pl.BlockSpec((None, None, nkb, tq, AUG_K), per_bh5),
            pl.BlockSpec((None, None, nkb, tq, AUG_K), per_bh5),
            pl.BlockSpec((None, None, nkb, V_ROWS, tq), per_bh5),
            pl.BlockSpec((None, None, nkb, V_ROWS, tq), per_bh5),
            pl.BlockSpec((None, None, nc, AUG_K), per_bh4),
            pl.BlockSpec((None, None, HEAD_DIM, nc), per_bh4),
            pl.BlockSpec((nsel, nc), lambda i, j, k: (0, 0)),
            pl.BlockSpec((nc, tq), lambda i, j, k: (0, 0)),
        ],
        out_specs=pl.BlockSpec((None, GQA_GROUP * HEAD_DIM, tq), lambda i, j, k: (i, j, k)),
        out_shape=jax.ShapeDtypeStruct((b, D_ATTN, t), F32),
        scratch_shapes=[pltpu.VMEM((AUG_K, GQA_GROUP * tq), BF16), pltpu.VMEM((V_ROWS, GQA_GROUP * tq), F32),
                        pltpu.VMEM((V_ROWS, GQA_GROUP * tq), F32),
                        pltpu.VMEM((2, tq, GQA_GROUP * tq), F32),
                        pltpu.VMEM((2, tq, GQA_GROUP * tq), BF16), pltpu.VMEM((nc, GQA_GROUP * tq), BF16),
                        pltpu.VMEM((tq, GQA_GROUP * tq), F32), pltpu.VMEM((GQA_GROUP * HEAD_DIM, tq), F32),
                        pltpu.VMEM((nsel, tq), F32), pltpu.VMEM((nsel, tq), F32),
                        pltpu.SMEM((nkb + 1,), jnp.int32)],
        compiler_params=pltpu.CompilerParams(dimension_semantics=("arbitrary", "arbitrary", "arbitrary"),
                                             vmem_limit_bytes=VMEM_LIMIT),
    )(qt, gt, ks, kw, vs, vw, kc, vc, msel, cmask)


def _ffn_kernel(x_ref, oa_ref, mc_ref, ga_ref, wo_ref, gf_ref, wg_ref, wu_ref, cw_ref, cb_ref, wd_ref, gl_ref,
                o_ref, gbuf, ybuf, *, tiles_per_seq):
    i = pl.program_id(0)
    tm = x_ref.shape[0]

    @pl.when(i % tiles_per_seq == 0)
    def _():
        gbuf[0:8, :] = jnp.zeros((8, D_FF), F32)

    @pl.when(i % tiles_per_seq != 0)
    def _():
        gbuf[0:8, :] = gbuf[SUB_ROWS:SUB_ROWS + 8, :]

    subs = [slice(r0, r0 + SUB_ROWS) for r0 in range(0, tm, SUB_ROWS)]
    x1s, h2s = [], []
    for rows in subs:
        oat = oa_ref[:, rows]
        mat = (oat * lax.rsqrt(jnp.mean(oat * oat, axis=0, keepdims=True) + EPS) * ga_ref[...]).astype(BF16)
        x1 = (x_ref[rows, :] + lax.dot_general(mat, wo_ref[0:D_ATTN, :], (((0,), (0,)), ((), ())),
                                                preferred_element_type=F32)
              + jnp.dot(mc_ref[rows, :], wo_ref[D_ATTN:D_MODEL, :], preferred_element_type=F32))
        x1s.append(x1)
        h2s.append(_rms(x1, gf_ref[...]).astype(BF16))
    for rows, x1, h2 in zip(subs, x1s, h2s):
        if rows.start > 0:
            gbuf[0:8, :] = gbuf[SUB_ROWS:SUB_ROWS + 8, :]
        for c in range(D_FF // FF_CHUNK):
            cs = slice(c * FF_CHUNK, (c + 1) * FF_CHUNK)
            gpre = jnp.dot(h2, wg_ref[:, cs], preferred_element_type=F32)
            up = jnp.dot(h2, wu_ref[:, cs], preferred_element_type=F32)
            gbuf[8:8 + SUB_ROWS, cs] = gpre
            gate = (cw_ref[0:1, cs] * gbuf[6:6 + SUB_ROWS, cs] + cw_ref[1:2, cs] * gbuf[7:7 + SUB_ROWS, cs]
                    + cw_ref[2:3, cs] * gpre + cb_ref[:, cs])
            ybuf[:, cs] = (jax.nn.silu(gate) * up).astype(BF16)
        acc = x1 + jnp.dot(ybuf[...], wd_ref[...], preferred_element_type=F32)
        o_ref[rows, :] = _rms(acc, gl_ref[...])


def _ffn_call(x2, oat, mc, ga, wo, gf, wg, wu, cw, cb, wd, gl):
    n = x2.shape[0]
    seq = oat.shape[2]
    tm = ROW_TILE
    tps = seq // tm
    row = lambda i: (i, 0)
    fix = lambda i: (0, 0)
    once = dict(pipeline_mode=pl.Buffered(1))
    return pl.pallas_call(
        functools.partial(_ffn_kernel, tiles_per_seq=seq // tm),
        grid=(n // tm,),
        in_specs=[
            pl.BlockSpec((tm, D_MODEL), row),
            pl.BlockSpec((None, D_ATTN, tm), lambda i: (i // tps, 0, i % tps)),
            pl.BlockSpec((tm, D_CONV), row),
            pl.BlockSpec((D_ATTN, 1), fix),
            pl.BlockSpec((D_MODEL, D_MODEL), fix, **once),
            pl.BlockSpec((1, D_MODEL), fix),
            pl.BlockSpec((D_MODEL, D_FF), fix, **once),
            pl.BlockSpec((D_MODEL, D_FF), fix, **once),
            pl.BlockSpec((3, D_FF), fix),
            pl.BlockSpec((1, D_FF), fix),
            pl.BlockSpec((D_FF, D_MODEL), fix, **once),
            pl.BlockSpec((1, D_MODEL), fix),
        ],
        out_specs=pl.BlockSpec((tm, D_MODEL), row),
        out_shape=jax.ShapeDtypeStruct((n, D_MODEL), F32),
        scratch_shapes=[pltpu.VMEM((SUB_ROWS + 8, D_FF), F32), pltpu.VMEM((SUB_ROWS, D_FF), BF16)],
        compiler_params=pltpu.CompilerParams(dimension_semantics=("arbitrary",), vmem_limit_bytes=VMEM_LIMIT),
    )(x2, oat, mc, ga, wo, gf, wg, wu, cw, cb, wd, gl)


def _alibi_cols(pos):
    cols = np.zeros((len(pos), AUG_ALIBI_ROWS), np.float32)
    cols[:, 0:3] = (pos % SEL_BLOCK)[:, None]
    cols[:, 3:6] = (pos // SEL_BLOCK)[:, None]
    cols[:, 6] = 1.0
    return cols


def _key_consts(t, with_sel):
    pos = np.arange(t)
    c = np.zeros((t, AUG_K), np.float32)
    c[:, AUG_ALIBI:AUG_SEL] = _alibi_cols(pos)
    if with_sel:
        c[pos, AUG_SEL + pos // SEL_BLOCK] = 1.0
    return jnp.asarray(c, BF16)


def _cmp_consts(nc):
    c = np.zeros((nc, AUG_K), np.float32)
    c[:, AUG_ALIBI:AUG_SEL] = _alibi_cols(np.arange(nc) * CMP_STRIDE + (CMP_BLOCK - 1))
    return jnp.asarray(c, BF16)


def _compress_weights(w_k1, w_v1, pos_k, pos_v, w_k2, w_v2):
    hh, dk, half = N_KV_HEADS, HEAD_DIM, CMP_STRIDE
    w1 = jnp.zeros((half, N_STREAMS, dk, N_STREAMS, 2, CMP_HIDDEN), F32)
    pos = []
    for st in range(N_STREAMS):
        w = (w_k1 if st < hh else w_v1).reshape(2, half, dk, CMP_HIDDEN)
        w1 = w1.at[:, st, :, st, :, :].set(w.transpose(1, 2, 0, 3))
        pos.append((pos_k if st < hh else pos_v).reshape(2, half, dk))
    pos = jnp.stack(pos, axis=2).reshape(2, half * N_STREAMS * dk)
    pos8 = jnp.concatenate([pos, jnp.zeros((6, pos.shape[1]), F32)], axis=0)
    w2 = jnp.stack([jnp.concatenate([w_k2 if st < hh else w_v2, jnp.zeros((CMP_HIDDEN, 128 - dk), F32)], axis=1)
                    for st in range(N_STREAMS)])
    w1 = w1.reshape(half * N_STREAMS * dk, N_STREAMS * 2 * CMP_HIDDEN)
    return w1.astype(BF16), pos8.astype(BF16), w2.astype(BF16)


def _cmp_limits(nc):
    end = np.arange(nc)[:, None] * CMP_STRIDE + (CMP_BLOCK - 1)
    off = np.arange(Q_TILE)[None, :]
    return jnp.asarray(end - off, jnp.int32)


def _sel_map_t(t, nc):
    n_cmp = (t - CMP_BLOCK) // CMP_STRIDE + 1
    n_sel = t // SEL_BLOCK
    cs = np.arange(n_cmp)[:, None] * CMP_STRIDE
    ss = np.arange(n_sel)[None, :] * SEL_BLOCK
    ov = np.maximum(0, np.minimum(cs + CMP_BLOCK, ss + SEL_BLOCK) - np.maximum(cs, ss)) / CMP_BLOCK
    m = np.zeros((n_sel, nc), np.float32)
    m[:, :n_cmp] = ov.T
    return jnp.asarray(m, BF16)


def kernel(x, norm_mix_g, w_in, pos_ck, w_ck1, w_ck2, pos_cv, w_cv1, w_cv2, conv_mix_w, norm_out_attn_g,
           norm_out_conv_g, w_out, norm_ffn_g, w_gate, w_up, ffn_conv_w, ffn_conv_b, w_down, norm_final_g):
    b, t, _ = x.shape
    hh, dk = N_KV_HEADS, HEAD_DIM
    assert t % ROW_TILE == 0 and t % Q_TILE == 0 and WINDOW % Q_TILE == 0 and t // SEL_BLOCK <= AUG_K - AUG_SEL
    assert ROW_TILE % SUB_ROWS == 0 and SUB_ROWS % Q_TILE == 0 and (t // SEL_BLOCK) % 16 == 0
    nc = t // CMP_STRIDE
    nsel = t // SEL_BLOCK
    nkb = t // Q_TILE
    depth = w_in.shape[0]
    assert depth == 1
    xx = x
    for l in range(depth):
        wi = w_in[l]
        col = np.cumsum([0, D_ATTN] + [D_KV] * 6 + [3 * N_HEADS_ATTN] + [D_CONV] * 3)
        kv_slabs = [wi[:, col[3 + 2 * br] + hd * dk:col[3 + 2 * br] + (hd + 1) * dk] if part == 0 else
                    wi[:, col[4 + 2 * br] + hd * dk:col[4 + 2 * br] + (hd + 1) * dk]
                    for br in range(2) for hd in range(hh) for part in range(2)]
        gate_cols = [jnp.concatenate([wi[:, col[7] + 12 * hd:col[7] + 12 * (hd + 1)], jnp.zeros((D_MODEL, 4), F32)], axis=1)
                     for hd in range(hh)]
        w_p = jnp.concatenate([wi[:, 0:D_ATTN]] + kv_slabs + [wi[:, col[1]:col[3]], wi[:, col[8]:col[11]]] + gate_cols
                              + [jnp.zeros((D_MODEL, 128 - 2 * GATE_ROWS), F32)], axis=1).astype(BF16)
        assert w_p.shape[1] == PROJ_COLS
        qt, gt, ksa, kwa, vst, vwt, kvc, mixed_conv = _proj_call(
            xx, norm_mix_g[l][None], w_p, conv_mix_w[l], norm_out_conv_g[l][None],
            _key_consts(t, True), _key_consts(t, False))

        kc, vc = _compress_call(kvc, *_compress_weights(w_ck1[l], w_cv1[l], pos_ck[l], pos_cv[l], w_ck2[l], w_cv2[l]),
                                _cmp_consts(nc))

        o_attn = _attn_call(qt, gt, ksa.reshape(b, hh, nkb, Q_TILE, AUG_K), kwa.reshape(b, hh, nkb, Q_TILE, AUG_K),
                            vst, vwt, kc, vc, _sel_map_t(t, nc), _cmp_limits(nc), min(SEL_TOPK, nsel))

        xx = _ffn_call(xx.reshape(b * t, D_MODEL), o_attn, mixed_conv.reshape(b * t, D_CONV),
                       norm_out_attn_g[l][:, None], w_out[l].astype(BF16), norm_ffn_g[l][None], w_gate[l].astype(BF16),
                       w_up[l].astype(BF16), ffn_conv_w[l], ffn_conv_b[l][None], w_down[l].astype(BF16),
                       norm_final_g[None])
    return xx.reshape(b, t, D_MODEL)
```

```python
import functools

import jax
import jax.numpy as jnp
import numpy as np
from jax import lax
from jax.experimental import pallas as pl
from jax.experimental.pallas import tpu as pltpu

F32 = jnp.float32
BF16 = jnp.bfloat16

D_MODEL = 1024
N_KV_HEADS = 2
GQA_GROUP = 4
N_HEADS_ATTN = N_KV_HEADS * GQA_GROUP
HEAD_DIM = 64
D_ATTN = N_HEADS_ATTN * HEAD_DIM
D_KV = N_KV_HEADS * HEAD_DIM
D_CONV = D_MODEL - D_ATTN
CMP_BLOCK = 32
CMP_STRIDE = 16
CMP_HIDDEN = 2 * HEAD_DIM
SEL_BLOCK = 64
SEL_TOPK = 16
WINDOW = 512
D_FF = 2816
EPS = 1e-6
NEG = -1e30
BIG = 1e30

AUG_K = 256
AUG_ALIBI = HEAD_DIM
AUG_ALIBI_ROWS = 16
AUG_SEL = AUG_ALIBI + AUG_ALIBI_ROWS
V_ROWS = 80

ROW_TILE = 1024
SUB_ROWS = 256
Q_TILE = 256
SEL_UNROLL = 4
FF_CHUNK = 256
PROJ_COLS = 2944
VMEM_LIMIT = 56 * 1024 * 1024

LOG2E = 1.4426950408889634


def _bf16_terms(x, n):
    out = []
    for _ in range(n):
        t = float(np.asarray(x, np.float32).astype(jnp.bfloat16).astype(np.float32))
        out.append(t)
        x = x - t
    return tuple(out)


LOG2E_3 = _bf16_terms(LOG2E, 3)


def _rms(x, g):
    return x * lax.rsqrt(jnp.mean(x * x, axis=-1, keepdims=True) + EPS) * g


COL_Q = 0
COL_KV = D_ATTN
COL_CMP = COL_KV + 4 * 2 * HEAD_DIM
COL_B = COL_CMP + 2 * D_KV
COL_C = COL_B + D_CONV
COL_U = COL_C + D_CONV
COL_GATE = COL_U + D_CONV
GATE_ROWS = 16


def _proj_kernel(x_ref, g_ref, w_ref, cw_ref, gc_ref, csel_ref, cwin_ref,
                 qt_ref, gt_ref, ksa_ref, kwa_ref, vst_ref, vwt_ref, kvc_ref, mc_ref, cbuf, cst, *, tiles_per_seq):
    i = pl.program_id(0)
    tm = x_ref.shape[0]
    tq = vst_ref.shape[-1]

    @pl.when(i % tiles_per_seq == 0)
    def _():
        cbuf[0:8, :] = jnp.zeros((8, D_CONV), F32)

    @pl.when(i % tiles_per_seq != 0)
    def _():
        cbuf[0:8, :] = cbuf[SUB_ROWS:SUB_ROWS + 8, :]

    lane = lax.broadcasted_iota(jnp.int32, (SUB_ROWS, 128), 1)
    ones_rows = jnp.where(lax.broadcasted_iota(jnp.int32, (V_ROWS - HEAD_DIM, tq), 0) == 0, 1.0, 0.0).astype(BF16)
    for sub in range(tm // SUB_ROWS):
        r0 = sub * SUB_ROWS
        rows = slice(r0, r0 + SUB_ROWS)
        if sub > 0:
            cbuf[0:8, :] = cbuf[SUB_ROWS:SUB_ROWS + 8, :]
        h = _rms(x_ref[rows, :], g_ref[...])
        p = jnp.dot(h.astype(BF16), w_ref[...], preferred_element_type=F32)

        for hd in range(N_KV_HEADS):
            qs = p[:, COL_Q + hd * 256:COL_Q + (hd + 1) * 256] * (HEAD_DIM ** -0.5 * LOG2E)
            qt_ref[hd, :, rows] = qs.T.astype(BF16)
        gt = jax.nn.sigmoid(p[:, COL_GATE:COL_GATE + 128]).T
        for hd in range(N_KV_HEADS):
            gt_ref[hd, :, rows] = gt[hd * GATE_ROWS:(hd + 1) * GATE_ROWS, :]

        for branch, (ka_ref, vt_ref, c_ref) in enumerate(((ksa_ref, vst_ref, csel_ref), (kwa_ref, vwt_ref, cwin_ref))):
            for hd in range(N_KV_HEADS):
                c0 = COL_KV + (2 * branch + hd) * 128
                slab = p[:, c0:c0 + 128]
                ka_ref[hd, rows, 0:128] = jnp.where(lane < HEAD_DIM, slab.astype(BF16), c_ref[rows, 0:128])
                ka_ref[hd, rows, 128:AUG_K] = c_ref[rows, 128:AUG_K]
                vt = slab.T[HEAD_DIM:128, :].astype(BF16)
                for kt in range(SUB_ROWS // tq):
                    vt_ref[hd, r0 // tq + kt, 0:HEAD_DIM, :] = vt[:, kt * tq:(kt + 1) * tq]
                    vt_ref[hd, r0 // tq + kt, HEAD_DIM:V_ROWS, :] = ones_rows

        nch = SUB_ROWS // CMP_STRIDE
        for part in range(2):
            cst[part] = p[:, COL_CMP + part * D_KV:COL_CMP + (part + 1) * D_KV]
            for tok in range(CMP_STRIDE):
                c0 = (2 * tok + part) * D_KV
                kvc_ref[r0 // CMP_STRIDE:r0 // CMP_STRIDE + nch, c0:c0 + D_KV] = (
                    cst[part, pl.ds(tok, nch, stride=CMP_STRIDE), :].astype(BF16))
        b = p[:, COL_B:COL_B + D_CONV]
        cu = p[:, COL_C:COL_C + D_CONV] * p[:, COL_U:COL_U + D_CONV]
        cbuf[8:8 + SUB_ROWS, :] = cu
        y = cw_ref[0:1, :] * cbuf[6:6 + SUB_ROWS, :] + cw_ref[1:2, :] * cbuf[7:7 + SUB_ROWS, :] + cw_ref[2:3, :] * cu
        mc_ref[rows, :] = _rms(b * y, gc_ref[...]).astype(BF16)


def _proj_call(x3, g, w, cw, gc, csel, cwin):
    bsz, seq, _ = x3.shape
    tm = ROW_TILE
    tq = Q_TILE
    tps = seq // tm
    hh = N_KV_HEADS
    row = lambda i: (i // tps, i % tps, 0)
    fix = lambda i: (0, 0)
    seq_tile = lambda i: (i % tps, 0)
    tok_minor = lambda i: (i // tps, 0, 0, i % tps)
    tok_major = lambda i: (i // tps, 0, i % tps, 0)
    tok_tiles = lambda i: (i // tps, 0, i % tps, 0, 0)
    return pl.pallas_call(
        functools.partial(_proj_kernel, tiles_per_seq=tps),
        grid=(bsz * tps,),
        in_specs=[
            pl.BlockSpec((None, tm, D_MODEL), row),
            pl.BlockSpec((1, D_MODEL), fix),
            pl.BlockSpec((D_MODEL, PROJ_COLS), fix),
            pl.BlockSpec((3, D_CONV), fix),
            pl.BlockSpec((1, D_CONV), fix),
            pl.BlockSpec((tm, AUG_K), seq_tile),
            pl.BlockSpec((tm, AUG_K), seq_tile),
        ],
        out_specs=[
            pl.BlockSpec((None, hh, GQA_GROUP * HEAD_DIM, tm), tok_minor),
            pl.BlockSpec((None, hh, GATE_ROWS, tm), tok_minor),
            pl.BlockSpec((None, hh, tm, AUG_K), tok_major),
            pl.BlockSpec((None, hh, tm, AUG_K), tok_major),
            pl.BlockSpec((None, hh, tm // tq, V_ROWS, tq), tok_tiles),
            pl.BlockSpec((None, hh, tm // tq, V_ROWS, tq), tok_tiles),
            pl.BlockSpec((None, tm // CMP_STRIDE, CMP_STRIDE * 2 * D_KV), row),
            pl.BlockSpec((None, tm, D_CONV), row),
        ],
        out_shape=[
            jax.ShapeDtypeStruct((bsz, hh, GQA_GROUP * HEAD_DIM, seq), BF16),
            jax.ShapeDtypeStruct((bsz, hh, GATE_ROWS, seq), F32),
            jax.ShapeDtypeStruct((bsz, hh, seq, AUG_K), BF16),
            jax.ShapeDtypeStruct((bsz, hh, seq, AUG_K), BF16),
            jax.ShapeDtypeStruct((bsz, hh, seq // tq, V_ROWS, tq), BF16),
            jax.ShapeDtypeStruct((bsz, hh, seq // tq, V_ROWS, tq), BF16),
            jax.ShapeDtypeStruct((bsz, seq // CMP_STRIDE, CMP_STRIDE * 2 * D_KV), BF16),
            jax.ShapeDtypeStruct((bsz, seq, D_CONV), BF16),
        ],
        scratch_shapes=[pltpu.VMEM((SUB_ROWS + 8, D_CONV), F32), pltpu.VMEM((2, SUB_ROWS, D_KV), F32)],
        compiler_params=pltpu.CompilerParams(dimension_semantics=("arbitrary",), vmem_limit_bytes=VMEM_LIMIT),
    )(x3, g, w, cw, gc, csel, cwin)


N_STREAMS = 2 * N_KV_HEADS


def _compress_kernel(x_ref, w1_ref, pos_ref, w2_ref, c_ref, kc_ref, vc_ref, sbuf):
    nc = x_ref.shape[0]
    y = jnp.dot(x_ref[...], w1_ref[...], preferred_element_type=F32)
    pb = jnp.dot(pos_ref[...], w1_ref[...], preferred_element_type=F32)
    rowi = lax.broadcasted_iota(jnp.int32, (nc, 128), 0)
    lane = lax.broadcasted_iota(jnp.int32, (nc, 128), 1)
    sbuf[:, nc:nc + 8, :] = jnp.zeros((N_STREAMS, 8, CMP_HIDDEN), F32)
    for st in range(N_STREAMS):
        c0 = st * 2 * CMP_HIDDEN
        posb = pb[0:1, c0:c0 + CMP_HIDDEN] + pb[1:2, c0 + CMP_HIDDEN:c0 + 2 * CMP_HIDDEN]
        sbuf[st, 0:nc, :] = y[:, c0 + CMP_HIDDEN:c0 + 2 * CMP_HIDDEN]
        hid = y[:, c0:c0 + CMP_HIDDEN] + sbuf[st, 1:nc + 1, :] + posb
        out = jnp.dot(jax.nn.gelu(hid).astype(BF16), w2_ref[st], preferred_element_type=F32)
        out = jnp.where(rowi < nc - 1, out, 0.0)
        if st < N_KV_HEADS:
            kc_ref[st, :, 0:128] = jnp.where(lane < HEAD_DIM, out.astype(BF16), c_ref[:, 0:128])
            kc_ref[st, :, 128:AUG_K] = c_ref[:, 128:AUG_K]
        else:
            vc_ref[st - N_KV_HEADS] = out.T[0:HEAD_DIM, :].astype(BF16)


def _compress_call(xc, w1, pos8, w2, consts):
    b, nc, width = xc.shape
    hh = N_KV_HEADS
    fix2 = lambda i: (0, 0)
    return pl.pallas_call(
        _compress_kernel,
        grid=(b,),
        in_specs=[
            pl.BlockSpec((None, nc, width), lambda i: (i, 0, 0)),
            pl.BlockSpec((width, N_STREAMS * 2 * CMP_HIDDEN), fix2),
            pl.BlockSpec((8, width), fix2),
            pl.BlockSpec((N_STREAMS, CMP_HIDDEN, 128), lambda i: (0, 0, 0)),
            pl.BlockSpec((nc, AUG_K), fix2),
        ],
        out_specs=[
            pl.BlockSpec((None, hh, nc, AUG_K), lambda i: (i, 0, 0, 0)),
            pl.BlockSpec((None, hh, HEAD_DIM, nc), lambda i: (i, 0, 0, 0)),
        ],
        out_shape=[
            jax.ShapeDtypeStruct((b, hh, nc, AUG_K), BF16),
            jax.ShapeDtypeStruct((b, hh, HEAD_DIM, nc), BF16),
        ],
        scratch_shapes=[pltpu.VMEM((N_STREAMS, nc + 8, CMP_HIDDEN), F32)],
        compiler_params=pltpu.CompilerParams(dimension_semantics=("arbitrary",), vmem_limit_bytes=VMEM_LIMIT),
    )(xc, w1, pos8, w2, consts)


def _attn_kernel(qt_ref, gt_ref, ks_ref, kw_ref, vs_ref, vw_ref, kc_ref, vc_ref, msel_ref, cmask_ref, o_ref,
                 qaug, acc_ref, accw_ref, sbuf, wsbuf, ocmp_ref, imp_ref, selm_ref, klist, *, topk):
    h = pl.program_id(1)
    qi = pl.program_id(2)
    tq = qt_ref.shape[1]
    r = GQA_GROUP * tq
    nc = kc_ref.shape[0]
    nsel = msel_ref.shape[0]
    nkb = ks_ref.shape[0]
    t0 = qi * tq

    lane16 = lax.broadcasted_iota(jnp.int32, (AUG_ALIBI_ROWS, r), 1)
    sub16 = lax.broadcasted_iota(jnp.int32, (AUG_ALIBI_ROWS, r), 0)
    gl = lane16 // tq
    off = (lane16 % tq).astype(F32)
    base = jnp.where(h == 0, 0.5, 0.03125).astype(F32)
    slope = jnp.where(gl == 0, base, jnp.where(gl == 1, base * 0.5, jnp.where(gl == 2, base * 0.25, base * 0.125)))
    blk0 = (t0 // SEL_BLOCK).astype(F32)
    c3 = jnp.where(sub16 % 3 == 0, LOG2E_3[0], jnp.where(sub16 % 3 == 1, LOG2E_3[1], LOG2E_3[2]))
    arow = jnp.where(sub16 < 3, slope * c3,
                     jnp.where(sub16 < 6, 64.0 * slope * c3,
                               jnp.where(sub16 == 6, -slope * LOG2E * (64.0 * blk0 + off), 0.0)))
    for g in range(GQA_GROUP):
        qaug[0:HEAD_DIM, g * tq:(g + 1) * tq] = qt_ref[g * HEAD_DIM:(g + 1) * HEAD_DIM, :]
    qaug[AUG_ALIBI:AUG_SEL, :] = arow.astype(BF16)
    qaug[AUG_SEL:AUG_K, :] = jnp.zeros((AUG_K - AUG_SEL, r), BF16)

    heads = [slice(g * tq, (g + 1) * tq) for g in range(GQA_GROUP)]
    q_i = lax.broadcasted_iota(jnp.int32, (tq, tq), 1)
    key_i = lax.broadcasted_iota(jnp.int32, (tq, tq), 0)
    causal = key_i <= q_i
    nwin = WINDOW // tq

    def win_scores(w, cols):
        kb = qi - nwin + w
        s = jnp.dot(kw_ref[jnp.maximum(kb, 0)], qaug[:, cols], preferred_element_type=F32)
        if w == 0:
            return jnp.where((key_i > q_i) & (kb >= 0), s, NEG)
        if w == nwin:
            return jnp.where(causal, s, NEG)
        return jnp.where(kb >= 0, s, NEG)

    def softmax_tile(s, m_old):
        m_new = jnp.maximum(m_old, jnp.max(s, axis=0, keepdims=True))
        return m_new, jnp.exp2(s - m_new).astype(BF16), jnp.exp2(m_old - m_new)

    def win_update(w, cols, s, m_old):
        m_new, p, alpha = softmax_tile(s, m_old)
        pv_w = jnp.dot(vw_ref[jnp.maximum(qi - nwin + w, 0)], p, preferred_element_type=F32)
        accw_ref[:, cols] = pv_w if w == 0 else accw_ref[:, cols] * alpha + pv_w
        return m_new

    mask_c = cmask_ref[...] <= t0
    cmp_s = [jnp.dot(kc_ref[...], qaug[:, cols], preferred_element_type=F32) for cols in heads]
    win_s = [win_scores(0, cols) for cols in heads]
    p_sum = jnp.zeros((nc, tq), F32)
    for g, cols in enumerate(heads):
        sc = jnp.where(mask_c, cmp_s[g], NEG)
        m_c = jnp.max(sc, axis=0, keepdims=True)
        e_c = jnp.exp2(sc - m_c)
        l_c = jnp.sum(e_c, axis=0, keepdims=True)
        p_c = e_c * jnp.where(m_c > 0.5 * NEG, 1.0 / l_c, 0.0)
        ocmp_ref[g * HEAD_DIM:(g + 1) * HEAD_DIM, :] = jnp.dot(vc_ref[...], p_c.astype(BF16),
                                                               preferred_element_type=F32)
        p_sum = p_sum + p_c

    m_w = [jnp.full((1, tq), NEG, F32)] * GQA_GROUP
    for w in range(nwin):
        if w + 1 < nwin:
            nxt = [win_scores(w + 1, cols) for cols in heads]
        else:
            nxt = None
            for cols in heads:
                wsbuf[:, cols] = win_scores(w + 1, cols)
        for g, cols in enumerate(heads):
            m_w[g] = win_update(w, cols, win_s[g], m_w[g])
        win_s = nxt
        if w == 0:
            p1 = p_sum.astype(BF16)
            p2 = (p_sum - p1.astype(F32)).astype(BF16)
            msel = msel_ref[...]
            imp = (jnp.dot(msel, p1, preferred_element_type=F32)
                   + jnp.dot(msel, p2, preferred_element_type=F32))

    jj =lax.broadcasted_iota(jnp.int32, (nsel, tq), 0)
    jt = (t0 + lax.broadcasted_iota(jnp.int32, (nsel, tq), 1)) // SEL_BLOCK
    imp = jnp.where((jj == 0) | (jj == jt) | (jj == jt - 1), BIG, imp)
    imp = jnp.where(jj > jt, NEG, imp)
    imp_ref[...] = imp
    bpt = tq // SEL_BLOCK
    sub8 = lax.broadcasted_iota(jnp.int32, (8, tq), 0)

    def rank_select(nblk):
        groups = [imp_ref[8 * gi:8 * gi + 8, :] for gi in range(nblk // 8)]
        cnts = [jnp.zeros((8, tq), jnp.int32) for _ in groups]
        for jp in range(nblk):
            rowv = jnp.broadcast_to(imp_ref[jp:jp + 1, :], (8, tq))
            for gi, grp in enumerate(groups):
                if 8 * gi > jp:
                    beats = rowv >= grp
                elif 8 * gi + 7 <= jp:
                    beats = rowv > grp
                else:
                    beats = (rowv > grp) | ((rowv == grp) & (sub8 + 8 * gi > jp))
                cnts[gi] = cnts[gi] + jnp.where(beats, 1, 0)
        selm = [jnp.where(c < topk, 1.0, 0.0) for c in cnts] + [jnp.zeros((nsel - nblk, tq), F32)] * (nblk < nsel)
        selm = jnp.concatenate(selm, axis=0)
        selm_ref[...] = selm
        selbias = jnp.where(selm > 0.0, 0.0, NEG).astype(BF16)
        for g in range(GQA_GROUP):
            qaug[AUG_SEL:AUG_SEL + nsel, g * tq:(g + 1) * tq] = selbias

    for idx in range(nsel // 16):
        pl.when((qi * bpt) // 16 == idx)(functools.partial(rank_select, 16 * (idx + 1)))

    for cols in heads:
        sbuf[0, :, cols] = jnp.dot(ks_ref[0], qaug[:, cols], preferred_element_type=F32)
    for g, cols in enumerate(heads):
        win_update(nwin, cols, wsbuf[:, cols], m_w[g])

    n_use = jnp.int32(0)
    for gi in range(nsel // 8):
        hit = jnp.max(selm_ref[8 * gi:8 * gi + 8, :], axis=1, keepdims=True)
        for part in range(8 // bpt):
            kb = gi * (8 // bpt) + part
            if kb < nkb - 1:
                klist[n_use] = kb
                used = (jnp.max(hit[part * bpt:(part + 1) * bpt, :]) > 0.0) & (kb < qi)
                n_use = n_use + used.astype(jnp.int32)
    klist[n_use] = qi

    acc_ref[...] = jnp.zeros(acc_ref.shape, F32)

    def step(j, cur, m_old):
        k_next = ks_ref[klist[j + 1]]
        v_cur = vs_ref[klist[j]]
        for g in range(GQA_GROUP):
            cols = slice(g * tq, (g + 1) * tq)
            sbuf[1 - cur, :, cols] = jnp.dot(k_next, qaug[:, cols], preferred_element_type=F32)
        m_news = []
        for g in range(GQA_GROUP):
            cols = slice(g * tq, (g + 1) * tq)
            m_new, p, alpha = softmax_tile(sbuf[cur, :, cols], m_old[:, cols])
            acc_ref[:, cols] = acc_ref[:, cols] * alpha + jnp.dot(v_cur, p, preferred_element_type=F32)
            m_news.append(m_new)
        return jnp.concatenate(m_news, axis=1)

    def finish(cur, m_old):
        v_last = vs_ref[qi]
        gt = gt_ref[...]
        for g, cols in enumerate(heads):
            _, p_last, alpha_last = softmax_tile(jnp.where(causal, sbuf[cur, :, cols], NEG), m_old[:, cols])
            acc = acc_ref[:, cols] * alpha_last + jnp.dot(v_last, p_last, preferred_element_type=F32)
            acc_w = accw_ref[:, cols]
            o_sel = acc[0:HEAD_DIM, :] * (1.0 / acc[HEAD_DIM:HEAD_DIM + 1, :])
            o_win = acc_w[0:HEAD_DIM, :] * (1.0 / acc_w[HEAD_DIM:HEAD_DIM + 1, :])
            o_ref[g * HEAD_DIM:(g + 1) * HEAD_DIM, :] = (gt[3 * g:3 * g + 1, :] * ocmp_ref[g * HEAD_DIM:(g + 1) * HEAD_DIM, :]
                                                         + gt[3 * g + 1:3 * g + 2, :] * o_sel
                                                         + gt[3 * g + 2:3 * g + 3, :] * o_win)

    def unrolled(i, carry):
        for u in range(SEL_UNROLL):
            carry = step(SEL_UNROLL * i + u, u % 2, carry)
        return carry

    carry0 = jnp.full((1, r), NEG, F32)
    n_main = n_use // SEL_UNROLL
    carry_main = lax.fori_loop(0, n_main, unrolled, carry0)
    for rem in range(SEL_UNROLL):
        @pl.when(n_use % SEL_UNROLL == rem)
        def _(rem=rem):
            carry = carry_main
            for u in range(rem):
                carry = step(SEL_UNROLL * n_main + u, u % 2, carry)
            finish(rem % 2, carry)


def _attn_call(qt, gt, ks, kw, vs, vw, kc, vc, msel, cmask, topk):
    b, hh, _, t = qt.shape
    tq = Q_TILE
    nkb = t // tq
    nc = kc.shape[2]
    nsel = msel.shape[0]
    per_q = lambda i, j, k: (i, j, 0, k)
    per_bh4 = lambda i, j, k: (i, j, 0, 0)
    per_bh5 = lambda i, j, k: (i, j, 0, 0, 0)
    return pl.pallas_call(
        functools.partial(_attn_kernel, topk=topk),
        grid=(b, hh, nkb),
        in_specs=[
            pl.BlockSpec((None, None, GQA_GROUP * HEAD_DIM, tq), per_q),
            pl.BlockSpec((None, None, GATE_ROWS, tq), per_q),
            pl.BlockSpec((None, None, nkb, tq, AUG_K), per_bh5),
            pl.BlockSpec((None, None, nkb, tq, AUG_K), per_bh5),
            pl.BlockSpec((None, None, nkb, V_ROWS, tq), per_bh5),
            pl.BlockSpec((None, None, nkb, V_ROWS, tq), per_bh5),
            pl.BlockSpec((None, None, nc, AUG_K), per_bh4),
            pl.BlockSpec((None, None, HEAD_DIM, nc), per_bh4),
            pl.BlockSpec((nsel, nc), lambda i, j, k: (0, 0)),
            pl.BlockSpec((nc, tq), lambda i, j, k: (0, 0)),
        ],
        out_specs=pl.BlockSpec((None, GQA_GROUP * HEAD_DIM, tq), lambda i, j, k: (i, j, k)),
        out_shape=jax.ShapeDtypeStruct((b, D_ATTN, t), F32),
        scratch_shapes=[pltpu.VMEM((AUG_K, GQA_GROUP * tq), BF16), pltpu.VMEM((V_ROWS, GQA_GROUP * tq), F32),
                        pltpu.VMEM((V_ROWS, GQA_GROUP * tq), F32),
                        pltpu.VMEM((2, tq, GQA_GROUP * tq), F32),
                        pltpu.VMEM((tq, GQA_GROUP * tq), F32), pltpu.VMEM((GQA_GROUP * HEAD_DIM, tq), F32),
                        pltpu.VMEM((nsel, tq), F32), pltpu.VMEM((nsel, tq), F32),
                        pltpu.SMEM((nkb + 1,), jnp.int32)],
        compiler_params=pltpu.CompilerParams(dimension_semantics=("arbitrary", "arbitrary", "arbitrary"),
                                             vmem_limit_bytes=VMEM_LIMIT),
    )(qt, gt, ks, kw, vs, vw, kc, vc, msel, cmask)


def _ffn_kernel(x_ref, oa_ref, mc_ref, ga_ref, wo_ref, gf_ref, wg_ref, wu_ref, cw_ref, cb_ref, wd_ref, gl_ref,
                o_ref, gbuf, ybuf, *, tiles_per_seq):
    i = pl.program_id(0)
    tm = x_ref.shape[0]

    @pl.when(i % tiles_per_seq == 0)
    def _():
        gbuf[0:8, :] = jnp.zeros((8, D_FF), F32)

    @pl.when(i % tiles_per_seq != 0)
    def _():
        gbuf[0:8, :] = gbuf[SUB_ROWS:SUB_ROWS + 8, :]

    subs = [slice(r0, r0 + SUB_ROWS) for r0 in range(0, tm, SUB_ROWS)]
    x1s, h2s = [], []
    for rows in subs:
        oat = oa_ref[:, rows]
        mat = (oat * lax.rsqrt(jnp.mean(oat * oat, axis=0, keepdims=True) + EPS) * ga_ref[...]).astype(BF16)
        x1 = (x_ref[rows, :] + lax.dot_general(mat, wo_ref[0:D_ATTN, :], (((0,), (0,)), ((), ())),
                                                preferred_element_type=F32)
              + jnp.dot(mc_ref[rows, :], wo_ref[D_ATTN:D_MODEL, :], preferred_element_type=F32))
        x1s.append(x1)
        h2s.append(_rms(x1, gf_ref[...]).astype(BF16))
    for rows, x1, h2 in zip(subs, x1s, h2s):
        if rows.start > 0:
            gbuf[0:8, :] = gbuf[SUB_ROWS:SUB_ROWS + 8, :]
        for c in range(D_FF // FF_CHUNK):
            cs = slice(c * FF_CHUNK, (c + 1) * FF_CHUNK)
            gpre = jnp.dot(h2, wg_ref[:, cs], preferred_element_type=F32)
            up = jnp.dot(h2, wu_ref[:, cs], preferred_element_type=F32)
            gbuf[8:8 + SUB_ROWS, cs] = gpre
            gate = (cw_ref[0:1, cs] * gbuf[6:6 + SUB_ROWS, cs] + cw_ref[1:2, cs] * gbuf[7:7 + SUB_ROWS, cs]
                    + cw_ref[2:3, cs] * gpre + cb_ref[:, cs])
            ybuf[:, cs] = (jax.nn.silu(gate) * up).astype(BF16)
        acc = x1 + jnp.dot(ybuf[...], wd_ref[...], preferred_element_type=F32)
        o_ref[rows, :] = _rms(acc, gl_ref[...])


def _ffn_call(x2, oat, mc, ga, wo, gf, wg, wu, cw, cb, wd, gl):
    n = x2.shape[0]
    seq = oat.shape[2]
    tm = ROW_TILE
    tps = seq // tm
    row = lambda i: (i, 0)
    fix = lambda i: (0, 0)
    once = dict(pipeline_mode=pl.Buffered(1))
    return pl.pallas_call(
        functools.partial(_ffn_kernel, tiles_per_seq=seq // tm),
        grid=(n // tm,),
        in_specs=[
            pl.BlockSpec((tm, D_MODEL), row),
            pl.BlockSpec((None, D_ATTN, tm), lambda i: (i // tps, 0, i % tps)),
            pl.BlockSpec((tm, D_CONV), row),
            pl.BlockSpec((D_ATTN, 1), fix),
            pl.BlockSpec((D_MODEL, D_MODEL), fix, **once),
            pl.BlockSpec((1, D_MODEL), fix),
            pl.BlockSpec((D_MODEL, D_FF), fix, **once),
            pl.BlockSpec((D_MODEL, D_FF), fix, **once),
            pl.BlockSpec((3, D_FF), fix),
            pl.BlockSpec((1, D_FF), fix),
            pl.BlockSpec((D_FF, D_MODEL), fix, **once),
            pl.BlockSpec((1, D_MODEL), fix),
        ],
        out_specs=pl.BlockSpec((tm, D_MODEL), row),
        out_shape=jax.ShapeDtypeStruct((n, D_MODEL), F32),
        scratch_shapes=[pltpu.VMEM((SUB_ROWS + 8, D_FF), F32), pltpu.VMEM((SUB_ROWS, D_FF), BF16)],
        compiler_params=pltpu.CompilerParams(dimension_semantics=("arbitrary",), vmem_limit_bytes=VMEM_LIMIT),
    )(x2, oat, mc, ga, wo, gf, wg, wu, cw, cb, wd, gl)


def _alibi_cols(pos):
    cols = np.zeros((len(pos), AUG_ALIBI_ROWS), np.float32)
    cols[:, 0:3] = (pos % SEL_BLOCK)[:, None]
    cols[:, 3:6] = (pos // SEL_BLOCK)[:, None]
    cols[:, 6] = 1.0
    return cols


def _key_consts(t, with_sel):
    pos = np.arange(t)
    c = np.zeros((t, AUG_K), np.float32)
    c[:, AUG_ALIBI:AUG_SEL] = _alibi_cols(pos)
    if with_sel:
        c[pos, AUG_SEL + pos // SEL_BLOCK] = 1.0
    return jnp.asarray(c, BF16)


def _cmp_consts(nc):
    c = np.zeros((nc, AUG_K), np.float32)
    c[:, AUG_ALIBI:AUG_SEL] = _alibi_cols(np.arange(nc) * CMP_STRIDE + (CMP_BLOCK - 1))
    return jnp.asarray(c, BF16)


def _compress_weights(w_k1, w_v1, pos_k, pos_v, w_k2, w_v2):
    hh, dk, half = N_KV_HEADS, HEAD_DIM, CMP_STRIDE
    w1 = jnp.zeros((half, N_STREAMS, dk, N_STREAMS, 2, CMP_HIDDEN), F32)
    pos = []
    for st in range(N_STREAMS):
        w = (w_k1 if st < hh else w_v1).reshape(2, half, dk, CMP_HIDDEN)
        w1 = w1.at[:, st, :, st, :, :].set(w.transpose(1, 2, 0, 3))
        pos.append((pos_k if st < hh else pos_v).reshape(2, half, dk))
    pos = jnp.stack(pos, axis=2).reshape(2, half * N_STREAMS * dk)
    pos8 = jnp.concatenate([pos, jnp.zeros((6, pos.shape[1]), F32)], axis=0)
    w2 = jnp.stack([jnp.concatenate([w_k2 if st < hh else w_v2, jnp.zeros((CMP_HIDDEN, 128 - dk), F32)], axis=1)
                    for st in range(N_STREAMS)])
    w1 = w1.reshape(half * N_STREAMS * dk, N_STREAMS * 2 * CMP_HIDDEN)
    return w1.astype(BF16), pos8.astype(BF16), w2.astype(BF16)


def _cmp_limits(nc):
    end = np.arange(nc)[:, None] * CMP_STRIDE + (CMP_BLOCK - 1)
    off = np.arange(Q_TILE)[None, :]
    return jnp.asarray(end - off, jnp.int32)


def _sel_map_t(t, nc):
    n_cmp = (t - CMP_BLOCK) // CMP_STRIDE + 1
    n_sel = t // SEL_BLOCK
    cs = np.arange(n_cmp)[:, None] * CMP_STRIDE
    ss = np.arange(n_sel)[None, :] * SEL_BLOCK
    ov = np.maximum(0, np.minimum(cs + CMP_BLOCK, ss + SEL_BLOCK) - np.maximum(cs, ss)) / CMP_BLOCK
    m = np.zeros((n_sel, nc), np.float32)
    m[:, :n_cmp] = ov.T
    return jnp.asarray(m, BF16)


def kernel(x, norm_mix_g, w_in, pos_ck, w_ck1, w_ck2, pos_cv, w_cv1, w_cv2, conv_mix_w, norm_out_attn_g,
           norm_out_conv_g, w_out, norm_ffn_g, w_gate, w_up, ffn_conv_w, ffn_conv_b, w_down, norm_final_g):
    b, t, _ = x.shape
    hh, dk = N_KV_HEADS, HEAD_DIM
    assert t % ROW_TILE == 0 and t % Q_TILE == 0 and WINDOW % Q_TILE == 0 and t // SEL_BLOCK <= AUG_K - AUG_SEL
    assert ROW_TILE % SUB_ROWS == 0 and SUB_ROWS % Q_TILE == 0 and (t // SEL_BLOCK) % 16 == 0
    nc = t // CMP_STRIDE
    nsel = t // SEL_BLOCK
    nkb = t // Q_TILE
    depth = w_in.shape[0]
    assert depth == 1
    xx = x
    for l in range(depth):
        wi = w_in[l]
        col = np.cumsum([0, D_ATTN] + [D_KV] * 6 + [3 * N_HEADS_ATTN] + [D_CONV] * 3)
        kv_slabs = [wi[:, col[3 + 2 * br] + hd * dk:col[3 + 2 * br] + (hd + 1) * dk] if part == 0 else
                    wi[:, col[4 + 2 * br] + hd * dk:col[4 + 2 * br] + (hd + 1) * dk]
                    for br in range(2) for hd in range(hh) for part in range(2)]
        gate_cols = [jnp.concatenate([wi[:, col[7] + 12 * hd:col[7] + 12 * (hd + 1)], jnp.zeros((D_MODEL, 4), F32)], axis=1)
                     for hd in range(hh)]
        w_p = jnp.concatenate([wi[:, 0:D_ATTN]] + kv_slabs + [wi[:, col[1]:col[3]], wi[:, col[8]:col[11]]] + gate_cols
                              + [jnp.zeros((D_MODEL, 128 - 2 * GATE_ROWS), F32)], axis=1).astype(BF16)
        assert w_p.shape[1] == PROJ_COLS
        qt, gt, ksa, kwa, vst, vwt, kvc, mixed_conv = _proj_call(
            xx, norm_mix_g[l][None], w_p, conv_mix_w[l], norm_out_conv_g[l][None],
            _key_consts(t, True), _key_consts(t, False))

        kc, vc = _compress_call(kvc, *_compress_weights(w_ck1[l], w_cv1[l], pos_ck[l], pos_cv[l], w_ck2[l], w_cv2[l]),
                                _cmp_consts(nc))

        o_attn = _attn_call(qt, gt, ksa.reshape(b, hh, nkb, Q_TILE, AUG_K), kwa.reshape(b, hh, nkb, Q_TILE, AUG_K),
                            vst, vwt, kc, vc, _sel_map_t(t, nc), _cmp_limits(nc), min(SEL_TOPK, nsel))

        xx = _ffn_call(xx.reshape(b * t, D_MODEL), o_attn, mixed_conv.reshape(b * t, D_CONV),
                       norm_out_attn_g[l][:, None], w_out[l].astype(BF16), norm_ffn_g[l][None], w_gate[l].astype(BF16),
                       w_up[l].astype(BF16), ffn_conv_w[l], ffn_conv_b[l][None], w_down[l].astype(BF16),
                       norm_final_g[None])
    return xx.reshape(b, t, D_MODEL)
```

```python
import functools

import jax
import jax.numpy as jnp
import numpy as np
from jax import lax
from jax.experimental import pallas as pl
from jax.experimental.pallas import tpu as pltpu

F32 = jnp.float32
BF16 = jnp.bfloat16

D_MODEL = 1024
N_KV_HEADS = 2
GQA_GROUP = 4
N_HEADS_ATTN = N_KV_HEADS * GQA_GROUP
HEAD_DIM = 64
D_ATTN = N_HEADS_ATTN * HEAD_DIM
D_KV = N_KV_HEADS * HEAD_DIM
D_CONV = D_MODEL - D_ATTN
CMP_BLOCK = 32
CMP_STRIDE = 16
CMP_HIDDEN = 2 * HEAD_DIM
SEL_BLOCK = 64
SEL_TOPK = 16
WINDOW = 512
D_FF = 2816
EPS = 1e-6
NEG = -1e30
BIG = 1e30

AUG_K = 256
AUG_ALIBI = HEAD_DIM
AUG_ALIBI_ROWS = 16
AUG_SEL = AUG_ALIBI + AUG_ALIBI_ROWS
V_ROWS = 80

ROW_TILE = 1024
SUB_ROWS = 256
Q_TILE = 256
SEL_UNROLL = 6
FF_CHUNK = 256
PROJ_COLS = 2944
VMEM_LIMIT = 56 * 1024 * 1024

LOG2E = 1.4426950408889634


def _bf16_terms(x, n):
    out = []
    for _ in range(n):
        t = float(np.asarray(x, np.float32).astype(jnp.bfloat16).astype(np.float32))
        out.append(t)
        x = x - t
    return tuple(out)


LOG2E_3 = _bf16_terms(LOG2E, 3)


def _rms(x, g):
    return x * lax.rsqrt(jnp.mean(x * x, axis=-1, keepdims=True) + EPS) * g


COL_Q = 0
COL_KV = D_ATTN
COL_CMP = COL_KV + 4 * 2 * HEAD_DIM
COL_B = COL_CMP + 2 * D_KV
COL_C = COL_B + D_CONV
COL_U = COL_C + D_CONV
COL_GATE = COL_U + D_CONV
GATE_ROWS = 16


def _proj_kernel(x_ref, g_ref, w_ref, cw_ref, gc_ref, csel_ref, cwin_ref,
                 qt_ref, gt_ref, ksa_ref, kwa_ref, vst_ref, vwt_ref, kvc_ref, mc_ref, cbuf, cst, *, tiles_per_seq):
    i = pl.program_id(0)
    tm = x_ref.shape[0]
    tq = vst_ref.shape[-1]

    @pl.when(i % tiles_per_seq == 0)
    def _():
        cbuf[0:8, :] = jnp.zeros((8, D_CONV), F32)

    @pl.when(i % tiles_per_seq != 0)
    def _():
        cbuf[0:8, :] = cbuf[SUB_ROWS:SUB_ROWS + 8, :]

    lane = lax.broadcasted_iota(jnp.int32, (SUB_ROWS, 128), 1)
    ones_rows = jnp.where(lax.broadcasted_iota(jnp.int32, (V_ROWS - HEAD_DIM, tq), 0) == 0, 1.0, 0.0).astype(BF16)
    for sub in range(tm // SUB_ROWS):
        r0 = sub * SUB_ROWS
        rows = slice(r0, r0 + SUB_ROWS)
        if sub > 0:
            cbuf[0:8, :] = cbuf[SUB_ROWS:SUB_ROWS + 8, :]
        h = _rms(x_ref[rows, :], g_ref[...])
        p = jnp.dot(h.astype(BF16), w_ref[...], preferred_element_type=F32)

        for hd in range(N_KV_HEADS):
            qs = p[:, COL_Q + hd * 256:COL_Q + (hd + 1) * 256] * (HEAD_DIM ** -0.5 * LOG2E)
            qt_ref[hd, :, rows] = qs.T.astype(BF16)
        gt = jax.nn.sigmoid(p[:, COL_GATE:COL_GATE + 128]).T
        for hd in range(N_KV_HEADS):
            gt_ref[hd, :, rows] = gt[hd * GATE_ROWS:(hd + 1) * GATE_ROWS, :]

        for branch, (ka_ref, vt_ref, c_ref) in enumerate(((ksa_ref, vst_ref, csel_ref), (kwa_ref, vwt_ref, cwin_ref))):
            for hd in range(N_KV_HEADS):
                c0 = COL_KV + (2 * branch + hd) * 128
                slab = p[:, c0:c0 + 128]
                ka_ref[hd, rows, 0:128] = jnp.where(lane < HEAD_DIM, slab.astype(BF16), c_ref[rows, 0:128])
                ka_ref[hd, rows, 128:AUG_K] = c_ref[rows, 128:AUG_K]
                vt = slab.T[HEAD_DIM:128, :].astype(BF16)
                for kt in range(SUB_ROWS // tq):
                    vt_ref[hd, r0 // tq + kt, 0:HEAD_DIM, :] = vt[:, kt * tq:(kt + 1) * tq]
                    vt_ref[hd, r0 // tq + kt, HEAD_DIM:V_ROWS, :] = ones_rows

        nch = SUB_ROWS // CMP_STRIDE
        for part in range(2):
            cst[part] = p[:, COL_CMP + part * D_KV:COL_CMP + (part + 1) * D_KV]
            for tok in range(CMP_STRIDE):
                c0 = (2 * tok + part) * D_KV
                kvc_ref[r0 // CMP_STRIDE:r0 // CMP_STRIDE + nch, c0:c0 + D_KV] = (
                    cst[part, pl.ds(tok, nch, stride=CMP_STRIDE), :].astype(BF16))
        b = p[:, COL_B:COL_B + D_CONV]
        cu = p[:, COL_C:COL_C + D_CONV] * p[:, COL_U:COL_U + D_CONV]
        cbuf[8:8 + SUB_ROWS, :] = cu
        y = cw_ref[0:1, :] * cbuf[6:6 + SUB_ROWS, :] + cw_ref[1:2, :] * cbuf[7:7 + SUB_ROWS, :] + cw_ref[2:3, :] * cu
        mc_ref[rows, :] = _rms(b * y, gc_ref[...]).astype(BF16)


def _proj_call(x3, g, w, cw, gc, csel, cwin):
    bsz, seq, _ = x3.shape
    tm = ROW_TILE
    tq = Q_TILE
    tps = seq // tm
    hh = N_KV_HEADS
    row = lambda i: (i // tps, i % tps, 0)
    fix = lambda i: (0, 0)
    seq_tile = lambda i: (i % tps, 0)
    tok_minor = lambda i: (i // tps, 0, 0, i % tps)
    tok_major = lambda i: (i // tps, 0, i % tps, 0)
    tok_tiles = lambda i: (i // tps, 0, i % tps, 0, 0)
    return pl.pallas_call(
        functools.partial(_proj_kernel, tiles_per_seq=tps),
        grid=(bsz * tps,),
        in_specs=[
            pl.BlockSpec((None, tm, D_MODEL), row),
            pl.BlockSpec((1, D_MODEL), fix),
            pl.BlockSpec((D_MODEL, PROJ_COLS), fix),
            pl.BlockSpec((3, D_CONV), fix),
            pl.BlockSpec((1, D_CONV), fix),
            pl.BlockSpec((tm, AUG_K), seq_tile),
            pl.BlockSpec((tm, AUG_K), seq_tile),
        ],
        out_specs=[
            pl.BlockSpec((None, hh, GQA_GROUP * HEAD_DIM, tm), tok_minor),
            pl.BlockSpec((None, hh, GATE_ROWS, tm), tok_minor),
            pl.BlockSpec((None, hh, tm, AUG_K), tok_major),
            pl.BlockSpec((None, hh, tm, AUG_K), tok_major),
            pl.BlockSpec((None, hh, tm // tq, V_ROWS, tq), tok_tiles),
            pl.BlockSpec((None, hh, tm // tq, V_ROWS, tq), tok_tiles),
            pl.BlockSpec((None, tm // CMP_STRIDE, CMP_STRIDE * 2 * D_KV), row),
            pl.BlockSpec((None, tm, D_CONV), row),
        ],
        out_shape=[
            jax.ShapeDtypeStruct((bsz, hh, GQA_GROUP * HEAD_DIM, seq), BF16),
            jax.ShapeDtypeStruct((bsz, hh, GATE_ROWS, seq), F32),
            jax.ShapeDtypeStruct((bsz, hh, seq, AUG_K), BF16),
            jax.ShapeDtypeStruct((bsz, hh, seq, AUG_K), BF16),
            jax.ShapeDtypeStruct((bsz, hh, seq // tq, V_ROWS, tq), BF16),
            jax.ShapeDtypeStruct((bsz, hh, seq // tq, V_ROWS, tq), BF16),
            jax.ShapeDtypeStruct((bsz, seq // CMP_STRIDE, CMP_STRIDE * 2 * D_KV), BF16),
            jax.ShapeDtypeStruct((bsz, seq, D_CONV), BF16),
        ],
        scratch_shapes=[pltpu.VMEM((SUB_ROWS + 8, D_CONV), F32), pltpu.VMEM((2, SUB_ROWS, D_KV), F32)],
        compiler_params=pltpu.CompilerParams(dimension_semantics=("arbitrary",), vmem_limit_bytes=VMEM_LIMIT),
    )(x3, g, w, cw, gc, csel, cwin)


N_STREAMS = 2 * N_KV_HEADS


def _compress_kernel(x_ref, w1_ref, pos_ref, w2_ref, c_ref, kc_ref, vc_ref, sbuf):
    nc = x_ref.shape[0]
    y = jnp.dot(x_ref[...], w1_ref[...], preferred_element_type=F32)
    pb = jnp.dot(pos_ref[...], w1_ref[...], preferred_element_type=F32)
    rowi = lax.broadcasted_iota(jnp.int32, (nc, 128), 0)
    lane = lax.broadcasted_iota(jnp.int32, (nc, 128), 1)
    sbuf[:, nc:nc + 8, :] = jnp.zeros((N_STREAMS, 8, CMP_HIDDEN), F32)
    for st in range(N_STREAMS):
        c0 = st * 2 * CMP_HIDDEN
        posb = pb[0:1, c0:c0 + CMP_HIDDEN] + pb[1:2, c0 + CMP_HIDDEN:c0 + 2 * CMP_HIDDEN]
        sbuf[st, 0:nc, :] = y[:, c0 + CMP_HIDDEN:c0 + 2 * CMP_HIDDEN]
        hid = y[:, c0:c0 + CMP_HIDDEN] + sbuf[st, 1:nc + 1, :] + posb
        out = jnp.dot(jax.nn.gelu(hid).astype(BF16), w2_ref[st], preferred_element_type=F32)
        out = jnp.where(rowi < nc - 1, out, 0.0)
        if st < N_KV_HEADS:
            kc_ref[st, :, 0:128] = jnp.where(lane < HEAD_DIM, out.astype(BF16), c_ref[:, 0:128])
            kc_ref[st, :, 128:AUG_K] = c_ref[:, 128:AUG_K]
        else:
            vc_ref[st - N_KV_HEADS] = out.T[0:HEAD_DIM, :].astype(BF16)


def _compress_call(xc, w1, pos8, w2, consts):
    b, nc, width = xc.shape
    hh = N_KV_HEADS
    fix2 = lambda i: (0, 0)
    return pl.pallas_call(
        _compress_kernel,
        grid=(b,),
        in_specs=[
            pl.BlockSpec((None, nc, width), lambda i: (i, 0, 0)),
            pl.BlockSpec((width, N_STREAMS * 2 * CMP_HIDDEN), fix2),
            pl.BlockSpec((8, width), fix2),
            pl.BlockSpec((N_STREAMS, CMP_HIDDEN, 128), lambda i: (0, 0, 0)),
            pl.BlockSpec((nc, AUG_K), fix2),
        ],
        out_specs=[
            pl.BlockSpec((None, hh, nc, AUG_K), lambda i: (i, 0, 0, 0)),
            pl.BlockSpec((None, hh, HEAD_DIM, nc), lambda i: (i, 0, 0, 0)),
        ],
        out_shape=[
            jax.ShapeDtypeStruct((b, hh, nc, AUG_K), BF16),
            jax.ShapeDtypeStruct((b, hh, HEAD_DIM, nc), BF16),
        ],
        scratch_shapes=[pltpu.VMEM((N_STREAMS, nc + 8, CMP_HIDDEN), F32)],
        compiler_params=pltpu.CompilerParams(dimension_semantics=("arbitrary",), vmem_limit_bytes=VMEM_LIMIT),
    )(xc, w1, pos8, w2, consts)


def _attn_kernel(qt_ref, gt_ref, ks_ref, kw_ref, vs_ref, vw_ref, kc_ref, vc_ref, msel_ref, cmask_ref, o_ref,
                 qaug, acc_ref, accw_ref, sbuf, wsbuf, ocmp_ref, imp_ref, selm_ref, klist, *, topk):
    h = pl.program_id(1)
    qi = pl.program_id(2)
    tq = qt_ref.shape[1]
    r = GQA_GROUP * tq
    nc = kc_ref.shape[0]
    nsel = msel_ref.shape[0]
    nkb = ks_ref.shape[0]
    t0 = qi * tq

    lane16 = lax.broadcasted_iota(jnp.int32, (AUG_ALIBI_ROWS, r), 1)
    sub16 = lax.broadcasted_iota(jnp.int32, (AUG_ALIBI_ROWS, r), 0)
    gl = lane16 // tq
    off = (lane16 % tq).astype(F32)
    base = jnp.where(h == 0, 0.5, 0.03125).astype(F32)
    slope = jnp.where(gl == 0, base, jnp.where(gl == 1, base * 0.5, jnp.where(gl == 2, base * 0.25, base * 0.125)))
    blk0 = (t0 // SEL_BLOCK).astype(F32)
    c3 = jnp.where(sub16 % 3 == 0, LOG2E_3[0], jnp.where(sub16 % 3 == 1, LOG2E_3[1], LOG2E_3[2]))
    arow = jnp.where(sub16 < 3, slope * c3,
                     jnp.where(sub16 < 6, 64.0 * slope * c3,
                               jnp.where(sub16 == 6, -slope * LOG2E * (64.0 * blk0 + off), 0.0)))
    for g in range(GQA_GROUP):
        qaug[0:HEAD_DIM, g * tq:(g + 1) * tq] = qt_ref[g * HEAD_DIM:(g + 1) * HEAD_DIM, :]
    qaug[AUG_ALIBI:AUG_SEL, :] = arow.astype(BF16)
    qaug[AUG_SEL:AUG_K, :] = jnp.zeros((AUG_K - AUG_SEL, r), BF16)

    heads = [slice(g * tq, (g + 1) * tq) for g in range(GQA_GROUP)]
    q_i = lax.broadcasted_iota(jnp.int32, (tq, tq), 1)
    key_i = lax.broadcasted_iota(jnp.int32, (tq, tq), 0)
    causal = key_i <= q_i
    nwin = WINDOW // tq

    def win_scores(w, cols):
        kb = qi - nwin + w
        s = jnp.dot(kw_ref[jnp.maximum(kb, 0)], qaug[:, cols], preferred_element_type=F32)
        if w == 0:
            return jnp.where((key_i > q_i) & (kb >= 0), s, NEG)
        if w == nwin:
            return jnp.where(causal, s, NEG)
        return jnp.where(kb >= 0, s, NEG)

    def softmax_tile(s, m_old):
        m_new = jnp.maximum(m_old, jnp.max(s, axis=0, keepdims=True))
        return m_new, jnp.exp2(s - m_new).astype(BF16), jnp.exp2(m_old - m_new)

    def win_update(w, cols, s, m_old):
        m_new, p, alpha = softmax_tile(s, m_old)
        pv_w = jnp.dot(vw_ref[jnp.maximum(qi - nwin + w, 0)], p, preferred_element_type=F32)
        accw_ref[:, cols] = pv_w if w == 0 else accw_ref[:, cols] * alpha + pv_w
        return m_new

    mask_c = cmask_ref[...] <= t0
    cmp_s = [jnp.dot(kc_ref[...], qaug[:, cols], preferred_element_type=F32) for cols in heads]
    win_s = [win_scores(0, cols) for cols in heads]
    p_sum = jnp.zeros((nc, tq), F32)
    for g, cols in enumerate(heads):
        sc = jnp.where(mask_c, cmp_s[g], NEG)
        m_c = jnp.max(sc, axis=0, keepdims=True)
        e_c = jnp.exp2(sc - m_c)
        l_c = jnp.sum(e_c, axis=0, keepdims=True)
        p_c = e_c * jnp.where(m_c > 0.5 * NEG, 1.0 / l_c, 0.0)
        ocmp_ref[g * HEAD_DIM:(g + 1) * HEAD_DIM, :] = jnp.dot(vc_ref[...], p_c.astype(BF16),
                                                               preferred_element_type=F32)
        p_sum = p_sum + p_c

    m_w = [jnp.full((1, tq), NEG, F32)] * GQA_GROUP
    for w in range(nwin):
        if w + 1 < nwin:
            nxt = [win_scores(w + 1, cols) for cols in heads]
        else:
            nxt = None
            for cols in heads:
                wsbuf[:, cols] = win_scores(w + 1, cols)
        for g, cols in enumerate(heads):
            m_w[g] = win_update(w, cols, win_s[g], m_w[g])
        win_s = nxt
        if w == 0:
            p1 = p_sum.astype(BF16)
            p2 = (p_sum - p1.astype(F32)).astype(BF16)
            msel = msel_ref[...]
            imp = (jnp.dot(msel, p1, preferred_element_type=F32)
                   + jnp.dot(msel, p2, preferred_element_type=F32))

    jj =lax.broadcasted_iota(jnp.int32, (nsel, tq), 0)
    jt = (t0 + lax.broadcasted_iota(jnp.int32, (nsel, tq), 1)) // SEL_BLOCK
    imp = jnp.where((jj == 0) | (jj == jt) | (jj == jt - 1), BIG, imp)
    imp = jnp.where(jj > jt, NEG, imp)
    imp_ref[...] = imp
    bpt = tq // SEL_BLOCK
    sub8 = lax.broadcasted_iota(jnp.int32, (8, tq), 0)

    def rank_select(nblk):
        groups = [imp_ref[8 * gi:8 * gi + 8, :] for gi in range(nblk // 8)]
        cnts = [jnp.zeros((8, tq), jnp.int32) for _ in groups]
        for jp in range(nblk):
            rowv = jnp.broadcast_to(imp_ref[jp:jp + 1, :], (8, tq))
            for gi, grp in enumerate(groups):
                if 8 * gi > jp:
                    beats = rowv >= grp
                elif 8 * gi + 7 <= jp:
                    beats = rowv > grp
                else:
                    beats = (rowv > grp) | ((rowv == grp) & (sub8 + 8 * gi > jp))
                cnts[gi] = cnts[gi] + jnp.where(beats, 1, 0)
        selm = [jnp.where(c < topk, 1.0, 0.0) for c in cnts] + [jnp.zeros((nsel - nblk, tq), F32)] * (nblk < nsel)
        selm = jnp.concatenate(selm, axis=0)
        selm_ref[...] = selm
        selbias = jnp.where(selm > 0.0, 0.0, NEG).astype(BF16)
        for g in range(GQA_GROUP):
            qaug[AUG_SEL:AUG_SEL + nsel, g * tq:(g + 1) * tq] = selbias

    for idx in range(nsel // 16):
        pl.when((qi * bpt) // 16 == idx)(functools.partial(rank_select, 16 * (idx + 1)))

    for cols in heads:
        sbuf[0, :, cols] = jnp.dot(ks_ref[0], qaug[:, cols], preferred_element_type=F32)
    for g, cols in enumerate(heads):
        win_update(nwin, cols, wsbuf[:, cols], m_w[g])

    n_use = jnp.int32(0)
    for gi in range(nsel // 8):
        hit = jnp.max(selm_ref[8 * gi:8 * gi + 8, :], axis=1, keepdims=True)
        for part in range(8 // bpt):
            kb = gi * (8 // bpt) + part
            if kb < nkb - 1:
                klist[n_use] = kb
                used = (jnp.max(hit[part * bpt:(part + 1) * bpt, :]) > 0.0) & (kb < qi)
                n_use = n_use + used.astype(jnp.int32)
    klist[n_use] = qi

    acc_ref[...] = jnp.zeros(acc_ref.shape, F32)

    def step(j, cur, m_old):
        k_next = ks_ref[klist[j + 1]]
        v_cur = vs_ref[klist[j]]
        for g in range(GQA_GROUP):
            cols = slice(g * tq, (g + 1) * tq)
            sbuf[1 - cur, :, cols] = jnp.dot(k_next, qaug[:, cols], preferred_element_type=F32)
        m_news = []
        for g in range(GQA_GROUP):
            cols = slice(g * tq, (g + 1) * tq)
            m_new, p, alpha = softmax_tile(sbuf[cur, :, cols], m_old[:, cols])
            acc_ref[:, cols] = acc_ref[:, cols] * alpha + jnp.dot(v_cur, p, preferred_element_type=F32)
            m_news.append(m_new)
        return jnp.concatenate(m_news, axis=1)

    def finish(cur, m_old):
        v_last = vs_ref[qi]
        gt = gt_ref[...]
        for g, cols in enumerate(heads):
            _, p_last, alpha_last = softmax_tile(jnp.where(causal, sbuf[cur, :, cols], NEG), m_old[:, cols])
            acc = acc_ref[:, cols] * alpha_last + jnp.dot(v_last, p_last, preferred_element_type=F32)
            acc_w = accw_ref[:, cols]
            o_sel = acc[0:HEAD_DIM, :] * (1.0 / acc[HEAD_DIM:HEAD_DIM + 1, :])
            o_win = acc_w[0:HEAD_DIM, :] * (1.0 / acc_w[HEAD_DIM:HEAD_DIM + 1, :])
            o_ref[g * HEAD_DIM:(g + 1) * HEAD_DIM, :] = (gt[3 * g:3 * g + 1, :] * ocmp_ref[g * HEAD_DIM:(g + 1) * HEAD_DIM, :]
                                                         + gt[3 * g + 1:3 * g + 2, :] * o_sel
                                                         + gt[3 * g + 2:3 * g + 3, :] * o_win)

    def unrolled(i, carry):
        for u in range(SEL_UNROLL):
            carry = step(SEL_UNROLL * i + u, u % 2, carry)
        return carry

    carry0 = jnp.full((1, r), NEG, F32)
    n_main = n_use // SEL_UNROLL
    carry_main = lax.fori_loop(0, n_main, unrolled, carry0)
    for rem in range(SEL_UNROLL):
        @pl.when(n_use % SEL_UNROLL == rem)
        def _(rem=rem):
            carry = carry_main
            for u in range(rem):
                carry = step(SEL_UNROLL * n_main + u, u % 2, carry)
            finish(rem % 2, carry)


def _attn_call(qt, gt, ks, kw, vs, vw, kc, vc, msel, cmask, topk):
    b, hh, _, t = qt.shape
    tq = Q_TILE
    nkb = t // tq
    nc = kc.shape[2]
    nsel = msel.shape[0]
    per_q = lambda i, j, k: (i, j, 0, k)
    per_bh4 = lambda i, j, k: (i, j, 0, 0)
    per_bh5 = lambda i, j, k: (i, j, 0, 0, 0)
    return pl.pallas_call(
        functools.partial(_attn_kernel, topk=topk),
        grid=(b, hh, nkb),
        in_specs=[
            pl.BlockSpec((None, None, GQA_GROUP * HEAD_DIM, tq), per_q),
            pl.BlockSpec((None, None, GATE_ROWS, tq), per_q),
            pl.BlockSpec((None, None, nkb, tq, AUG_K), per_bh5),
            pl.BlockSpec((None, None, nkb, tq, AUG_K), per_bh5),
            pl.BlockSpec((None, None, nkb, V_ROWS, tq), per_bh5),
            pl.BlockSpec((None, None, nkb, V_ROWS, tq), per_bh5),
            pl.BlockSpec((None, None, nc, AUG_K), per_bh4),
            pl.BlockSpec((None, None, HEAD_DIM, nc), per_bh4),
            pl.BlockSpec((nsel, nc), lambda i, j, k: (0, 0)),
            pl.BlockSpec((nc, tq), lambda i, j, k: (0, 0)),
        ],
        out_specs=pl.BlockSpec((None, GQA_GROUP * HEAD_DIM, tq), lambda i, j, k: (i, j, k)),
        out_shape=jax.ShapeDtypeStruct((b, D_ATTN, t), F32),
        scratch_shapes=[pltpu.VMEM((AUG_K, GQA_GROUP * tq), BF16), pltpu.VMEM((V_ROWS, GQA_GROUP * tq), F32),
                        pltpu.VMEM((V_ROWS, GQA_GROUP * tq), F32),
                        pltpu.VMEM((2, tq, GQA_GROUP * tq), F32),
                        pltpu.VMEM((tq, GQA_GROUP * tq), F32), pltpu.VMEM((GQA_GROUP * HEAD_DIM, tq), F32),
                        pltpu.VMEM((nsel, tq), F32), pltpu.VMEM((nsel, tq), F32),
                        pltpu.SMEM((nkb + 1,), jnp.int32)],
        compiler_params=pltpu.CompilerParams(dimension_semantics=("arbitrary", "arbitrary", "arbitrary"),
                                             vmem_limit_bytes=VMEM_LIMIT),
    )(qt, gt, ks, kw, vs, vw, kc, vc, msel, cmask)


def _ffn_kernel(x_ref, oa_ref, mc_ref, ga_ref, wo_ref, gf_ref, wg_ref, wu_ref, cw_ref, cb_ref, wd_ref, gl_ref,
                o_ref, gbuf, ybuf, *, tiles_per_seq):
    i = pl.program_id(0)
    tm = x_ref.shape[0]

    @pl.when(i % tiles_per_seq == 0)
    def _():
        gbuf[0:8, :] = jnp.zeros((8, D_FF), F32)

    @pl.when(i % tiles_per_seq != 0)
    def _():
        gbuf[0:8, :] = gbuf[SUB_ROWS:SUB_ROWS + 8, :]

    subs = [slice(r0, r0 + SUB_ROWS) for r0 in range(0, tm, SUB_ROWS)]
    x1s, h2s = [], []
    for rows in subs:
        oat = oa_ref[:, rows]
        mat = (oat * lax.rsqrt(jnp.mean(oat * oat, axis=0, keepdims=True) + EPS) * ga_ref[...]).astype(BF16)
        x1 = (x_ref[rows, :] + lax.dot_general(mat, wo_ref[0:D_ATTN, :], (((0,), (0,)), ((), ())),
                                                preferred_element_type=F32)
              + jnp.dot(mc_ref[rows, :], wo_ref[D_ATTN:D_MODEL, :], preferred_element_type=F32))
        x1s.append(x1)
        h2s.append(_rms(x1, gf_ref[...]).astype(BF16))
    for rows, x1, h2 in zip(subs, x1s, h2s):
        if rows.start > 0:
            gbuf[0:8, :] = gbuf[SUB_ROWS:SUB_ROWS + 8, :]
        for c in range(D_FF // FF_CHUNK):
            cs = slice(c * FF_CHUNK, (c + 1) * FF_CHUNK)
            gpre = jnp.dot(h2, wg_ref[:, cs], preferred_element_type=F32)
            up = jnp.dot(h2, wu_ref[:, cs], preferred_element_type=F32)
            gbuf[8:8 + SUB_ROWS, cs] = gpre
            gate = (cw_ref[0:1, cs] * gbuf[6:6 + SUB_ROWS, cs] + cw_ref[1:2, cs] * gbuf[7:7 + SUB_ROWS, cs]
                    + cw_ref[2:3, cs] * gpre + cb_ref[:, cs])
            ybuf[:, cs] = (jax.nn.silu(gate) * up).astype(BF16)
        acc = x1 + jnp.dot(ybuf[...], wd_ref[...], preferred_element_type=F32)
        o_ref[rows, :] = _rms(acc, gl_ref[...])


def _ffn_call(x2, oat, mc, ga, wo, gf, wg, wu, cw, cb, wd, gl):
    n = x2.shape[0]
    seq = oat.shape[2]
    tm = ROW_TILE
    tps = seq // tm
    row = lambda i: (i, 0)
    fix = lambda i: (0, 0)
    once = dict(pipeline_mode=pl.Buffered(1))
    return pl.pallas_call(
        functools.partial(_ffn_kernel, tiles_per_seq=seq // tm),
        grid=(n // tm,),
        in_specs=[
            pl.BlockSpec((tm, D_MODEL), row),
            pl.BlockSpec((None, D_ATTN, tm), lambda i: (i // tps, 0, i % tps)),
            pl.BlockSpec((tm, D_CONV), row),
            pl.BlockSpec((D_ATTN, 1), fix),
            pl.BlockSpec((D_MODEL, D_MODEL), fix, **once),
            pl.BlockSpec((1, D_MODEL), fix),
            pl.BlockSpec((D_MODEL, D_FF), fix, **once),
            pl.BlockSpec((D_MODEL, D_FF), fix, **once),
            pl.BlockSpec((3, D_FF), fix),
            pl.BlockSpec((1, D_FF), fix),
            pl.BlockSpec((D_FF, D_MODEL), fix, **once),
            pl.BlockSpec((1, D_MODEL), fix),
        ],
        out_specs=pl.BlockSpec((tm, D_MODEL), row),
        out_shape=jax.ShapeDtypeStruct((n, D_MODEL), F32),
        scratch_shapes=[pltpu.VMEM((SUB_ROWS + 8, D_FF), F32), pltpu.VMEM((SUB_ROWS, D_FF), BF16)],
        compiler_params=pltpu.CompilerParams(dimension_semantics=("arbitrary",), vmem_limit_bytes=VMEM_LIMIT),
    )(x2, oat, mc, ga, wo, gf, wg, wu, cw, cb, wd, gl)


def _alibi_cols(pos):
    cols = np.zeros((len(pos), AUG_ALIBI_ROWS), np.float32)
    cols[:, 0:3] = (pos % SEL_BLOCK)[:, None]
    cols[:, 3:6] = (pos // SEL_BLOCK)[:, None]
    cols[:, 6] = 1.0
    return cols


def _key_consts(t, with_sel):
    pos = np.arange(t)
    c = np.zeros((t, AUG_K), np.float32)
    c[:, AUG_ALIBI:AUG_SEL] = _alibi_cols(pos)
    if with_sel:
        c[pos, AUG_SEL + pos // SEL_BLOCK] = 1.0
    return jnp.asarray(c, BF16)


def _cmp_consts(nc):
    c = np.zeros((nc, AUG_K), np.float32)
    c[:, AUG_ALIBI:AUG_SEL] = _alibi_cols(np.arange(nc) * CMP_STRIDE + (CMP_BLOCK - 1))
    return jnp.asarray(c, BF16)


def _compress_weights(w_k1, w_v1, pos_k, pos_v, w_k2, w_v2):
    hh, dk, half = N_KV_HEADS, HEAD_DIM, CMP_STRIDE
    w1 = jnp.zeros((half, N_STREAMS, dk, N_STREAMS, 2, CMP_HIDDEN), F32)
    pos = []
    for st in range(N_STREAMS):
        w = (w_k1 if st < hh else w_v1).reshape(2, half, dk, CMP_HIDDEN)
        w1 = w1.at[:, st, :, st, :, :].set(w.transpose(1, 2, 0, 3))
        pos.append((pos_k if st < hh else pos_v).reshape(2, half, dk))
    pos = jnp.stack(pos, axis=2).reshape(2, half * N_STREAMS * dk)
    pos8 = jnp.concatenate([pos, jnp.zeros((6, pos.shape[1]), F32)], axis=0)
    w2 = jnp.stack([jnp.concatenate([w_k2 if st < hh else w_v2, jnp.zeros((CMP_HIDDEN, 128 - dk), F32)], axis=1)
                    for st in range(N_STREAMS)])
    w1 = w1.reshape(half * N_STREAMS * dk, N_STREAMS * 2 * CMP_HIDDEN)
    return w1.astype(BF16), pos8.astype(BF16), w2.astype(BF16)


def _cmp_limits(nc):
    end = np.arange(nc)[:, None] * CMP_STRIDE + (CMP_BLOCK - 1)
    off = np.arange(Q_TILE)[None, :]
    return jnp.asarray(end - off, jnp.int32)


def _sel_map_t(t, nc):
    n_cmp = (t - CMP_BLOCK) // CMP_STRIDE + 1
    n_sel = t // SEL_BLOCK
    cs = np.arange(n_cmp)[:, None] * CMP_STRIDE
    ss = np.arange(n_sel)[None, :] * SEL_BLOCK
    ov = np.maximum(0, np.minimum(cs + CMP_BLOCK, ss + SEL_BLOCK) - np.maximum(cs, ss)) / CMP_BLOCK
    m = np.zeros((n_sel, nc), np.float32)
    m[:, :n_cmp] = ov.T
    return jnp.asarray(m, BF16)


def kernel(x, norm_mix_g, w_in, pos_ck, w_ck1, w_ck2, pos_cv, w_cv1, w_cv2, conv_mix_w, norm_out_attn_g,
           norm_out_conv_g, w_out, norm_ffn_g, w_gate, w_up, ffn_conv_w, ffn_conv_b, w_down, norm_final_g):
    b, t, _ = x.shape
    hh, dk = N_KV_HEADS, HEAD_DIM
    assert t % ROW_TILE == 0 and t % Q_TILE == 0 and WINDOW % Q_TILE == 0 and t // SEL_BLOCK <= AUG_K - AUG_SEL
    assert ROW_TILE % SUB_ROWS == 0 and SUB_ROWS % Q_TILE == 0 and (t // SEL_BLOCK) % 16 == 0
    nc = t // CMP_STRIDE
    nsel = t // SEL_BLOCK
    nkb = t // Q_TILE
    depth = w_in.shape[0]
    assert depth == 1
    xx = x
    for l in range(depth):
        wi = w_in[l]
        col = np.cumsum([0, D_ATTN] + [D_KV] * 6 + [3 * N_HEADS_ATTN] + [D_CONV] * 3)
        kv_slabs = [wi[:, col[3 + 2 * br] + hd * dk:col[3 + 2 * br] + (hd + 1) * dk] if part == 0 else
                    wi[:, col[4 + 2 * br] + hd * dk:col[4 + 2 * br] + (hd + 1) * dk]
                    for br in range(2) for hd in range(hh) for part in range(2)]
        gate_cols = [jnp.concatenate([wi[:, col[7] + 12 * hd:col[7] + 12 * (hd + 1)], jnp.zeros((D_MODEL, 4), F32)], axis=1)
                     for hd in range(hh)]
        w_p = jnp.concatenate([wi[:, 0:D_ATTN]] + kv_slabs + [wi[:, col[1]:col[3]], wi[:, col[8]:col[11]]] + gate_cols
                              + [jnp.zeros((D_MODEL, 128 - 2 * GATE_ROWS), F32)], axis=1).astype(BF16)
        assert w_p.shape[1] == PROJ_COLS
        qt, gt, ksa, kwa, vst, vwt, kvc, mixed_conv = _proj_call(
            xx, norm_mix_g[l][None], w_p, conv_mix_w[l], norm_out_conv_g[l][None],
            _key_consts(t, True), _key_consts(t, False))

        kc, vc = _compress_call(kvc, *_compress_weights(w_ck1[l], w_cv1[l], pos_ck[l], pos_cv[l], w_ck2[l], w_cv2[l]),
                                _cmp_consts(nc))

        o_attn = _attn_call(qt, gt, ksa.reshape(b, hh, nkb, Q_TILE, AUG_K), kwa.reshape(b, hh, nkb, Q_TILE, AUG_K),
                            vst, vwt, kc, vc, _sel_map_t(t, nc), _cmp_limits(nc), min(SEL_TOPK, nsel))

        xx = _ffn_call(xx.reshape(b * t, D_MODEL), o_attn, mixed_conv.reshape(b * t, D_CONV),
                       norm_out_attn_g[l][:, None], w_out[l].astype(BF16), norm_ffn_g[l][None], w_gate[l].astype(BF16),
                       w_up[l].astype(BF16), ffn_conv_w[l], ffn_conv_b[l][None], w_down[l].astype(BF16),
                       norm_final_g[None])
    return xx.reshape(b, t, D_MODEL)
```

```python
import functools

import jax
import jax.numpy as jnp
import numpy as np
from jax import lax
from jax.experimental import pallas as pl
from jax.experimental.pallas import tpu as pltpu

F32 = jnp.float32
BF16 = jnp.bfloat16

D_MODEL = 1024
N_KV_HEADS = 2
GQA_GROUP = 4
N_HEADS_ATTN = N_KV_HEADS * GQA_GROUP
HEAD_DIM = 64
D_ATTN = N_HEADS_ATTN * HEAD_DIM
D_KV = N_KV_HEADS * HEAD_DIM
D_CONV = D_MODEL - D_ATTN
CMP_BLOCK = 32
CMP_STRIDE = 16
CMP_HIDDEN = 2 * HEAD_DIM
SEL_BLOCK = 64
SEL_TOPK = 16
WINDOW = 512
D_FF = 2816
EPS = 1e-6
NEG = -1e30
BIG = 1e30

AUG_K = 256
AUG_ALIBI = HEAD_DIM
AUG_ALIBI_ROWS = 16
AUG_SEL = AUG_ALIBI + AUG_ALIBI_ROWS
V_ROWS = 80

ROW_TILE = 1024
SUB_ROWS = 256
Q_TILE = 256
SEL_UNROLL = 8
FF_CHUNK = 256
PROJ_COLS = 2944
VMEM_LIMIT = 56 * 1024 * 1024

LOG2E = 1.4426950408889634


def _bf16_terms(x, n):
    out = []
    for _ in range(n):
        t = float(np.asarray(x, np.float32).astype(jnp.bfloat16).astype(np.float32))
        out.append(t)
        x = x - t
    return tuple(out)


LOG2E_3 = _bf16_terms(LOG2E, 3)


def _rms(x, g):
    return x * lax.rsqrt(jnp.mean(x * x, axis=-1, keepdims=True) + EPS) * g


COL_Q = 0
COL_KV = D_ATTN
COL_CMP = COL_KV + 4 * 2 * HEAD_DIM
COL_B = COL_CMP + 2 * D_KV
COL_C = COL_B + D_CONV
COL_U = COL_C + D_CONV
COL_GATE = COL_U + D_CONV
GATE_ROWS = 16


def _proj_kernel(x_ref, g_ref, w_ref, cw_ref, gc_ref, csel_ref, cwin_ref,
                 qt_ref, gt_ref, ksa_ref, kwa_ref, vst_ref, vwt_ref, kvc_ref, mc_ref, cbuf, cst, *, tiles_per_seq):
    i = pl.program_id(0)
    tm = x_ref.shape[0]
    tq = vst_ref.shape[-1]

    @pl.when(i % tiles_per_seq == 0)
    def _():
        cbuf[0:8, :] = jnp.zeros((8, D_CONV), F32)

    @pl.when(i % tiles_per_seq != 0)
    def _():
        cbuf[0:8, :] = cbuf[SUB_ROWS:SUB_ROWS + 8, :]

    lane = lax.broadcasted_iota(jnp.int32, (SUB_ROWS, 128), 1)
    ones_rows = jnp.where(lax.broadcasted_iota(jnp.int32, (V_ROWS - HEAD_DIM, tq), 0) == 0, 1.0, 0.0).astype(BF16)
    for sub in range(tm // SUB_ROWS):
        r0 = sub * SUB_ROWS
        rows = slice(r0, r0 + SUB_ROWS)
        if sub > 0:
            cbuf[0:8, :] = cbuf[SUB_ROWS:SUB_ROWS + 8, :]
        h = _rms(x_ref[rows, :], g_ref[...])
        p = jnp.dot(h.astype(BF16), w_ref[...], preferred_element_type=F32)

        for hd in range(N_KV_HEADS):
            qs = p[:, COL_Q + hd * 256:COL_Q + (hd + 1) * 256] * (HEAD_DIM ** -0.5 * LOG2E)
            qt_ref[hd, :, rows] = qs.T.astype(BF16)
        gt = jax.nn.sigmoid(p[:, COL_GATE:COL_GATE + 128]).T
        for hd in range(N_KV_HEADS):
            gt_ref[hd, :, rows] = gt[hd * GATE_ROWS:(hd + 1) * GATE_ROWS, :]

        for branch, (ka_ref, vt_ref, c_ref) in enumerate(((ksa_ref, vst_ref, csel_ref), (kwa_ref, vwt_ref, cwin_ref))):
            for hd in range(N_KV_HEADS):
                c0 = COL_KV + (2 * branch + hd) * 128
                slab = p[:, c0:c0 + 128]
                ka_ref[hd, rows, 0:128] = jnp.where(lane < HEAD_DIM, slab.astype(BF16), c_ref[rows, 0:128])
                ka_ref[hd, rows, 128:AUG_K] = c_ref[rows, 128:AUG_K]
                vt = slab.T[HEAD_DIM:128, :].astype(BF16)
                for kt in range(SUB_ROWS // tq):
                    vt_ref[hd, r0 // tq + kt, 0:HEAD_DIM, :] = vt[:, kt * tq:(kt + 1) * tq]
                    vt_ref[hd, r0 // tq + kt, HEAD_DIM:V_ROWS, :] = ones_rows

        nch = SUB_ROWS // CMP_STRIDE
        for part in range(2):
            cst[part] = p[:, COL_CMP + part * D_KV:COL_CMP + (part + 1) * D_KV]
            for tok in range(CMP_STRIDE):
                c0 = (2 * tok + part) * D_KV
                kvc_ref[r0 // CMP_STRIDE:r0 // CMP_STRIDE + nch, c0:c0 + D_KV] = (
                    cst[part, pl.ds(tok, nch, stride=CMP_STRIDE), :].astype(BF16))
        b = p[:, COL_B:COL_B + D_CONV]
        cu = p[:, COL_C:COL_C + D_CONV] * p[:, COL_U:COL_U + D_CONV]
        cbuf[8:8 + SUB_ROWS, :] = cu
        y = cw_ref[0:1, :] * cbuf[6:6 + SUB_ROWS, :] + cw_ref[1:2, :] * cbuf[7:7 + SUB_ROWS, :] + cw_ref[2:3, :] * cu
        mc_ref[rows, :] = _rms(b * y, gc_ref[...]).astype(BF16)


def _proj_call(x3, g, w, cw, gc, csel, cwin):
    bsz, seq, _ = x3.shape
    tm = ROW_TILE
    tq = Q_TILE
    tps = seq // tm
    hh = N_KV_HEADS
    row = lambda i: (i // tps, i % tps, 0)
    fix = lambda i: (0, 0)
    seq_tile = lambda i: (i % tps, 0)
    tok_minor = lambda i: (i // tps, 0, 0, i % tps)
    tok_major = lambda i: (i // tps, 0, i % tps, 0)
    tok_tiles = lambda i: (i // tps, 0, i % tps, 0, 0)
    return pl.pallas_call(
        functools.partial(_proj_kernel, tiles_per_seq=tps),
        grid=(bsz * tps,),
        in_specs=[
            pl.BlockSpec((None, tm, D_MODEL), row),
            pl.BlockSpec((1, D_MODEL), fix),
            pl.BlockSpec((D_MODEL, PROJ_COLS), fix),
            pl.BlockSpec((3, D_CONV), fix),
            pl.BlockSpec((1, D_CONV), fix),
            pl.BlockSpec((tm, AUG_K), seq_tile),
            pl.BlockSpec((tm, AUG_K), seq_tile),
        ],
        out_specs=[
            pl.BlockSpec((None, hh, GQA_GROUP * HEAD_DIM, tm), tok_minor),
            pl.BlockSpec((None, hh, GATE_ROWS, tm), tok_minor),
            pl.BlockSpec((None, hh, tm, AUG_K), tok_major),
            pl.BlockSpec((None, hh, tm, AUG_K), tok_major),
            pl.BlockSpec((None, hh, tm // tq, V_ROWS, tq), tok_tiles),
            pl.BlockSpec((None, hh, tm // tq, V_ROWS, tq), tok_tiles),
            pl.BlockSpec((None, tm // CMP_STRIDE, CMP_STRIDE * 2 * D_KV), row),
            pl.BlockSpec((None, tm, D_CONV), row),
        ],
        out_shape=[
            jax.ShapeDtypeStruct((bsz, hh, GQA_GROUP * HEAD_DIM, seq), BF16),
            jax.ShapeDtypeStruct((bsz, hh, GATE_ROWS, seq), F32),
            jax.ShapeDtypeStruct((bsz, hh, seq, AUG_K), BF16),
            jax.ShapeDtypeStruct((bsz, hh, seq, AUG_K), BF16),
            jax.ShapeDtypeStruct((bsz, hh, seq // tq, V_ROWS, tq), BF16),
            jax.ShapeDtypeStruct((bsz, hh, seq // tq, V_ROWS, tq), BF16),
            jax.ShapeDtypeStruct((bsz, seq // CMP_STRIDE, CMP_STRIDE * 2 * D_KV), BF16),
            jax.ShapeDtypeStruct((bsz, seq, D_CONV), BF16),
        ],
        scratch_shapes=[pltpu.VMEM((SUB_ROWS + 8, D_CONV), F32), pltpu.VMEM((2, SUB_ROWS, D_KV), F32)],
        compiler_params=pltpu.CompilerParams(dimension_semantics=("arbitrary",), vmem_limit_bytes=VMEM_LIMIT),
    )(x3, g, w, cw, gc, csel, cwin)


N_STREAMS = 2 * N_KV_HEADS


def _compress_kernel(x_ref, w1_ref, pos_ref, w2_ref, c_ref, kc_ref, vc_ref, sbuf):
    nc = x_ref.shape[0]
    y = jnp.dot(x_ref[...], w1_ref[...], preferred_element_type=F32)
    pb = jnp.dot(pos_ref[...], w1_ref[...], preferred_element_type=F32)
    rowi = lax.broadcasted_iota(jnp.int32, (nc, 128), 0)
    lane = lax.broadcasted_iota(jnp.int32, (nc, 128), 1)
    sbuf[:, nc:nc + 8, :] = jnp.zeros((N_STREAMS, 8, CMP_HIDDEN), F32)
    for st in range(N_STREAMS):
        c0 = st * 2 * CMP_HIDDEN
        posb = pb[0:1, c0:c0 + CMP_HIDDEN] + pb[1:2, c0 + CMP_HIDDEN:c0 + 2 * CMP_HIDDEN]
        sbuf[st, 0:nc, :] = y[:, c0 + CMP_HIDDEN:c0 + 2 * CMP_HIDDEN]
        hid = y[:, c0:c0 + CMP_HIDDEN] + sbuf[st, 1:nc + 1, :] + posb
        out = jnp.dot(jax.nn.gelu(hid).astype(BF16), w2_ref[st], preferred_element_type=F32)
        out = jnp.where(rowi < nc - 1, out, 0.0)
        if st < N_KV_HEADS:
            kc_ref[st, :, 0:128] = jnp.where(lane < HEAD_DIM, out.astype(BF16), c_ref[:, 0:128])
            kc_ref[st, :, 128:AUG_K] = c_ref[:, 128:AUG_K]
        else:
            vc_ref[st - N_KV_HEADS] = out.T[0:HEAD_DIM, :].astype(BF16)


def _compress_call(xc, w1, pos8, w2, consts):
    b, nc, width = xc.shape
    hh = N_KV_HEADS
    fix2 = lambda i: (0, 0)
    return pl.pallas_call(
        _compress_kernel,
        grid=(b,),
        in_specs=[
            pl.BlockSpec((None, nc, width), lambda i: (i, 0, 0)),
            pl.BlockSpec((width, N_STREAMS * 2 * CMP_HIDDEN), fix2),
            pl.BlockSpec((8, width), fix2),
            pl.BlockSpec((N_STREAMS, CMP_HIDDEN, 128), lambda i: (0, 0, 0)),
            pl.BlockSpec((nc, AUG_K), fix2),
        ],
        out_specs=[
            pl.BlockSpec((None, hh, nc, AUG_K), lambda i: (i, 0, 0, 0)),
            pl.BlockSpec((None, hh, HEAD_DIM, nc), lambda i: (i, 0, 0, 0)),
        ],
        out_shape=[
            jax.ShapeDtypeStruct((b, hh, nc, AUG_K), BF16),
            jax.ShapeDtypeStruct((b, hh, HEAD_DIM, nc), BF16),
        ],
        scratch_shapes=[pltpu.VMEM((N_STREAMS, nc + 8, CMP_HIDDEN), F32)],
        compiler_params=pltpu.CompilerParams(dimension_semantics=("arbitrary",), vmem_limit_bytes=VMEM_LIMIT),
    )(xc, w1, pos8, w2, consts)


def _attn_kernel(qt_ref, gt_ref, ks_ref, kw_ref, vs_ref, vw_ref, kc_ref, vc_ref, msel_ref, cmask_ref, o_ref,
                 qaug, acc_ref, accw_ref, sbuf, wsbuf, ocmp_ref, imp_ref, selm_ref, klist, *, topk):
    h = pl.program_id(1)
    qi = pl.program_id(2)
    tq = qt_ref.shape[1]
    r = GQA_GROUP * tq
    nc = kc_ref.shape[0]
    nsel = msel_ref.shape[0]
    nkb = ks_ref.shape[0]
    t0 = qi * tq

    lane16 = lax.broadcasted_iota(jnp.int32, (AUG_ALIBI_ROWS, r), 1)
    sub16 = lax.broadcasted_iota(jnp.int32, (AUG_ALIBI_ROWS, r), 0)
    gl = lane16 // tq
    off = (lane16 % tq).astype(F32)
    base = jnp.where(h == 0, 0.5, 0.03125).astype(F32)
    slope = jnp.where(gl == 0, base, jnp.where(gl == 1, base * 0.5, jnp.where(gl == 2, base * 0.25, base * 0.125)))
    blk0 = (t0 // SEL_BLOCK).astype(F32)
    c3 = jnp.where(sub16 % 3 == 0, LOG2E_3[0], jnp.where(sub16 % 3 == 1, LOG2E_3[1], LOG2E_3[2]))
    arow = jnp.where(sub16 < 3, slope * c3,
                     jnp.where(sub16 < 6, 64.0 * slope * c3,
                               jnp.where(sub16 == 6, -slope * LOG2E * (64.0 * blk0 + off), 0.0)))
    for g in range(GQA_GROUP):
        qaug[0:HEAD_DIM, g * tq:(g + 1) * tq] = qt_ref[g * HEAD_DIM:(g + 1) * HEAD_DIM, :]
    qaug[AUG_ALIBI:AUG_SEL, :] = arow.astype(BF16)
    qaug[AUG_SEL:AUG_K, :] = jnp.zeros((AUG_K - AUG_SEL, r), BF16)

    heads = [slice(g * tq, (g + 1) * tq) for g in range(GQA_GROUP)]
    q_i = lax.broadcasted_iota(jnp.int32, (tq, tq), 1)
    key_i = lax.broadcasted_iota(jnp.int32, (tq, tq), 0)
    causal = key_i <= q_i
    nwin = WINDOW // tq

    def win_scores(w, cols):
        kb = qi - nwin + w
        s = jnp.dot(kw_ref[jnp.maximum(kb, 0)], qaug[:, cols], preferred_element_type=F32)
        if w == 0:
            return jnp.where((key_i > q_i) & (kb >= 0), s, NEG)
        if w == nwin:
            return jnp.where(causal, s, NEG)
        return jnp.where(kb >= 0, s, NEG)

    def softmax_tile(s, m_old):
        m_new = jnp.maximum(m_old, jnp.max(s, axis=0, keepdims=True))
        return m_new, jnp.exp2(s - m_new).astype(BF16), jnp.exp2(m_old - m_new)

    def win_update(w, cols, s, m_old):
        m_new, p, alpha = softmax_tile(s, m_old)
        pv_w = jnp.dot(vw_ref[jnp.maximum(qi - nwin + w, 0)], p, preferred_element_type=F32)
        accw_ref[:, cols] = pv_w if w == 0 else accw_ref[:, cols] * alpha + pv_w
        return m_new

    mask_c = cmask_ref[...] <= t0
    cmp_s = [jnp.dot(kc_ref[...], qaug[:, cols], preferred_element_type=F32) for cols in heads]
    win_s = [win_scores(0, cols) for cols in heads]
    p_sum = jnp.zeros((nc, tq), F32)
    for g, cols in enumerate(heads):
        sc = jnp.where(mask_c, cmp_s[g], NEG)
        m_c = jnp.max(sc, axis=0, keepdims=True)
        e_c = jnp.exp2(sc - m_c)
        l_c = jnp.sum(e_c, axis=0, keepdims=True)
        p_c = e_c * jnp.where(m_c > 0.5 * NEG, 1.0 / l_c, 0.0)
        ocmp_ref[g * HEAD_DIM:(g + 1) * HEAD_DIM, :] = jnp.dot(vc_ref[...], p_c.astype(BF16),
                                                               preferred_element_type=F32)
        p_sum = p_sum + p_c

    m_w = [jnp.full((1, tq), NEG, F32)] * GQA_GROUP
    for w in range(nwin):
        if w + 1 < nwin:
            nxt = [win_scores(w + 1, cols) for cols in heads]
        else:
            nxt = None
            for cols in heads:
                wsbuf[:, cols] = win_scores(w + 1, cols)
        for g, cols in enumerate(heads):
            m_w[g] = win_update(w, cols, win_s[g], m_w[g])
        win_s = nxt
        if w == 0:
            p1 = p_sum.astype(BF16)
            p2 = (p_sum - p1.astype(F32)).astype(BF16)
            msel = msel_ref[...]
            imp = (jnp.dot(msel, p1, preferred_element_type=F32)
                   + jnp.dot(msel, p2, preferred_element_type=F32))

    jj =lax.broadcasted_iota(jnp.int32, (nsel, tq), 0)
    jt = (t0 + lax.broadcasted_iota(jnp.int32, (nsel, tq), 1)) // SEL_BLOCK
    imp = jnp.where((jj == 0) | (jj == jt) | (jj == jt - 1), BIG, imp)
    imp = jnp.where(jj > jt, NEG, imp)
    imp_ref[...] = imp
    bpt = tq // SEL_BLOCK
    sub8 = lax.broadcasted_iota(jnp.int32, (8, tq), 0)

    def rank_select(nblk):
        groups = [imp_ref[8 * gi:8 * gi + 8, :] for gi in range(nblk // 8)]
        cnts = [jnp.zeros((8, tq), jnp.int32) for _ in groups]
        for jp in range(nblk):
            rowv = jnp.broadcast_to(imp_ref[jp:jp + 1, :], (8, tq))
            for gi, grp in enumerate(groups):
                if 8 * gi > jp:
                    beats = rowv >= grp
                elif 8 * gi + 7 <= jp:
                    beats = rowv > grp
                else:
                    beats = (rowv > grp) | ((rowv == grp) & (sub8 + 8 * gi > jp))
                cnts[gi] = cnts[gi] + jnp.where(beats, 1, 0)
        selm = [jnp.where(c < topk, 1.0, 0.0) for c in cnts] + [jnp.zeros((nsel - nblk, tq), F32)] * (nblk < nsel)
        selm = jnp.concatenate(selm, axis=0)
        selm_ref[...] = selm
        selbias = jnp.where(selm > 0.0, 0.0, NEG).astype(BF16)
        for g in range(GQA_GROUP):
            qaug[AUG_SEL:AUG_SEL + nsel, g * tq:(g + 1) * tq] = selbias

    for idx in range(nsel // 16):
        pl.when((qi * bpt) // 16 == idx)(functools.partial(rank_select, 16 * (idx + 1)))

    for cols in heads:
        sbuf[0, :, cols] = jnp.dot(ks_ref[0], qaug[:, cols], preferred_element_type=F32)
    for g, cols in enumerate(heads):
        win_update(nwin, cols, wsbuf[:, cols], m_w[g])

    n_use = jnp.int32(0)
    for gi in range(nsel // 8):
        hit = jnp.max(selm_ref[8 * gi:8 * gi + 8, :], axis=1, keepdims=True)
        for part in range(8 // bpt):
            kb = gi * (8 // bpt) + part
            if kb < nkb - 1:
                klist[n_use] = kb
                used = (jnp.max(hit[part * bpt:(part + 1) * bpt, :]) > 0.0) & (kb < qi)
                n_use = n_use + used.astype(jnp.int32)
    klist[n_use] = qi

    acc_ref[...] = jnp.zeros(acc_ref.shape, F32)

    def step(j, cur, m_old):
        k_next = ks_ref[klist[j + 1]]
        v_cur = vs_ref[klist[j]]
        for g in range(GQA_GROUP):
            cols = slice(g * tq, (g + 1) * tq)
            sbuf[1 - cur, :, cols] = jnp.dot(k_next, qaug[:, cols], preferred_element_type=F32)
        m_news = []
        for g in range(GQA_GROUP):
            cols = slice(g * tq, (g + 1) * tq)
            m_new, p, alpha = softmax_tile(sbuf[cur, :, cols], m_old[:, cols])
            acc_ref[:, cols] = acc_ref[:, cols] * alpha + jnp.dot(v_cur, p, preferred_element_type=F32)
            m_news.append(m_new)
        return jnp.concatenate(m_news, axis=1)

    def finish(cur, m_old):
        v_last = vs_ref[qi]
        gt = gt_ref[...]
        for g, cols in enumerate(heads):
            _, p_last, alpha_last = softmax_tile(jnp.where(causal, sbuf[cur, :, cols], NEG), m_old[:, cols])
            acc = acc_ref[:, cols] * alpha_last + jnp.dot(v_last, p_last, preferred_element_type=F32)
            acc_w = accw_ref[:, cols]
            o_sel = acc[0:HEAD_DIM, :] * (1.0 / acc[HEAD_DIM:HEAD_DIM + 1, :])
            o_win = acc_w[0:HEAD_DIM, :] * (1.0 / acc_w[HEAD_DIM:HEAD_DIM + 1, :])
            o_ref[g * HEAD_DIM:(g + 1) * HEAD_DIM, :] = (gt[3 * g:3 * g + 1, :] * ocmp_ref[g * HEAD_DIM:(g + 1) * HEAD_DIM, :]
                                                         + gt[3 * g + 1:3 * g + 2, :] * o_sel
                                                         + gt[3 * g + 2:3 * g + 3, :] * o_win)

    def unrolled(i, carry):
        for u in range(SEL_UNROLL):
            carry = step(SEL_UNROLL * i + u, u % 2, carry)
        return carry

    carry0 = jnp.full((1, r), NEG, F32)
    n_main = n_use // SEL_UNROLL
    carry_main = lax.fori_loop(0, n_main, unrolled, carry0)
    for rem in range(SEL_UNROLL):
        @pl.when(n_use % SEL_UNROLL == rem)
        def _(rem=rem):
            carry = carry_main
            for u in range(rem):
                carry = step(SEL_UNROLL * n_main + u, u % 2, carry)
            finish(rem % 2, carry)


def _attn_call(qt, gt, ks, kw, vs, vw, kc, vc, msel, cmask, topk):
    b, hh, _, t = qt.shape
    tq = Q_TILE
    nkb = t // tq
    nc = kc.shape[2]
    nsel = msel.shape[0]
    per_q = lambda i, j, k: (i, j, 0, k)
    per_bh4 = lambda i, j, k: (i, j, 0, 0)
    per_bh5 = lambda i, j, k: (i, j, 0, 0, 0)
    return pl.pallas_call(
        functools.partial(_attn_kernel, topk=topk),
        grid=(b, hh, nkb),
        in_specs=[
            pl.BlockSpec((None, None, GQA_GROUP * HEAD_DIM, tq), per_q),
            pl.BlockSpec((None, None, GATE_ROWS, tq), per_q),
            pl.BlockSpec((None, None, nkb, tq, AUG_K), per_bh5),
            pl.BlockSpec((None, None, nkb, tq, AUG_K), per_bh5),
            pl.BlockSpec((None, None, nkb, V_ROWS, tq), per_bh5),
            pl.BlockSpec((None, None, nkb, V_ROWS, tq), per_bh5),
            pl.BlockSpec((None, None, nc, AUG_K), per_bh4),
            pl.BlockSpec((None, None, HEAD_DIM, nc), per_bh4),
            pl.BlockSpec((nsel, nc), lambda i, j, k: (0, 0)),
            pl.BlockSpec((nc, tq), lambda i, j, k: (0, 0)),
        ],
        out_specs=pl.BlockSpec((None, GQA_GROUP * HEAD_DIM, tq), lambda i, j, k: (i, j, k)),
        out_shape=jax.ShapeDtypeStruct((b, D_ATTN, t), F32),
        scratch_shapes=[pltpu.VMEM((AUG_K, GQA_GROUP * tq), BF16), pltpu.VMEM((V_ROWS, GQA_GROUP * tq), F32),
                        pltpu.VMEM((V_ROWS, GQA_GROUP * tq), F32),
                        pltpu.VMEM((2, tq, GQA_GROUP * tq), F32),
                        pltpu.VMEM((tq, GQA_GROUP * tq), F32), pltpu.VMEM((GQA_GROUP * HEAD_DIM, tq), F32),
                        pltpu.VMEM((nsel, tq), F32), pltpu.VMEM((nsel, tq), F32),
                        pltpu.SMEM((nkb + 1,), jnp.int32)],
        compiler_params=pltpu.CompilerParams(dimension_semantics=("arbitrary", "arbitrary", "arbitrary"),
                                             vmem_limit_bytes=VMEM_LIMIT),
    )(qt, gt, ks, kw, vs, vw, kc, vc, msel, cmask)


def _ffn_kernel(x_ref, oa_ref, mc_ref, ga_ref, wo_ref, gf_ref, wg_ref, wu_ref, cw_ref, cb_ref, wd_ref, gl_ref,
                o_ref, gbuf, ybuf, *, tiles_per_seq):
    i = pl.program_id(0)
    tm = x_ref.shape[0]

    @pl.when(i % tiles_per_seq == 0)
    def _():
        gbuf[0:8, :] = jnp.zeros((8, D_FF), F32)

    @pl.when(i % tiles_per_seq != 0)
    def _():
        gbuf[0:8, :] = gbuf[SUB_ROWS:SUB_ROWS + 8, :]

    subs = [slice(r0, r0 + SUB_ROWS) for r0 in range(0, tm, SUB_ROWS)]
    x1s, h2s = [], []
    for rows in subs:
        oat = oa_ref[:, rows]
        mat = (oat * lax.rsqrt(jnp.mean(oat * oat, axis=0, keepdims=True) + EPS) * ga_ref[...]).astype(BF16)
        x1 = (x_ref[rows, :] + lax.dot_general(mat, wo_ref[0:D_ATTN, :], (((0,), (0,)), ((), ())),
                                                preferred_element_type=F32)
              + jnp.dot(mc_ref[rows, :], wo_ref[D_ATTN:D_MODEL, :], preferred_element_type=F32))
        x1s.append(x1)
        h2s.append(_rms(x1, gf_ref[...]).astype(BF16))
    for rows, x1, h2 in zip(subs, x1s, h2s):
        if rows.start > 0:
            gbuf[0:8, :] = gbuf[SUB_ROWS:SUB_ROWS + 8, :]
        for c in range(D_FF // FF_CHUNK):
            cs = slice(c * FF_CHUNK, (c + 1) * FF_CHUNK)
            gpre = jnp.dot(h2, wg_ref[:, cs], preferred_element_type=F32)
            up = jnp.dot(h2, wu_ref[:, cs], preferred_element_type=F32)
            gbuf[8:8 + SUB_ROWS, cs] = gpre
            gate = (cw_ref[0:1, cs] * gbuf[6:6 + SUB_ROWS, cs] + cw_ref[1:2, cs] * gbuf[7:7 + SUB_ROWS, cs]
                    + cw_ref[2:3, cs] * gpre + cb_ref[:, cs])
            ybuf[:, cs] = (jax.nn.silu(gate) * up).astype(BF16)
        acc = x1 + jnp.dot(ybuf[...], wd_ref[...], preferred_element_type=F32)
        o_ref[rows, :] = _rms(acc, gl_ref[...])


def _ffn_call(x2, oat, mc, ga, wo, gf, wg, wu, cw, cb, wd, gl):
    n = x2.shape[0]
    seq = oat.shape[2]
    tm = ROW_TILE
    tps = seq // tm
    row = lambda i: (i, 0)
    fix = lambda i: (0, 0)
    once = dict(pipeline_mode=pl.Buffered(1))
    return pl.pallas_call(
        functools.partial(_ffn_kernel, tiles_per_seq=seq // tm),
        grid=(n // tm,),
        in_specs=[
            pl.BlockSpec((tm, D_MODEL), row),
            pl.BlockSpec((None, D_ATTN, tm), lambda i: (i // tps, 0, i % tps)),
            pl.BlockSpec((tm, D_CONV), row),
            pl.BlockSpec((D_ATTN, 1), fix),
            pl.BlockSpec((D_MODEL, D_MODEL), fix, **once),
            pl.BlockSpec((1, D_MODEL), fix),
            pl.BlockSpec((D_MODEL, D_FF), fix, **once),
            pl.BlockSpec((D_MODEL, D_FF), fix, **once),
            pl.BlockSpec((3, D_FF), fix),
            pl.BlockSpec((1, D_FF), fix),
            pl.BlockSpec((D_FF, D_MODEL), fix, **once),
            pl.BlockSpec((1, D_MODEL), fix),
        ],
        out_specs=pl.BlockSpec((tm, D_MODEL), row),
        out_shape=jax.ShapeDtypeStruct((n, D_MODEL), F32),
        scratch_shapes=[pltpu.VMEM((SUB_ROWS + 8, D_FF), F32), pltpu.VMEM((SUB_ROWS, D_FF), BF16)],
        compiler_params=pltpu.CompilerParams(dimension_semantics=("arbitrary",), vmem_limit_bytes=VMEM_LIMIT),
    )(x2, oat, mc, ga, wo, gf, wg, wu, cw, cb, wd, gl)


def _alibi_cols(pos):
    cols = np.zeros((len(pos), AUG_ALIBI_ROWS), np.float32)
    cols[:, 0:3] = (pos % SEL_BLOCK)[:, None]
    cols[:, 3:6] = (pos // SEL_BLOCK)[:, None]
    cols[:, 6] = 1.0
    return cols


def _key_consts(t, with_sel):
    pos = np.arange(t)
    c = np.zeros((t, AUG_K), np.float32)
    c[:, AUG_ALIBI:AUG_SEL] = _alibi_cols(pos)
    if with_sel:
        c[pos, AUG_SEL + pos // SEL_BLOCK] = 1.0
    return jnp.asarray(c, BF16)


def _cmp_consts(nc):
    c = np.zeros((nc, AUG_K), np.float32)
    c[:, AUG_ALIBI:AUG_SEL] = _alibi_cols(np.arange(nc) * CMP_STRIDE + (CMP_BLOCK - 1))
    return jnp.asarray(c, BF16)


def _compress_weights(w_k1, w_v1, pos_k, pos_v, w_k2, w_v2):
    hh, dk, half = N_KV_HEADS, HEAD_DIM, CMP_STRIDE
    w1 = jnp.zeros((half, N_STREAMS, dk, N_STREAMS, 2, CMP_HIDDEN), F32)
    pos = []
    for st in range(N_STREAMS):
        w = (w_k1 if st < hh else w_v1).reshape(2, half, dk, CMP_HIDDEN)
        w1 = w1.at[:, st, :, st, :, :].set(w.transpose(1, 2, 0, 3))
        pos.append((pos_k if st < hh else pos_v).reshape(2, half, dk))
    pos = jnp.stack(pos, axis=2).reshape(2, half * N_STREAMS * dk)
    pos8 = jnp.concatenate([pos, jnp.zeros((6, pos.shape[1]), F32)], axis=0)
    w2 = jnp.stack([jnp.concatenate([w_k2 if st < hh else w_v2, jnp.zeros((CMP_HIDDEN, 128 - dk), F32)], axis=1)
                    for st in range(N_STREAMS)])
    w1 = w1.reshape(half * N_STREAMS * dk, N_STREAMS * 2 * CMP_HIDDEN)
    return w1.astype(BF16), pos8.astype(BF16), w2.astype(BF16)


def _cmp_limits(nc):
    end = np.arange(nc)[:, None] * CMP_STRIDE + (CMP_BLOCK - 1)
    off = np.arange(Q_TILE)[None, :]
    return jnp.asarray(end - off, jnp.int32)


def _sel_map_t(t, nc):
    n_cmp = (t - CMP_BLOCK) // CMP_STRIDE + 1
    n_sel = t // SEL_BLOCK
    cs = np.arange(n_cmp)[:, None] * CMP_STRIDE
    ss = np.arange(n_sel)[None, :] * SEL_BLOCK
    ov = np.maximum(0, np.minimum(cs + CMP_BLOCK, ss + SEL_BLOCK) - np.maximum(cs, ss)) / CMP_BLOCK
    m = np.zeros((n_sel, nc), np.float32)
    m[:, :n_cmp] = ov.T
    return jnp.asarray(m, BF16)


def kernel(x, norm_mix_g, w_in, pos_ck, w_ck1, w_ck2, pos_cv, w_cv1, w_cv2, conv_mix_w, norm_out_attn_g,
           norm_out_conv_g, w_out, norm_ffn_g, w_gate, w_up, ffn_conv_w, ffn_conv_b, w_down, norm_final_g):
    b, t, _ = x.shape
    hh, dk = N_KV_HEADS, HEAD_DIM
    assert t % ROW_TILE == 0 and t % Q_TILE == 0 and WINDOW % Q_TILE == 0 and t // SEL_BLOCK <= AUG_K - AUG_SEL
    assert ROW_TILE % SUB_ROWS == 0 and SUB_ROWS % Q_TILE == 0 and (t // SEL_BLOCK) % 16 == 0
    nc = t // CMP_STRIDE
    nsel = t // SEL_BLOCK
    nkb = t // Q_TILE
    depth = w_in.shape[0]
    assert depth == 1
    xx = x
    for l in range(depth):
        wi = w_in[l]
        col = np.cumsum([0, D_ATTN] + [D_KV] * 6 + [3 * N_HEADS_ATTN] + [D_CONV] * 3)
        kv_slabs = [wi[:, col[3 + 2 * br] + hd * dk:col[3 + 2 * br] + (hd + 1) * dk] if part == 0 else
                    wi[:, col[4 + 2 * br] + hd * dk:col[4 + 2 * br] + (hd + 1) * dk]
                    for br in range(2) for hd in range(hh) for part in range(2)]
        gate_cols = [jnp.concatenate([wi[:, col[7] + 12 * hd:col[7] + 12 * (hd + 1)], jnp.zeros((D_MODEL, 4), F32)], axis=1)
                     for hd in range(hh)]
        w_p = jnp.concatenate([wi[:, 0:D_ATTN]] + kv_slabs + [wi[:, col[1]:col[3]], wi[:, col[8]:col[11]]] + gate_cols
                              + [jnp.zeros((D_MODEL, 128 - 2 * GATE_ROWS), F32)], axis=1).astype(BF16)
        assert w_p.shape[1] == PROJ_COLS
        qt, gt, ksa, kwa, vst, vwt, kvc, mixed_conv = _proj_call(
            xx, norm_mix_g[l][None], w_p, conv_mix_w[l], norm_out_conv_g[l][None],
            _key_consts(t, True), _key_consts(t, False))

        kc, vc = _compress_call(kvc, *_compress_weights(w_ck1[l], w_cv1[l], pos_ck[l], pos_cv[l], w_ck2[l], w_cv2[l]),
                                _cmp_consts(nc))

        o_attn = _attn_call(qt, gt, ksa.reshape(b, hh, nkb, Q_TILE, AUG_K), kwa.reshape(b, hh, nkb, Q_TILE, AUG_K),
                            vst, vwt, kc, vc, _sel_map_t(t, nc), _cmp_limits(nc), min(SEL_TOPK, nsel))

        xx = _ffn_call(xx.reshape(b * t, D_MODEL), o_attn, mixed_conv.reshape(b * t, D_CONV),
                       norm_out_attn_g[l][:, None], w_out[l].astype(BF16), norm_ffn_g[l][None], w_gate[l].astype(BF16),
                       w_up[l].astype(BF16), ffn_conv_w[l], ffn_conv_b[l][None], w_down[l].astype(BF16),
                       norm_final_g[None])
    return xx.reshape(b, t, D_MODEL)
```
